```python
import jax, jax.numpy as jnp
from jax import lax
import numpy as np

D_MODEL = 1024
BATCH = 8
SEQ = 8192
DEPTH = 4

N_META = 16
MLSTM_HEADS = 4
MLSTM_DQK = 128
MLSTM_DV = 256
MLSTM_CHUNK = 64
QK_CONV_WIDTH = 4
GATE_SOFTCAP = 15.0
SB_HEADS = 4
SB_DH = 128
SB_BLOCK = 128
PAD_FRONT = SB_BLOCK - N_META
CONV_WIDTH = 31
FFN_HIDDEN = -(-8 * D_MODEL // (3 * 256)) * 256
MQK = MLSTM_HEADS * MLSTM_DQK
MV = MLSTM_HEADS * MLSTM_DV
SBW = SB_HEADS * SB_DH
IN_SIZES = (2 * MQK, MV, MV, 2 * MLSTM_HEADS, SBW, SBW, SBW)
IN_WIDTH = sum(IN_SIZES)
MIX_WIDTH = MV + SBW
N_EVEN = (DEPTH + 1) // 2
N_ODD = DEPTH // 2
NEG = -1e30
EPS = 1e-6

kernel_name = 'hybrid_mlstm_stickbreaking_conformer'


def rms_norm(x, g):
    xf = x.astype(jnp.float32)
    y = xf * lax.rsqrt(jnp.mean(xf * xf, axis=-1, keepdims=True) + EPS)
    return (y * g.astype(jnp.float32)).astype(x.dtype)


def layer_norm(x, g, b):
    xf = x.astype(jnp.float32)
    mu = jnp.mean(xf, axis=-1, keepdims=True)
    var = jnp.mean(jnp.square(xf - mu), axis=-1, keepdims=True)
    y = (xf - mu) * lax.rsqrt(var + EPS)
    return (y * g.astype(jnp.float32) + b.astype(jnp.float32)).astype(x.dtype)


def causal_depthwise_conv(x, w, b):
    k = w.shape[0]
    y = lax.conv_general_dilated(x, w[:, None, :].astype(x.dtype), (1,), [(k - 1, 0)],
                                 dimension_numbers=('NWC', 'WIO', 'NWC'),
                                 feature_group_count=x.shape[-1])
    return y + b.astype(x.dtype)


def to_chunks(a):
    b, h, t = a.shape[:3]
    a = a.reshape(b, h, t // MLSTM_CHUNK, MLSTM_CHUNK, *a.shape[3:])
    return jnp.moveaxis(a, 2, 0)


def mlstm_chunkwise(q, k, v, log_i, log_f):
    b, h, t, dk = q.shape
    dv = v.shape[-1]
    tril = jnp.tril(jnp.ones((MLSTM_CHUNK, MLSTM_CHUNK), dtype=bool))

    def step(carry, xs):
        c_st, n_st, m_st = carry
        qc, kc, vc, li, lf = xs
        bcum = jnp.cumsum(lf, axis=-1)
        g = bcum[..., -1]
        dmat = bcum[..., :, None] - bcum[..., None, :] + li[..., None, :]
        dmat = jnp.where(tril, dmat, NEG)
        inter = bcum + m_st[..., None]
        m_t = jnp.maximum(inter, jnp.max(dmat, axis=-1))
        w_intra = jnp.exp(dmat - m_t[..., None])
        w_inter = jnp.exp(inter - m_t)
        s = jnp.einsum('bhtd,bhsd->bhts', qc, kc) * w_intra
        num = jnp.einsum('bhts,bhsv->bhtv', s, vc) + w_inter[..., None] * jnp.einsum('bhtd,bhdv->bhtv', qc, c_st)
        den = jnp.sum(s, axis=-1) + w_inter * jnp.einsum('bhtd,bhd->bht', qc, n_st)
        h_out = num / jnp.maximum(jnp.abs(den), jnp.exp(-m_t))[..., None]
        a = g[..., None] - bcum + li
        m_new = jnp.maximum(g + m_st, jnp.max(a, axis=-1))
        wa = jnp.exp(a - m_new[..., None])
        wc = jnp.exp(g + m_st - m_new)
        c_new = wc[..., None, None] * c_st + jnp.einsum('bhs,bhsd,bhsv->bhdv', wa, kc, vc)
        n_new = wc[..., None] * n_st + jnp.einsum('bhs,bhsd->bhd', wa, kc)
        return (c_new, n_new, m_new), h_out

    init = (jnp.zeros((b, h, dk, dv), jnp.float32), jnp.zeros((b, h, dk), jnp.float32),
            jnp.zeros((b, h), jnp.float32))
    xs = (to_chunks(q), to_chunks(k), to_chunks(v), to_chunks(log_i), to_chunks(log_f))
    _, hs = lax.scan(step, init, xs)
    return jnp.moveaxis(hs, 0, 2).reshape(b, h, t, dv)


def stick_breaking(q, k, v, valid):
    b, h, t, d = q.shape
    nb = t // SB_BLOCK
    scale = d ** -0.5
    r = jnp.arange(SB_BLOCK)
    rev_in = (r[:, None] >= r[None, :]).astype(jnp.float32)
    outs = []
    for i in range(nb):
        nk = i + 1
        end = nk * SB_BLOCK
        qi = q[:, :, i * SB_BLOCK:end]
        kb = k[:, :, :end].reshape(b, h, nk, SB_BLOCK, d)
        vb = v[:, :, :end].reshape(b, h, nk, SB_BLOCK, d)
        z = jnp.einsum('bhtd,bhnsd->bhtns', qi, kb).astype(jnp.float32) * scale
        t_idx = i * SB_BLOCK + r
        s_idx = jnp.arange(end).reshape(nk, SB_BLOCK)
        mask = (s_idx[None] < t_idx[:, None, None]) & valid[:end].reshape(nk, SB_BLOCK)[None]
        log1m = jnp.where(mask, -jax.nn.softplus(z), 0.0)
        within = jnp.einsum('bhtns,su->bhtnu', log1m, rev_in, precision=lax.Precision.HIGHEST)
        n_r = jnp.arange(nk)
        rev_blk = (n_r[:, None] > n_r[None, :]).astype(jnp.float32)
        across = jnp.einsum('bhtn,nm->bhtm', within[..., 0], rev_blk, precision=lax.Precision.HIGHEST)
        log_w = jnp.where(mask, z + within + across[..., None], NEG)
        w = jnp.exp(log_w)
        outs.append(jnp.einsum('bhtns,bhnsd->bhtd', w, vb.astype(jnp.float32)))
    return jnp.concatenate(outs, axis=2)


def mlstm_sb_mixer(u, w_in, qk_conv_w, qk_conv_b, gate_b, hnorm_g, w_out):
    b, t, _ = u.shape
    tp = t + PAD_FRONT
    up = jnp.pad(u, ((0, 0), (PAD_FRONT, 0), (0, 0)))
    valid = jnp.arange(tp) >= PAD_FRONT
    proj = up @ w_in
    qk_m, v_m, o_m, gates, q_s, k_s, v_s = jnp.split(proj, np.cumsum(IN_SIZES)[:-1].tolist(), axis=-1)
    qk_m = jax.nn.silu(causal_depthwise_conv(qk_m, qk_conv_w, qk_conv_b))
    q_m, k_m = jnp.split(qk_m, 2, axis=-1)
    gates = gates.astype(jnp.float32) + gate_b.astype(jnp.float32)
    gates = GATE_SOFTCAP * jnp.tanh(gates / GATE_SOFTCAP)
    log_i = jnp.where(valid[:, None], gates[..., :MLSTM_HEADS], NEG)
    log_f = jnp.where(valid[:, None], jax.nn.log_sigmoid(gates[..., MLSTM_HEADS:]), 0.0)

    def heads(a, nh):
        return a.reshape(b, tp, nh, -1).transpose(0, 2, 1, 3)

    h_m = mlstm_chunkwise(heads(q_m, MLSTM_HEADS) * MLSTM_DQK ** -0.5, heads(k_m, MLSTM_HEADS),
                          heads(v_m, MLSTM_HEADS), log_i.transpose(0, 2, 1), log_f.transpose(0, 2, 1))
    h_m = rms_norm(h_m.transpose(0, 2, 1, 3), hnorm_g.reshape(MLSTM_HEADS, MLSTM_DV)).reshape(b, tp, MV)
    h_m = h_m * jax.nn.sigmoid(o_m.astype(jnp.float32))
    h_s = stick_breaking(heads(q_s, SB_HEADS), heads(k_s, SB_HEADS), heads(v_s, SB_HEADS), valid)
    h_s = h_s.transpose(0, 2, 1, 3).reshape(b, tp, SBW)
    mixed = jnp.concatenate([h_m, h_s], axis=-1)[:, PAD_FRONT:].astype(u.dtype)
    return mixed @ w_out


def conformer_conv(u, w_pw1, b_pw1, w_dw, b_dw, ln_g, ln_b, w_pw2, b_pw2):
    a, gate = jnp.split(u @ w_pw1 + b_pw1, 2, axis=-1)
    y = a * jax.nn.sigmoid(gate)
    y = causal_depthwise_conv(y, w_dw, b_dw)
    y = jax.nn.silu(layer_norm(y, ln_g, ln_b))
    return y @ w_pw2 + b_pw2


def swiglu(u, w_gate, w_up, w_down):
    return (jax.nn.silu(u @ w_gate) * (u @ w_up)) @ w_down


def _fwd_setup_inputs(seed: int = 0) -> dict:
    key = jax.random.key(seed)
    ks = jax.random.split(key, 24)
    f32 = jnp.float32
    nrm = lambda k, shape, s: jax.random.normal(k, shape, f32) * s
    gate_noise = nrm(ks[5], (N_EVEN, 2 * MLSTM_HEADS), 0.1)
    gate_center = jnp.concatenate([jnp.full((MLSTM_HEADS,), -2.0, f32), jnp.full((MLSTM_HEADS,), 3.0, f32)])
    return {
        'x': nrm(ks[0], (BATCH, SEQ, D_MODEL), 1.0),
        'meta': nrm(ks[1], (N_META, D_MODEL), 1.0),
        'norm_g': 1.0 + nrm(ks[2], (DEPTH, 4, D_MODEL), 0.02),
        'mix_w_in': nrm(ks[3], (N_EVEN, D_MODEL, IN_WIDTH), D_MODEL ** -0.5),
        'mix_qk_conv_w': nrm(ks[4], (N_EVEN, QK_CONV_WIDTH, 2 * MQK), QK_CONV_WIDTH ** -0.5),
        'mix_qk_conv_b': nrm(ks[6], (N_EVEN, 2 * MQK), 0.02),
        'mix_gate_b': gate_center + gate_noise,
        'mix_hnorm_g': 1.0 + nrm(ks[7], (N_EVEN, MV), 0.02),
        'mix_w_out': nrm(ks[8], (N_EVEN, MIX_WIDTH, D_MODEL), MIX_WIDTH ** -0.5),
        'conv_w_pw1': nrm(ks[9], (N_ODD, D_MODEL, 2 * D_MODEL), D_MODEL ** -0.5),
        'conv_b_pw1': nrm(ks[10], (N_ODD, 2 * D_MODEL), 0.02),
        'conv_w_dw': nrm(ks[11], (N_ODD, CONV_WIDTH, D_MODEL), CONV_WIDTH ** -0.5),
        'conv_b_dw': nrm(ks[12], (N_ODD, D_MODEL), 0.02),
        'conv_ln_g': 1.0 + nrm(ks[13], (N_ODD, D_MODEL), 0.02),
        'conv_ln_b': nrm(ks[14], (N_ODD, D_MODEL), 0.02),
        'conv_w_pw2': nrm(ks[15], (N_ODD, D_MODEL, D_MODEL), D_MODEL ** -0.5),
        'conv_b_pw2': nrm(ks[16], (N_ODD, D_MODEL), 0.02),
        'ffn_w_gate': nrm(ks[17], (DEPTH, D_MODEL, FFN_HIDDEN), D_MODEL ** -0.5),
        'ffn_w_up': nrm(ks[18], (DEPTH, D_MODEL, FFN_HIDDEN), D_MODEL ** -0.5),
        'ffn_w_down': nrm(ks[19], (DEPTH, FFN_HIDDEN, D_MODEL), FFN_HIDDEN ** -0.5),
    }


def _fwd_reference(x, meta, norm_g, mix_w_in, mix_qk_conv_w, mix_qk_conv_b, mix_gate_b, mix_hnorm_g,
              mix_w_out, conv_w_pw1, conv_b_pw1, conv_w_dw, conv_b_dw, conv_ln_g, conv_ln_b,
              conv_w_pw2, conv_b_pw2, ffn_w_gate, ffn_w_up, ffn_w_down):
    b = x.shape[0]
    h = jnp.concatenate([jnp.broadcast_to(meta[None].astype(x.dtype), (b, N_META, D_MODEL)), x], axis=1)
    for layer in range(DEPTH):
        g = norm_g[layer]
        u = rms_norm(h, g[0])
        i = layer // 2
        if layer % 2 == 0:
            y = mlstm_sb_mixer(u, mix_w_in[i], mix_qk_conv_w[i], mix_qk_conv_b[i], mix_gate_b[i],
                               mix_hnorm_g[i], mix_w_out[i])
        else:
            y = conformer_conv(u, conv_w_pw1[i], conv_b_pw1[i], conv_w_dw[i], conv_b_dw[i],
                               conv_ln_g[i], conv_ln_b[i], conv_w_pw2[i], conv_b_pw2[i])
        h = h + rms_norm(y, g[1])
        f = swiglu(rms_norm(h, g[2]), ffn_w_gate[layer], ffn_w_up[layer], ffn_w_down[layer])
        h = h + rms_norm(f, g[3])
    return h[:, N_META:]


import jax as _jax
import jax.numpy as _jnp

TWIN_FORMAT = 'train_step'
FWD_PARAMS = ['x', 'meta', 'norm_g', 'mix_w_in', 'mix_qk_conv_w', 'mix_qk_conv_b', 'mix_gate_b', 'mix_hnorm_g', 'mix_w_out', 'conv_w_pw1', 'conv_b_pw1', 'conv_w_dw', 'conv_b_dw', 'conv_ln_g', 'conv_ln_b', 'conv_w_pw2', 'conv_b_pw2', 'ffn_w_gate', 'ffn_w_up', 'ffn_w_down']
TWIN_WEIGHTS = ['meta', 'norm_g', 'mix_w_in', 'mix_qk_conv_w', 'mix_qk_conv_b', 'mix_gate_b', 'mix_hnorm_g', 'mix_w_out', 'conv_w_pw1', 'conv_b_pw1', 'conv_w_dw', 'conv_b_dw', 'conv_ln_g', 'conv_ln_b', 'conv_w_pw2', 'conv_b_pw2', 'ffn_w_gate', 'ffn_w_up', 'ffn_w_down']
TWIN_DIFF_INPUT = 'x'
TWIN_INPUTS = ['x', 'meta', 'norm_g', 'mix_w_in', 'mix_qk_conv_w', 'mix_qk_conv_b', 'mix_gate_b', 'mix_hnorm_g', 'mix_w_out', 'conv_w_pw1', 'conv_b_pw1', 'conv_w_dw', 'conv_b_dw', 'conv_ln_g', 'conv_ln_b', 'conv_w_pw2', 'conv_b_pw2', 'ffn_w_gate', 'ffn_w_up', 'ffn_w_down', 'loss_target', 'm_meta', 'm_norm_g', 'm_mix_w_in', 'm_mix_qk_conv_w', 'm_mix_qk_conv_b', 'm_mix_gate_b', 'm_mix_hnorm_g', 'm_mix_w_out', 'm_conv_w_pw1', 'm_conv_b_pw1', 'm_conv_w_dw', 'm_conv_b_dw', 'm_conv_ln_g', 'm_conv_ln_b', 'm_conv_w_pw2', 'm_conv_b_pw2', 'm_ffn_w_gate', 'm_ffn_w_up', 'm_ffn_w_down', 'v_meta', 'v_norm_g', 'v_mix_w_in', 'v_mix_qk_conv_w', 'v_mix_qk_conv_b', 'v_mix_gate_b', 'v_mix_hnorm_g', 'v_mix_w_out', 'v_conv_w_pw1', 'v_conv_b_pw1', 'v_conv_w_dw', 'v_conv_b_dw', 'v_conv_ln_g', 'v_conv_ln_b', 'v_conv_w_pw2', 'v_conv_b_pw2', 'v_ffn_w_gate', 'v_ffn_w_up', 'v_ffn_w_down']
TWIN_OUTPUTS = ['loss', 'grad_x', 'grad_meta', 'grad_norm_g', 'grad_mix_w_in', 'grad_mix_qk_conv_w', 'grad_mix_qk_conv_b', 'grad_mix_gate_b', 'grad_mix_hnorm_g', 'grad_mix_w_out', 'grad_conv_w_pw1', 'grad_conv_b_pw1', 'grad_conv_w_dw', 'grad_conv_b_dw', 'grad_conv_ln_g', 'grad_conv_ln_b', 'grad_conv_w_pw2', 'grad_conv_b_pw2', 'grad_ffn_w_gate', 'grad_ffn_w_up', 'grad_ffn_w_down', 'delta_meta', 'delta_norm_g', 'delta_mix_w_in', 'delta_mix_qk_conv_w', 'delta_mix_qk_conv_b', 'delta_mix_gate_b', 'delta_mix_hnorm_g', 'delta_mix_w_out', 'delta_conv_w_pw1', 'delta_conv_b_pw1', 'delta_conv_w_dw', 'delta_conv_b_dw', 'delta_conv_ln_g', 'delta_conv_ln_b', 'delta_conv_w_pw2', 'delta_conv_b_pw2', 'delta_ffn_w_gate', 'delta_ffn_w_up', 'delta_ffn_w_down', 'new_m_meta', 'new_m_norm_g', 'new_m_mix_w_in', 'new_m_mix_qk_conv_w', 'new_m_mix_qk_conv_b', 'new_m_mix_gate_b', 'new_m_mix_hnorm_g', 'new_m_mix_w_out', 'new_m_conv_w_pw1', 'new_m_conv_b_pw1', 'new_m_conv_w_dw', 'new_m_conv_b_dw', 'new_m_conv_ln_g', 'new_m_conv_ln_b', 'new_m_conv_w_pw2', 'new_m_conv_b_pw2', 'new_m_ffn_w_gate', 'new_m_ffn_w_up', 'new_m_ffn_w_down', 'new_v_meta', 'new_v_norm_g', 'new_v_mix_w_in', 'new_v_mix_qk_conv_w', 'new_v_mix_qk_conv_b', 'new_v_mix_gate_b', 'new_v_mix_hnorm_g', 'new_v_mix_w_out', 'new_v_conv_w_pw1', 'new_v_conv_b_pw1', 'new_v_conv_w_dw', 'new_v_conv_b_dw', 'new_v_conv_ln_g', 'new_v_conv_ln_b', 'new_v_conv_w_pw2', 'new_v_conv_b_pw2', 'new_v_ffn_w_gate', 'new_v_ffn_w_up', 'new_v_ffn_w_down']
TWIN_LEAF_KINDS = {'loss': 'loss', 'grad_x': 'grad_x', 'grad_meta': 'grad_w', 'grad_norm_g': 'grad_w', 'grad_mix_w_in': 'grad_w', 'grad_mix_qk_conv_w': 'grad_w', 'grad_mix_qk_conv_b': 'grad_w', 'grad_mix_gate_b': 'grad_w', 'grad_mix_hnorm_g': 'grad_w', 'grad_mix_w_out': 'grad_w', 'grad_conv_w_pw1': 'grad_w', 'grad_conv_b_pw1': 'grad_w', 'grad_conv_w_dw': 'grad_w', 'grad_conv_b_dw': 'grad_w', 'grad_conv_ln_g': 'grad_w', 'grad_conv_ln_b': 'grad_w', 'grad_conv_w_pw2': 'grad_w', 'grad_conv_b_pw2': 'grad_w', 'grad_ffn_w_gate': 'grad_w', 'grad_ffn_w_up': 'grad_w', 'grad_ffn_w_down': 'grad_w', 'delta_meta': 'delta_w', 'delta_norm_g': 'delta_w', 'delta_mix_w_in': 'delta_w', 'delta_mix_qk_conv_w': 'delta_w', 'delta_mix_qk_conv_b': 'delta_w', 'delta_mix_gate_b': 'delta_w', 'delta_mix_hnorm_g': 'delta_w', 'delta_mix_w_out': 'delta_w', 'delta_conv_w_pw1': 'delta_w', 'delta_conv_b_pw1': 'delta_w', 'delta_conv_w_dw': 'delta_w', 'delta_conv_b_dw': 'delta_w', 'delta_conv_ln_g': 'delta_w', 'delta_conv_ln_b': 'delta_w', 'delta_conv_w_pw2': 'delta_w', 'delta_conv_b_pw2': 'delta_w', 'delta_ffn_w_gate': 'delta_w', 'delta_ffn_w_up': 'delta_w', 'delta_ffn_w_down': 'delta_w', 'new_m_meta': 'new_m', 'new_m_norm_g': 'new_m', 'new_m_mix_w_in': 'new_m', 'new_m_mix_qk_conv_w': 'new_m', 'new_m_mix_qk_conv_b': 'new_m', 'new_m_mix_gate_b': 'new_m', 'new_m_mix_hnorm_g': 'new_m', 'new_m_mix_w_out': 'new_m', 'new_m_conv_w_pw1': 'new_m', 'new_m_conv_b_pw1': 'new_m', 'new_m_conv_w_dw': 'new_m', 'new_m_conv_b_dw': 'new_m', 'new_m_conv_ln_g': 'new_m', 'new_m_conv_ln_b': 'new_m', 'new_m_conv_w_pw2': 'new_m', 'new_m_conv_b_pw2': 'new_m', 'new_m_ffn_w_gate': 'new_m', 'new_m_ffn_w_up': 'new_m', 'new_m_ffn_w_down': 'new_m', 'new_v_meta': 'new_v', 'new_v_norm_g': 'new_v', 'new_v_mix_w_in': 'new_v', 'new_v_mix_qk_conv_w': 'new_v', 'new_v_mix_qk_conv_b': 'new_v', 'new_v_mix_gate_b': 'new_v', 'new_v_mix_hnorm_g': 'new_v', 'new_v_mix_w_out': 'new_v', 'new_v_conv_w_pw1': 'new_v', 'new_v_conv_b_pw1': 'new_v', 'new_v_conv_w_dw': 'new_v', 'new_v_conv_b_dw': 'new_v', 'new_v_conv_ln_g': 'new_v', 'new_v_conv_ln_b': 'new_v', 'new_v_conv_w_pw2': 'new_v', 'new_v_conv_b_pw2': 'new_v', 'new_v_ffn_w_gate': 'new_v', 'new_v_ffn_w_up': 'new_v', 'new_v_ffn_w_down': 'new_v'}


def _forward(args):
    return _fwd_reference(*[args[k] for k in FWD_PARAMS])


def _output_shape():
    out = _jax.eval_shape(lambda: _forward(_fwd_setup_inputs(0)))
    return out.shape, out.dtype

N_MICROBATCH = 1
ADAM_LR = 0.001
ADAM_B1 = 0.9
ADAM_B2 = 0.999
ADAM_EPS = 1e-08
ADAM_WD = 0.01
ADAM_STEP = 10
PER_EXAMPLE_BATCH_AXIS = {'x': 0, 'loss_target': 0}
SHARED_INPUTS = []
_WEIGHT_DTYPES = {'meta': _jnp.float32, 'norm_g': _jnp.float32, 'mix_w_in': _jnp.float32, 'mix_qk_conv_w': _jnp.float32, 'mix_qk_conv_b': _jnp.float32, 'mix_gate_b': _jnp.float32, 'mix_hnorm_g': _jnp.float32, 'mix_w_out': _jnp.float32, 'conv_w_pw1': _jnp.float32, 'conv_b_pw1': _jnp.float32, 'conv_w_dw': _jnp.float32, 'conv_b_dw': _jnp.float32, 'conv_ln_g': _jnp.float32, 'conv_ln_b': _jnp.float32, 'conv_w_pw2': _jnp.float32, 'conv_b_pw2': _jnp.float32, 'ffn_w_gate': _jnp.float32, 'ffn_w_up': _jnp.float32, 'ffn_w_down': _jnp.float32}
MOMENT_SCALE = {'meta': 3.840990e-01, 'norm_g': 4.294019e+01, 'mix_w_in': 2.386644e+00, 'mix_qk_conv_w': 1.701297e+00, 'mix_qk_conv_b': 1.937117e+00, 'mix_gate_b': 1.101480e+01, 'mix_hnorm_g': 4.159149e+00, 'mix_w_out': 4.831479e+00, 'conv_w_pw1': 1.642123e+00, 'conv_b_pw1': 2.147837e+01, 'conv_w_dw': 3.355687e+00, 'conv_b_dw': 5.864030e+01, 'conv_ln_g': 2.058805e+01, 'conv_ln_b': 3.264029e+01, 'conv_w_pw2': 1.191661e+01, 'conv_b_pw2': 6.693024e+01, 'ffn_w_gate': 9.751468e-01, 'ffn_w_up': 1.315089e+00, 'ffn_w_down': 2.207435e+00}


def _to_microbatches(a, axis):
    t = _jnp.moveaxis(a, axis, 0)
    t = t.reshape((N_MICROBATCH, t.shape[0] // N_MICROBATCH) + t.shape[1:])
    return _jnp.moveaxis(t, 1, axis + 1)


def setup_inputs(seed: int = 0) -> dict:
    inp = _fwd_setup_inputs(seed)
    key = _jax.random.fold_in(_jax.random.key(seed), 7919)
    shape, _ = _output_shape()
    out = dict(inp)
    out["loss_target"] = _jax.random.normal(_jax.random.fold_in(key, 0), shape, _jnp.float32)
    for i, name in enumerate(TWIN_WEIGHTS):
        w = inp[name].astype(_jnp.float32)
        if MOMENT_SCALE is None:
            s = _jnp.sqrt(_jnp.mean(_jnp.square(w)) + 1e-30)
        else:
            s = MOMENT_SCALE[name]
        km, kv = _jax.random.split(_jax.random.fold_in(key, i + 1))
        out[name] = w
        out["m_" + name] = s * _jax.random.normal(km, w.shape, _jnp.float32)
        out["v_" + name] = (s * s) * _jax.random.uniform(kv, w.shape, _jnp.float32, 0.5, 1.5)
    if N_MICROBATCH > 1:
        for name, axis in PER_EXAMPLE_BATCH_AXIS.items():
            out[name] = _to_microbatches(out[name], axis)
    return {'x': out['x'], 'meta': out['meta'], 'norm_g': out['norm_g'], 'mix_w_in': out['mix_w_in'], 'mix_qk_conv_w': out['mix_qk_conv_w'], 'mix_qk_conv_b': out['mix_qk_conv_b'], 'mix_gate_b': out['mix_gate_b'], 'mix_hnorm_g': out['mix_hnorm_g'], 'mix_w_out': out['mix_w_out'], 'conv_w_pw1': out['conv_w_pw1'], 'conv_b_pw1': out['conv_b_pw1'], 'conv_w_dw': out['conv_w_dw'], 'conv_b_dw': out['conv_b_dw'], 'conv_ln_g': out['conv_ln_g'], 'conv_ln_b': out['conv_ln_b'], 'conv_w_pw2': out['conv_w_pw2'], 'conv_b_pw2': out['conv_b_pw2'], 'ffn_w_gate': out['ffn_w_gate'], 'ffn_w_up': out['ffn_w_up'], 'ffn_w_down': out['ffn_w_down'], 'loss_target': out['loss_target'], 'm_meta': out['m_meta'], 'm_norm_g': out['m_norm_g'], 'm_mix_w_in': out['m_mix_w_in'], 'm_mix_qk_conv_w': out['m_mix_qk_conv_w'], 'm_mix_qk_conv_b': out['m_mix_qk_conv_b'], 'm_mix_gate_b': out['m_mix_gate_b'], 'm_mix_hnorm_g': out['m_mix_hnorm_g'], 'm_mix_w_out': out['m_mix_w_out'], 'm_conv_w_pw1': out['m_conv_w_pw1'], 'm_conv_b_pw1': out['m_conv_b_pw1'], 'm_conv_w_dw': out['m_conv_w_dw'], 'm_conv_b_dw': out['m_conv_b_dw'], 'm_conv_ln_g': out['m_conv_ln_g'], 'm_conv_ln_b': out['m_conv_ln_b'], 'm_conv_w_pw2': out['m_conv_w_pw2'], 'm_conv_b_pw2': out['m_conv_b_pw2'], 'm_ffn_w_gate': out['m_ffn_w_gate'], 'm_ffn_w_up': out['m_ffn_w_up'], 'm_ffn_w_down': out['m_ffn_w_down'], 'v_meta': out['v_meta'], 'v_norm_g': out['v_norm_g'], 'v_mix_w_in': out['v_mix_w_in'], 'v_mix_qk_conv_w': out['v_mix_qk_conv_w'], 'v_mix_qk_conv_b': out['v_mix_qk_conv_b'], 'v_mix_gate_b': out['v_mix_gate_b'], 'v_mix_hnorm_g': out['v_mix_hnorm_g'], 'v_mix_w_out': out['v_mix_w_out'], 'v_conv_w_pw1': out['v_conv_w_pw1'], 'v_conv_b_pw1': out['v_conv_b_pw1'], 'v_conv_w_dw': out['v_conv_w_dw'], 'v_conv_b_dw': out['v_conv_b_dw'], 'v_conv_ln_g': out['v_conv_ln_g'], 'v_conv_ln_b': out['v_conv_ln_b'], 'v_conv_w_pw2': out['v_conv_w_pw2'], 'v_conv_b_pw2': out['v_conv_b_pw2'], 'v_ffn_w_gate': out['v_ffn_w_gate'], 'v_ffn_w_up': out['v_ffn_w_up'], 'v_ffn_w_down': out['v_ffn_w_down']}


def _loss(weights, diff, rest, loss_target):
    with _jax.named_scope("forward"):
        args = {**rest, TWIN_DIFF_INPUT: diff, **{k: w.astype(_WEIGHT_DTYPES[k]) for k, w in weights.items()}}
        y = _forward(args)
    with _jax.named_scope("loss_head"):
        err = _jnp.square(y.astype(_jnp.float32) - loss_target)
        return 0.5 * _jnp.sum(_jnp.mean(err, axis=-1)) if err.ndim else 0.5 * err


def _adamw(w, g, m, v):
    m = ADAM_B1 * m + (1.0 - ADAM_B1) * g
    v = ADAM_B2 * v + (1.0 - ADAM_B2) * _jnp.square(g)
    m_hat = m / (1.0 - ADAM_B1 ** ADAM_STEP)
    v_hat = v / (1.0 - ADAM_B2 ** ADAM_STEP)
    delta = -ADAM_LR * (m_hat / (_jnp.sqrt(v_hat) + ADAM_EPS) + ADAM_WD * w)
    return delta, m, v


def reference(x, meta, norm_g, mix_w_in, mix_qk_conv_w, mix_qk_conv_b, mix_gate_b, mix_hnorm_g, mix_w_out, conv_w_pw1, conv_b_pw1, conv_w_dw, conv_b_dw, conv_ln_g, conv_ln_b, conv_w_pw2, conv_b_pw2, ffn_w_gate, ffn_w_up, ffn_w_down, loss_target, m_meta, m_norm_g, m_mix_w_in, m_mix_qk_conv_w, m_mix_qk_conv_b, m_mix_gate_b, m_mix_hnorm_g, m_mix_w_out, m_conv_w_pw1, m_conv_b_pw1, m_conv_w_dw, m_conv_b_dw, m_conv_ln_g, m_conv_ln_b, m_conv_w_pw2, m_conv_b_pw2, m_ffn_w_gate, m_ffn_w_up, m_ffn_w_down, v_meta, v_norm_g, v_mix_w_in, v_mix_qk_conv_w, v_mix_qk_conv_b, v_mix_gate_b, v_mix_hnorm_g, v_mix_w_out, v_conv_w_pw1, v_conv_b_pw1, v_conv_w_dw, v_conv_b_dw, v_conv_ln_g, v_conv_ln_b, v_conv_w_pw2, v_conv_b_pw2, v_ffn_w_gate, v_ffn_w_up, v_ffn_w_down):
    given = dict(x=x, meta=meta, norm_g=norm_g, mix_w_in=mix_w_in, mix_qk_conv_w=mix_qk_conv_w, mix_qk_conv_b=mix_qk_conv_b, mix_gate_b=mix_gate_b, mix_hnorm_g=mix_hnorm_g, mix_w_out=mix_w_out, conv_w_pw1=conv_w_pw1, conv_b_pw1=conv_b_pw1, conv_w_dw=conv_w_dw, conv_b_dw=conv_b_dw, conv_ln_g=conv_ln_g, conv_ln_b=conv_ln_b, conv_w_pw2=conv_w_pw2, conv_b_pw2=conv_b_pw2, ffn_w_gate=ffn_w_gate, ffn_w_up=ffn_w_up, ffn_w_down=ffn_w_down, loss_target=loss_target, m_meta=m_meta, m_norm_g=m_norm_g, m_mix_w_in=m_mix_w_in, m_mix_qk_conv_w=m_mix_qk_conv_w, m_mix_qk_conv_b=m_mix_qk_conv_b, m_mix_gate_b=m_mix_gate_b, m_mix_hnorm_g=m_mix_hnorm_g, m_mix_w_out=m_mix_w_out, m_conv_w_pw1=m_conv_w_pw1, m_conv_b_pw1=m_conv_b_pw1, m_conv_w_dw=m_conv_w_dw, m_conv_b_dw=m_conv_b_dw, m_conv_ln_g=m_conv_ln_g, m_conv_ln_b=m_conv_ln_b, m_conv_w_pw2=m_conv_w_pw2, m_conv_b_pw2=m_conv_b_pw2, m_ffn_w_gate=m_ffn_w_gate, m_ffn_w_up=m_ffn_w_up, m_ffn_w_down=m_ffn_w_down, v_meta=v_meta, v_norm_g=v_norm_g, v_mix_w_in=v_mix_w_in, v_mix_qk_conv_w=v_mix_qk_conv_w, v_mix_qk_conv_b=v_mix_qk_conv_b, v_mix_gate_b=v_mix_gate_b, v_mix_hnorm_g=v_mix_hnorm_g, v_mix_w_out=v_mix_w_out, v_conv_w_pw1=v_conv_w_pw1, v_conv_b_pw1=v_conv_b_pw1, v_conv_w_dw=v_conv_w_dw, v_conv_b_dw=v_conv_b_dw, v_conv_ln_g=v_conv_ln_g, v_conv_ln_b=v_conv_ln_b, v_conv_w_pw2=v_conv_w_pw2, v_conv_b_pw2=v_conv_b_pw2, v_ffn_w_gate=v_ffn_w_gate, v_ffn_w_up=v_ffn_w_up, v_ffn_w_down=v_ffn_w_down)
    weights = {n: given[n] for n in TWIN_WEIGHTS}
    shared = {n: given[n] for n in SHARED_INPUTS}
    per_example = {n: given[n] for n in ['x']}
    grad_fn = _jax.value_and_grad(_loss, argnums=(0, 1))

    def one_microbatch(ex, loss_target):
        ex = dict(ex)
        diff = ex.pop(TWIN_DIFF_INPUT)
        return grad_fn(weights, diff, {**shared, **ex}, loss_target)

    if N_MICROBATCH == 1:
        loss, (grad_w, grad_x) = one_microbatch(per_example, given["loss_target"])
    else:
        def body(carry, xs):
            loss_sum, grad_sum = carry
            l_k, (gw_k, gx_k) = one_microbatch(xs[0], xs[1])
            with _jax.named_scope("update"):
                return (loss_sum + l_k, _jax.tree.map(_jnp.add, grad_sum, gw_k)), gx_k

        init = (_jnp.zeros((), _jnp.float32), _jax.tree.map(_jnp.zeros_like, weights))
        (loss, grad_w), grad_x = _jax.lax.scan(body, init, (per_example, given["loss_target"]))
    with _jax.named_scope("update"):
        delta_w, new_m, new_v = {}, {}, {}
        for n in TWIN_WEIGHTS:
            delta_w[n], new_m[n], new_v[n] = _adamw(weights[n], grad_w[n], given["m_" + n], given["v_" + n])
    return (loss, grad_x, *[grad_w[n] for n in TWIN_WEIGHTS], *[delta_w[n] for n in TWIN_WEIGHTS],
            *[new_m[n] for n in TWIN_WEIGHTS], *[new_v[n] for n in TWIN_WEIGHTS])
```

```python
import functools

import jax
import jax.numpy as jnp
from jax import lax
from jax.experimental import pallas as pl
from jax.experimental.pallas import tpu as pltpu

F32 = jnp.float32
MXU_DTYPE = jnp.bfloat16

D_MODEL = 1024
N_META = 16
DEPTH = 4
MLSTM_HEADS = 4
MLSTM_DQK = 128
MLSTM_DV = 256
MLSTM_CHUNK = 64
QK_CONV_WIDTH = 4
GATE_SOFTCAP = 15.0
SB_HEADS = 4
SB_DH = 128
SB_BLOCK = 128
PAD_FRONT = SB_BLOCK - N_META
CONV_WIDTH = 31
FFN_HIDDEN = 2816
MQK = MLSTM_HEADS * MLSTM_DQK
MV = MLSTM_HEADS * MLSTM_DV
SBW = SB_HEADS * SB_DH
IN_WIDTH = 2 * MQK + 2 * MV + 2 * MLSTM_HEADS + 3 * SBW
MIX_WIDTH = MV + SBW
NEG = -1e30
EPS = 1e-6
PROJ_WIDTH = 5120
GATE_COL = 3 * MV + 3 * SBW
LANE = 128
SUBLANE = 8
CONV_HALO = 32
VMEM_LIMIT = 56 * 1024 * 1024

ADAM_LR = 0.001
ADAM_B1 = 0.9
ADAM_B2 = 0.999
ADAM_EPS = 1e-08
ADAM_WD = 0.01
ADAM_STEP = 10


def _divisor(n, cands):
    for c in cands:
        if n % c == 0:
            return c
    raise ValueError(f"no tile for {n} in {cands}")


ROW_TILE_BYTES = 20 * 1024 * 1024


def _row_tile(tp, width=D_MODEL):
    for c in (640, 512, 384, 320, 256, 128, 64):
        if tp % c == 0 and c * width * 8 <= ROW_TILE_BYTES:
            return c
    raise ValueError(f"no row tile for {tp} x {width}")


def _params(sem):
    return pltpu.CompilerParams(dimension_semantics=sem, vmem_limit_bytes=VMEM_LIMIT)


def _dot(a, b, dims):
    return lax.dot_general(a.astype(MXU_DTYPE), b.astype(MXU_DTYPE), (dims, ((), ())),
                           preferred_element_type=F32)


def _nn(a, b):
    return _dot(a, b, ((1,), (0,)))


def _nt(a, b):
    return _dot(a, b, ((1,), (1,)))


def _tn(a, b):
    return _dot(a, b, ((0,), (0,)))


def _sigmoid(x):
    return 1.0 / (1.0 + jnp.exp(-x))


def _softplus(x):
    return jnp.maximum(x, 0.0) + jnp.log(1.0 + jnp.exp(-jnp.abs(x)))


def _matmul(a, b, *, ta=False, tb=False, name):
    m, k = (a.shape[1], a.shape[0]) if ta else a.shape
    n = b.shape[0] if tb else b.shape[1]
    assert (b.shape[1] if tb else b.shape[0]) == k, (a.shape, b.shape, ta, tb)
    tm = _divisor(m, (640, 512, 384, 256, 128))
    tk = _divisor(k, (1408, 1024, 768, 640, 512, 384, 256, 128))
    tn = _divisor(n, (1408, 1024, 768, 512, 256, 128))
    nk = k // tk

    def body(a_ref, b_ref, o_ref, acc_ref):
        kk = pl.program_id(2)

        @pl.when(kk == 0)
        def _():
            acc_ref[...] = jnp.zeros_like(acc_ref)

        acc_ref[...] += _dot(a_ref[...], b_ref[...], ((0 if ta else 1,), (1 if tb else 0,)))

        @pl.when(kk == nk - 1)
        def _():
            o_ref[...] = acc_ref[...]

    a_spec = (pl.BlockSpec((tk, tm), lambda i, j, kk: (kk, i)) if ta
              else pl.BlockSpec((tm, tk), lambda i, j, kk: (i, kk)))
    b_spec = (pl.BlockSpec((tn, tk), lambda i, j, kk: (j, kk)) if tb
              else pl.BlockSpec((tk, tn), lambda i, j, kk: (kk, j)))
    return pl.pallas_call(
        body, name=name, grid=(m // tm, n // tn, nk),
        in_specs=[a_spec, b_spec],
        out_specs=pl.BlockSpec((tm, tn), lambda i, j, kk: (i, j)),
        out_shape=jax.ShapeDtypeStruct((m, n), F32),
        scratch_shapes=[pltpu.VMEM((tm, tn), F32)],
        compiler_params=_params(("parallel", "parallel", "arbitrary")),
    )(a, b)


def _rowwise(fn, rows, fulls, out_rows, out_accs, *, name):
    tp = rows[0][0].shape[0]
    tm = _row_tile(tp, sum(w for _, _, w in rows) + sum(out_rows))
    nr, nf, no, na = len(rows), len(fulls), len(out_rows), len(out_accs)

    def body(*refs):
        i = pl.program_id(0)
        outs = fn(i * tm, *[r[...] for r in refs[:nr + nf]])
        for k in range(no):
            refs[nr + nf + k][...] = outs[k]
        for k in range(na):
            ref = refs[nr + nf + no + k]

            @pl.when(i == 0)
            def _(ref=ref):
                ref[...] = jnp.zeros_like(ref)

            ref[...] += outs[no + k]

    in_specs = [pl.BlockSpec((tm, w), functools.partial(lambda i, cb: (i, cb), cb=cb)) for _, cb, w in rows]
    in_specs += [pl.BlockSpec(f.shape, lambda i: (0, 0)) for f in fulls]
    out_specs = [pl.BlockSpec((tm, w), lambda i: (i, 0)) for w in out_rows]
    out_specs += [pl.BlockSpec(s, lambda i: (0, 0)) for s in out_accs]
    out_shape = [jax.ShapeDtypeStruct((tp, w), F32) for w in out_rows]
    out_shape += [jax.ShapeDtypeStruct(s, F32) for s in out_accs]
    return pl.pallas_call(
        body, name=name, grid=(tp // tm,), in_specs=in_specs, out_specs=out_specs, out_shape=out_shape,
        compiler_params=_params(("arbitrary",)),
    )(*[r[0] for r in rows], *fulls)


def _whole(a):
    return (a, 0, a.shape[1])


def _live(row0, tm):
    return (row0 + lax.broadcasted_iota(jnp.int32, (tm, 1), 0)) >= PAD_FRONT


def _rms_core(x, g):
    r = lax.rsqrt(jnp.mean(x * x, axis=-1, keepdims=True) + EPS)
    return x * r, r


def _rms_fwd(x, g, *, name, res=None, bias=None):
    def fn(row0, *blk):
        it = iter(blk)
        xv = next(it)
        rv = next(it) if res is not None else None
        gv = next(it)
        if bias is not None:
            xv = xv + next(it)
        xh, _ = _rms_core(xv, gv)
        y = jnp.where(_live(row0, xv.shape[0]), xh * gv, 0.0)
        return [y + rv if rv is not None else y]

    rows = [_whole(x)] + ([_whole(res)] if res is not None else [])
    fulls = [g] + ([bias] if bias is not None else [])
    return _rowwise(fn, rows, fulls, [x.shape[1]], [], name=name)[0]


def _rms_bwd(x, g, dy, *, name, add=None, bias=None):
    def fn(row0, *blk):
        it = iter(blk)
        xv, dyv = next(it), next(it)
        av = next(it) if add is not None else None
        gv = next(it)
        if bias is not None:
            xv = xv + next(it)
        dyv = jnp.where(_live(row0, xv.shape[0]), dyv, 0.0)
        xh, r = _rms_core(xv, gv)
        dyg = dyv * gv
        dx = r * (dyg - xh * jnp.mean(dyg * xh, axis=-1, keepdims=True))
        outs = [dx + av if av is not None else dx, jnp.sum(dyv * xh, axis=0, keepdims=True)]
        if bias is not None:
            outs.append(jnp.sum(dx, axis=0, keepdims=True))
        return outs

    rows = [_whole(x), _whole(dy)] + ([_whole(add)] if add is not None else [])
    fulls = [g] + ([bias] if bias is not None else [])
    c = x.shape[1]
    return _rowwise(fn, rows, fulls, [c], [(1, c)] * (2 if bias is not None else 1), name=name)


def _swiglu_fwd(ab, *, name):
    h = ab.shape[1] // 2

    def fn(row0, a, b):
        return [a * _sigmoid(a) * b]

    return _rowwise(fn, [(ab, 0, h), (ab, 1, h)], [], [h], [], name=name)[0]


def _swiglu_bwd(ab, ds, *, name):
    h = ab.shape[1] // 2

    def fn(row0, a, b, d):
        sg = _sigmoid(a)
        return [jnp.concatenate([d * b * sg * (1.0 + a * (1.0 - sg)), d * a * sg], axis=1)]

    return _rowwise(fn, [(ab, 0, h), (ab, 1, h), _whole(ds)], [], [2 * h], [], name=name)[0]


def _glu_fwd(z, b, *, name):
    h = z.shape[1] // 2

    def fn(row0, a, gt, bv):
        y = (a + bv[:, :h]) * _sigmoid(gt + bv[:, h:])
        return [jnp.where(_live(row0, a.shape[0]), y, 0.0)]

    return _rowwise(fn, [(z, 0, h), (z, 1, h)], [b], [h], [], name=name)[0]


def _glu_bwd(z, b, dy, *, name):
    h = z.shape[1] // 2

    def fn(row0, a, gt, d, bv):
        d = jnp.where(_live(row0, a.shape[0]), d, 0.0)
        sg = _sigmoid(gt + bv[:, h:])
        dz = jnp.concatenate([d * sg, d * (a + bv[:, :h]) * sg * (1.0 - sg)], axis=1)
        return [dz, jnp.sum(dz, axis=0, keepdims=True)]

    return _rowwise(fn, [(z, 0, h), (z, 1, h), _whole(dy)], [b], [2 * h], [(1, 2 * h)], name=name)


def _ln_core(x):
    mu = jnp.mean(x, axis=-1, keepdims=True)
    xc = x - mu
    r = lax.rsqrt(jnp.mean(xc * xc, axis=-1, keepdims=True) + EPS)
    return xc * r, r


def _lnsilu_fwd(x, g, b, *, name):
    def fn(row0, xv, gv, bv):
        xh, _ = _ln_core(xv)
        v = xh * gv + bv
        return [v * _sigmoid(v)]

    return _rowwise(fn, [_whole(x)], [g, b], [x.shape[1]], [], name=name)[0]


def _lnsilu_bwd(x, g, b, dy, *, name):
    def fn(row0, xv, d, gv, bv):
        xh, r = _ln_core(xv)
        v = xh * gv + bv
        sg = _sigmoid(v)
        dv = d * sg * (1.0 + v * (1.0 - sg))
        dxh = dv * gv
        dx = r * (dxh - jnp.mean(dxh, axis=-1, keepdims=True) - xh * jnp.mean(dxh * xh, axis=-1, keepdims=True))
        return [dx, jnp.sum(dv * xh, axis=0, keepdims=True), jnp.sum(dv, axis=0, keepdims=True)]

    c = x.shape[1]
    return _rowwise(fn, [_whole(x), _whole(dy)], [g, b], [c], [(1, c), (1, c)], name=name)


def _silu_fwd(x, *, name):
    return _rowwise(lambda row0, v: [v * _sigmoid(v)], [_whole(x)], [], [x.shape[1]], [], name=name)[0]


def _silu_bwd(x, dy, *, name):
    def fn(row0, v, d):
        sg = _sigmoid(v)
        return [d * sg * (1.0 + v * (1.0 - sg))]

    return _rowwise(fn, [_whole(x), _whole(dy)], [], [x.shape[1]], [], name=name)[0]


def _gate_parts(row0, pg, gb):
    lane = lax.broadcasted_iota(jnp.int32, pg.shape, 1)
    th = jnp.tanh((pg + gb) / GATE_SOFTCAP)
    s = GATE_SOFTCAP * th
    return lane, th, s, _live(row0, pg.shape[0])


def _gates_fwd(proj, gate_b, *, name):
    def fn(row0, pg, gb):
        lane, th, s, live = _gate_parts(row0, pg, gb)
        li = jnp.where(live, s, NEG)
        lf = jnp.where(live, -_softplus(-s), 0.0)
        return [jnp.where(lane < MLSTM_HEADS, li, jnp.where(lane < 2 * MLSTM_HEADS, lf, 0.0))]

    return _rowwise(fn, [(proj, GATE_COL // LANE, LANE)], [gate_b], [LANE], [], name=name)[0]


def _gates_bwd(proj, gate_b, dgl, *, name):
    def fn(row0, pg, d, gb):
        lane, th, s, live = _gate_parts(row0, pg, gb)
        ds = jnp.where(lane < MLSTM_HEADS, d, d * _sigmoid(-s))
        ds = jnp.where(live & (lane < 2 * MLSTM_HEADS), ds, 0.0)
        dp = ds * (1.0 - th * th)
        return [dp, jnp.sum(dp, axis=0, keepdims=True)]

    return _rowwise(fn, [(proj, GATE_COL // LANE, LANE), _whole(dgl)], [gate_b], [LANE], [(1, LANE)], name=name)


def _head_rms(h):
    parts = [h[:, i * MLSTM_DV:(i + 1) * MLSTM_DV] for i in range(MLSTM_HEADS)]
    rs = [lax.rsqrt(jnp.mean(p * p, axis=-1, keepdims=True) + EPS) for p in parts]
    return parts, rs


def _hnorm_fwd(hm, proj, g, *, name):
    def fn(row0, h, o, gv):
        parts, rs = _head_rms(h)
        xh = jnp.concatenate([p * r for p, r in zip(parts, rs)], axis=1)
        return [xh * gv * _sigmoid(o)]

    return _rowwise(fn, [_whole(hm), (proj, 2, MV)], [g], [MV], [], name=name)[0]


def _hnorm_bwd(hm, proj, g, dmixed, *, name):
    def fn(row0, h, o, d, gv):
        parts, rs = _head_rms(h)
        so = _sigmoid(o)
        dn = d * so
        dxs, xhs = [], []
        for i, (p, r) in enumerate(zip(parts, rs)):
            sl = slice(i * MLSTM_DV, (i + 1) * MLSTM_DV)
            xh = p * r
            dyg = dn[:, sl] * gv[:, sl]
            dxs.append(r * (dyg - xh * jnp.mean(dyg * xh, axis=-1, keepdims=True)))
            xhs.append(xh)
        xh = jnp.concatenate(xhs, axis=1)
        return [jnp.concatenate(dxs, axis=1), d * xh * gv * so * (1.0 - so), jnp.sum(dn * xh, axis=0, keepdims=True)]

    return _rowwise(fn, [_whole(hm), (proj, 2, MV), (dmixed, 0, MV)], [g], [MV, MV], [(1, MV)], name=name)


def _loss_fwd_bwd(h, target, *, name):
    first = PAD_FRONT + N_META

    def fn(row0, hv, tv):
        rows = row0 + lax.broadcasted_iota(jnp.int32, (hv.shape[0], 1), 0)
        e = jnp.where(rows >= first, hv - tv, 0.0)
        return [e * (1.0 / D_MODEL), jnp.sum(e * e, axis=0, keepdims=True)]

    return _rowwise(fn, [_whole(h), _whole(target)], [], [D_MODEL], [(1, D_MODEL)], name=name)


def _conv_tiles(tp, c):
    return _row_tile(tp), _divisor(c, (256, 128))


def _conv_fwd(x, w, b, *, name):
    tp, (k, c) = x.shape[0], w.shape
    tm, tc = _conv_tiles(tp, c)
    base = CONV_HALO - (k - 1)

    def body(x_ref, xp_ref, w_ref, b_ref, o_ref, win):
        i = pl.program_id(1)
        win[0:CONV_HALO, :] = jnp.where(i > 0, xp_ref[tm - CONV_HALO:tm, :], 0.0)
        win[CONV_HALO:CONV_HALO + tm, :] = x_ref[...]
        acc = jnp.broadcast_to(b_ref[...], (tm, tc))
        for j in range(k):
            acc = acc + w_ref[j:j + 1, :] * win[base + j:base + j + tm, :]
        o_ref[...] = acc

    return pl.pallas_call(
        body, name=name, grid=(c // tc, tp // tm),
        in_specs=[pl.BlockSpec((tm, tc), lambda cc, i: (i, cc)),
                  pl.BlockSpec((tm, tc), lambda cc, i: (jnp.maximum(i - 1, 0), cc)),
                  pl.BlockSpec((k, tc), lambda cc, i: (0, cc)),
                  pl.BlockSpec((1, tc), lambda cc, i: (0, cc))],
        out_specs=pl.BlockSpec((tm, tc), lambda cc, i: (i, cc)),
        out_shape=jax.ShapeDtypeStruct((tp, c), F32),
        scratch_shapes=[pltpu.VMEM((CONV_HALO + tm, tc), F32)],
        compiler_params=_params(("parallel", "arbitrary")),
    )(x, x, w, b)


def _conv_bwd(x, w, dy, *, name):
    tp, (k, c) = x.shape[0], w.shape
    tm, tc = _conv_tiles(tp, c)
    nt = tp // tm
    base = CONV_HALO - (k - 1)

    def body(x_ref, xp_ref, d_ref, dn_ref, w_ref, dx_ref, dw_ref, db_ref, winx, wind):
        i = pl.program_id(1)
        winx[0:CONV_HALO, :] = jnp.where(i > 0, xp_ref[tm - CONV_HALO:tm, :], 0.0)
        winx[CONV_HALO:CONV_HALO + tm, :] = x_ref[...]
        d = d_ref[...]
        wind[0:tm, :] = d
        wind[tm:tm + CONV_HALO, :] = jnp.where(i < nt - 1, dn_ref[0:CONV_HALO, :], 0.0)

        @pl.when(i == 0)
        def _():
            dw_ref[...] = jnp.zeros_like(dw_ref)
            db_ref[...] = jnp.zeros_like(db_ref)

        acc = jnp.zeros((tm, tc), F32)
        for j in range(k):
            acc = acc + w_ref[j:j + 1, :] * wind[k - 1 - j:k - 1 - j + tm, :]
            dw_ref[j:j + 1, :] += jnp.sum(d * winx[base + j:base + j + tm, :], axis=0, keepdims=True)
        dx_ref[...] = acc
        db_ref[...] += jnp.sum(d, axis=0, keepdims=True)

    return pl.pallas_call(
        body, name=name, grid=(c // tc, nt),
        in_specs=[pl.BlockSpec((tm, tc), lambda cc, i: (i, cc)),
                  pl.BlockSpec((tm, tc), lambda cc, i: (jnp.maximum(i - 1, 0), cc)),
                  pl.BlockSpec((tm, tc), lambda cc, i: (i, cc)),
                  pl.BlockSpec((tm, tc), lambda cc, i: (jnp.minimum(i + 1, nt - 1), cc)),
                  pl.BlockSpec((k, tc), lambda cc, i: (0, cc))],
        out_specs=[pl.BlockSpec((tm, tc), lambda cc, i: (i, cc)),
                   pl.BlockSpec((k, tc), lambda cc, i: (0, cc)),
                   pl.BlockSpec((1, tc), lambda cc, i: (0, cc))],
        out_shape=[jax.ShapeDtypeStruct((tp, c), F32), jax.ShapeDtypeStruct((k, c), F32),
                   jax.ShapeDtypeStruct((1, c), F32)],
        scratch_shapes=[pltpu.VMEM((CONV_HALO + tm, tc), F32), pltpu.VMEM((CONV_HALO + tm, tc), F32)],
        compiler_params=_params(("parallel", "arbitrary")),
    )(x, x, dy, dy, w)


def _chunk_masks():
    L = MLSTM_CHUNK
    r = lax.broadcasted_iota(jnp.int32, (L, L), 0)
    c = lax.broadcasted_iota(jnp.int32, (L, L), 1)
    return r == c, c <= r, r <= c


def _to_row(col, eye):
    return jnp.sum(jnp.where(eye, col, 0.0), axis=0, keepdims=True)


def _to_col(row, eye):
    return jnp.sum(jnp.where(eye, row, 0.0), axis=1, keepdims=True)


def _mlstm_chunk(q, k, v, li_c, lf_c, c_st, n_st, m_st, masks):
    eye, low, up = masks
    li_r, lf_r = _to_row(li_c, eye), _to_row(lf_c, eye)
    b_c = jnp.sum(jnp.where(low, lf_r, 0.0), axis=1, keepdims=True)
    b_r = jnp.sum(jnp.where(up, lf_c, 0.0), axis=0, keepdims=True)
    g = jnp.sum(lf_c, axis=0, keepdims=True)
    dm = jnp.where(low, b_c - b_r + li_r, NEG)
    inter = b_c + m_st
    mt = jnp.maximum(inter, jnp.max(dm, axis=1, keepdims=True))
    wi = jnp.exp(dm - mt)
    wint = jnp.exp(inter - mt)
    s = _nt(q, k) * wi
    qc = _nn(q, c_st)
    qn = jnp.sum(q * n_st, axis=1, keepdims=True)
    num = _nn(s, v) + wint * qc
    den = jnp.sum(s, axis=1, keepdims=True) + wint * qn
    floor = jnp.exp(-mt)
    a_c = g - b_c + li_c
    a_r = g - b_r + li_r
    mnew = jnp.maximum(g + m_st, jnp.max(a_r, axis=1, keepdims=True))
    wa_c = jnp.exp(a_c - mnew)
    wc = jnp.exp(g + m_st - mnew)
    return dict(wi=wi, wint=wint, s=s, qc=qc, qn=qn, num=num, den=den, floor=floor, mnew=mnew, wa_c=wa_c, wc=wc)


def _mlstm_fwd(qk, proj, gl, *, name):
    tp = qk.shape[0]
    L, H, dk, dv = MLSTM_CHUNK, MLSTM_HEADS, MLSTM_DQK, MLSTM_DV
    nc = tp // L

    def body(qk_ref, v_ref, gl_ref, h_ref, call_ref, nall_ref, mall_ref, c_s, n_s, m_s):
        @pl.when(pl.program_id(0) == 0)
        def _():
            c_s[...] = jnp.zeros_like(c_s)
            n_s[...] = jnp.zeros_like(n_s)
            m_s[...] = jnp.zeros_like(m_s)

        masks = _chunk_masks()
        gates = gl_ref[...]
        for h in range(H):
            c_st, n_st, m_row = c_s[h], n_s[h], m_s[h]
            call_ref[0, h] = c_st
            nall_ref[0, h] = n_st
            mall_ref[0, h] = m_row
            q = qk_ref[:, h * dk:(h + 1) * dk] * (dk ** -0.5)
            k = qk_ref[:, MQK + h * dk:MQK + (h + 1) * dk]
            v = v_ref[:, h * dv:(h + 1) * dv]
            f = _mlstm_chunk(q, k, v, gates[:, h:h + 1], gates[:, H + h:H + h + 1], c_st, n_st, m_row[:, 0:1], masks)
            h_ref[:, h * dv:(h + 1) * dv] = f["num"] / jnp.maximum(jnp.abs(f["den"]), f["floor"])
            kw = k * f["wa_c"]
            c_s[h] = f["wc"] * c_st + _tn(kw, v)
            n_s[h] = f["wc"] * n_st + jnp.sum(kw, axis=0, keepdims=True)
            m_s[h] = jnp.broadcast_to(f["mnew"], (1, LANE))

    return pl.pallas_call(
        body, name=name, grid=(nc,),
        in_specs=[pl.BlockSpec((L, 2 * MQK), lambda i: (i, 0)),
                  pl.BlockSpec((L, MV), lambda i: (i, 1)),
                  pl.BlockSpec((L, LANE), lambda i: (i, 0))],
        out_specs=[pl.BlockSpec((L, MV), lambda i: (i, 0)),
                   pl.BlockSpec((1, H, dk, dv), lambda i: (i, 0, 0, 0)),
                   pl.BlockSpec((1, H, 1, dk), lambda i: (i, 0, 0, 0)),
                   pl.BlockSpec((1, H, 1, LANE), lambda i: (i, 0, 0, 0))],
        out_shape=[jax.ShapeDtypeStruct((tp, MV), F32),
                   jax.ShapeDtypeStruct((nc, H, dk, dv), F32),
                   jax.ShapeDtypeStruct((nc, H, 1, dk), F32),
                   jax.ShapeDtypeStruct((nc, H, 1, LANE), F32)],
        scratch_shapes=[pltpu.VMEM((H, dk, dv), F32), pltpu.VMEM((H, 1, dk), F32), pltpu.VMEM((H, 1, LANE), F32)],
        compiler_params=_params(("arbitrary",)),
    )(qk, proj, gl)


def _mlstm_bwd(qk, proj, gl, dmix, call, nall, mall, *, name):
    tp = qk.shape[0]
    L, H, dk, dv = MLSTM_CHUNK, MLSTM_HEADS, MLSTM_DQK, MLSTM_DV
    nc = tp // L

    def body(qk_ref, v_ref, gl_ref, dh_ref, call_ref, nall_ref, mall_ref, dqk_ref, dv_ref, dgl_ref, dc_s, dn_s):
        @pl.when(pl.program_id(0) == 0)
        def _():
            dc_s[...] = jnp.zeros_like(dc_s)
            dn_s[...] = jnp.zeros_like(dn_s)

        masks = _chunk_masks()
        eye, low, up = masks
        gates = gl_ref[...]
        lane = lax.broadcasted_iota(jnp.int32, (L, LANE), 1)
        dgl = jnp.zeros((L, LANE), F32)
        for h in range(H):
            c_st, n_st, m_st = call_ref[0, h], nall_ref[0, h], mall_ref[0, h][:, 0:1]
            q = qk_ref[:, h * dk:(h + 1) * dk] * (dk ** -0.5)
            k = qk_ref[:, MQK + h * dk:MQK + (h + 1) * dk]
            v = v_ref[:, h * dv:(h + 1) * dv]
            dh = dh_ref[:, h * dv:(h + 1) * dv]
            dcn, dnn = dc_s[h], dn_s[h]
            f = _mlstm_chunk(q, k, v, gates[:, h:h + 1], gates[:, H + h:H + h + 1], c_st, n_st, m_st, masks)
            wint, s, wa_c, wc = f["wint"], f["s"], f["wa_c"], f["wc"]

            scale = jnp.maximum(jnp.abs(f["den"]), f["floor"])
            r = 1.0 / scale
            dnum = dh * r
            dscale = -jnp.sum(dh * f["num"], axis=1, keepdims=True) * r * r
            dden = jnp.where(jnp.abs(f["den"]) > f["floor"], dscale * jnp.sign(f["den"]), 0.0)
            ds = _nt(dnum, v) + dden
            wd = wint * dnum
            dwint = jnp.sum(dnum * f["qc"], axis=1, keepdims=True) + dden * f["qn"]
            dd = ds * s
            da_mat = ds * f["wi"]
            dq = _nt(wd, c_st) + (dden * wint) * n_st + _nn(da_mat, k)
            dk_ = _tn(da_mat, q)
            dv_ = _tn(s, dnum)
            dc_acc = _tn(q, wd)
            dn_acc = jnp.sum(q * (dden * wint), axis=0, keepdims=True)

            kd = _nn(k, dcn)
            dk_ = dk_ + wa_c * (_nt(v, dcn) + dnn)
            dv_ = dv_ + wa_c * kd
            dwa = jnp.sum(kd * v, axis=1, keepdims=True) + jnp.sum(k * dnn, axis=1, keepdims=True)
            dwc = jnp.sum(jnp.sum(dcn * c_st, axis=1, keepdims=True), axis=0, keepdims=True) \
                + jnp.sum(dnn * n_st, axis=1, keepdims=True)
            da_c = dwa * wa_c
            dg = jnp.sum(da_c, axis=0, keepdims=True) + dwc * wc

            dd_cols = jnp.sum(dd, axis=0, keepdims=True)
            db_c = dwint * wint + jnp.sum(dd, axis=1, keepdims=True) - da_c
            db_r = _to_row(db_c, eye) - dd_cols
            dlf = jnp.sum(jnp.where(up, db_r, 0.0), axis=1, keepdims=True) + dg
            dli = da_c + _to_col(dd_cols, eye)

            dc_s[h] = wc * dcn + dc_acc
            dn_s[h] = wc * dnn + dn_acc
            dqk_ref[:, h * dk:(h + 1) * dk] = dq * (dk ** -0.5)
            dqk_ref[:, MQK + h * dk:MQK + (h + 1) * dk] = dk_
            dv_ref[:, h * dv:(h + 1) * dv] = dv_
            dgl = dgl + jnp.where(lane == h, dli, 0.0) + jnp.where(lane == H + h, dlf, 0.0)
        dgl_ref[...] = dgl

    rev = lambda i: nc - 1 - i
    return pl.pallas_call(
        body, name=name, grid=(nc,),
        in_specs=[pl.BlockSpec((L, 2 * MQK), lambda i: (rev(i), 0)),
                  pl.BlockSpec((L, MV), lambda i: (rev(i), 1)),
                  pl.BlockSpec((L, LANE), lambda i: (rev(i), 0)),
                  pl.BlockSpec((L, MV), lambda i: (rev(i), 0)),
                  pl.BlockSpec((1, H, dk, dv), lambda i: (rev(i), 0, 0, 0)),
                  pl.BlockSpec((1, H, 1, dk), lambda i: (rev(i), 0, 0, 0)),
                  pl.BlockSpec((1, H, 1, LANE), lambda i: (rev(i), 0, 0, 0))],
        out_specs=[pl.BlockSpec((L, 2 * MQK), lambda i: (rev(i), 0)),
                   pl.BlockSpec((L, MV), lambda i: (rev(i), 0)),
                   pl.BlockSpec((L, LANE), lambda i: (rev(i), 0))],
        out_shape=[jax.ShapeDtypeStruct((tp, 2 * MQK), F32), jax.ShapeDtypeStruct((tp, MV), F32),
                   jax.ShapeDtypeStruct((tp, LANE), F32)],
        scratch_shapes=[pltpu.VMEM((H, dk, dv), F32), pltpu.VMEM((H, 1, dk), F32)],
        compiler_params=_params(("arbitrary",)),
    )(qk, proj, gl, dmix, call, nall, mall)


SB_Q0 = 3 * MV // LANE
SB_K0 = SB_Q0 + SB_HEADS
SB_V0 = SB_K0 + SB_HEADS


def _cumsum_dot(x, tri):
    hi = x.astype(jnp.bfloat16)
    lo = (x - hi.astype(F32)).astype(jnp.bfloat16)
    dims = (((1,), (0,)), ((), ()))
    return (lax.dot_general(hi, tri, dims, preferred_element_type=F32)
            + lax.dot_general(lo, tri, dims, preferred_element_type=F32))


def _sb_block(q, kj, i, j, across):
    B = SB_BLOCK
    z = _nt(q, kj) * (SB_DH ** -0.5)
    t_idx = i * B + lax.broadcasted_iota(jnp.int32, (B, B), 0)
    s_idx = j * B + lax.broadcasted_iota(jnp.int32, (B, B), 1)
    mask = (s_idx < t_idx) & (s_idx >= PAD_FRONT)
    l = jnp.where(mask, -_softplus(z), 0.0)
    r = lax.broadcasted_iota(jnp.int32, (B, B), 0)
    c = lax.broadcasted_iota(jnp.int32, (B, B), 1)
    within = _cumsum_dot(l, (r >= c).astype(jnp.bfloat16))
    w = jnp.where(mask, jnp.exp(z + within + across), 0.0)
    return z, mask, l, w


def _sb_fwd(proj, *, name):
    tp = proj.shape[0]
    B, H = SB_BLOCK, SB_HEADS
    nq = tp // B
    assert nq <= LANE

    def body(q_ref, k_ref, v_ref, o_ref, ac_ref, run_s):
        i = pl.program_id(1)
        q = q_ref[...]
        lane = lax.broadcasted_iota(jnp.int32, (B, LANE), 1)
        o_ref[...] = jnp.zeros_like(o_ref)
        ac_ref[0, 0] = jnp.zeros((B, LANE), F32)
        run_s[...] = jnp.zeros_like(run_s)

        def step(jj, carry):
            j = i - jj
            rows = pl.ds(pl.multiple_of(j * B, B), B)
            run = run_s[...]
            z, mask, l, w = _sb_block(q, k_ref[rows, :], i, j, run)
            o_ref[...] += _nn(w, v_ref[rows, :])
            ac_ref[0, 0] = jnp.where(lane == j, run, ac_ref[0, 0])
            run_s[...] = run + jnp.sum(l, axis=1, keepdims=True)
            return carry

        lax.fori_loop(0, i + 1, step, 0)

    return pl.pallas_call(
        body, name=name, grid=(H, nq), scratch_shapes=[pltpu.VMEM((SB_BLOCK, LANE), F32)],
        in_specs=[pl.BlockSpec((B, SB_DH), lambda h, i: (i, SB_Q0 + h)),
                  pl.BlockSpec((tp, SB_DH), lambda h, i: (0, SB_K0 + h)),
                  pl.BlockSpec((tp, SB_DH), lambda h, i: (0, SB_V0 + h))],
        out_specs=[pl.BlockSpec((B, SB_DH), lambda h, i: (i, h)),
                   pl.BlockSpec((1, 1, B, LANE), lambda h, i: (h, i, 0, 0))],
        out_shape=[jax.ShapeDtypeStruct((tp, SBW), F32), jax.ShapeDtypeStruct((H, nq, B, LANE), F32)],
        compiler_params=_params(("parallel", "arbitrary")),
    )(proj, proj, proj)


def _sb_bwd(proj, across, dmix, *, name):
    tp = proj.shape[0]
    B, H = SB_BLOCK, SB_HEADS
    nq = tp // B
    do0 = MV // LANE

    def body(q_ref, k_ref, v_ref, ac_ref, do_ref, dq_ref, dk_ref, dv_ref, gpre_s):
        i = pl.program_id(1)

        @pl.when(i == 0)
        def _():
            dk_ref[...] = jnp.zeros_like(dk_ref)
            dv_ref[...] = jnp.zeros_like(dv_ref)

        dq_ref[...] = jnp.zeros_like(dq_ref)
        gpre_s[...] = jnp.zeros_like(gpre_s)
        q, do, tile = q_ref[...], do_ref[...], ac_ref[0, 0]
        lane = lax.broadcasted_iota(jnp.int32, (B, LANE), 1)
        r = lax.broadcasted_iota(jnp.int32, (B, B), 0)
        c = lax.broadcasted_iota(jnp.int32, (B, B), 1)
        prefix = (r <= c).astype(jnp.bfloat16)
        scale = SB_DH ** -0.5

        def step(j, carry):
            rows = pl.ds(pl.multiple_of(j * B, B), B)
            kj, vj = k_ref[rows, :], v_ref[rows, :]
            run = jnp.sum(jnp.where(lane == j, tile, 0.0), axis=1, keepdims=True)
            z, mask, l, w = _sb_block(q, kj, i, j, run)
            dv_ref[rows, :] += _tn(w, do)
            g = _nt(do, vj) * w
            gpre = gpre_s[...]
            gcum = gpre + _cumsum_dot(g, prefix)
            dz = (g - jnp.where(mask, _sigmoid(z) * gcum, 0.0)) * scale
            dk_ref[rows, :] += _tn(dz, q)
            dq_ref[...] += _nn(dz, kj)
            gpre_s[...] = gpre + jnp.sum(g, axis=1, keepdims=True)
            return carry

        lax.fori_loop(0, i + 1, step, 0)

    return pl.pallas_call(
        body, name=name, grid=(H, nq), scratch_shapes=[pltpu.VMEM((SB_BLOCK, LANE), F32)],
        in_specs=[pl.BlockSpec((B, SB_DH), lambda h, i: (i, SB_Q0 + h)),
                  pl.BlockSpec((tp, SB_DH), lambda h, i: (0, SB_K0 + h)),
                  pl.BlockSpec((tp, SB_DH), lambda h, i: (0, SB_V0 + h)),
                  pl.BlockSpec((1, 1, B, LANE), lambda h, i: (h, i, 0, 0)),
                  pl.BlockSpec((B, SB_DH), lambda h, i: (i, do0 + h))],
        out_specs=[pl.BlockSpec((B, SB_DH), lambda h, i: (i, h)),
                   pl.BlockSpec((tp, SB_DH), lambda h, i: (0, h)),
                   pl.BlockSpec((tp, SB_DH), lambda h, i: (0, h))],
        out_shape=[jax.ShapeDtypeStruct((tp, SBW), F32)] * 3,
        compiler_params=_params(("parallel", "arbitrary")),
    )(proj, proj, proj, across, dmix)


def _ffn_forward(h, p, tag):
    u = _rms_fwd(h, p["g2"], name=f"{tag}_ffn_norm")
    ab = _matmul(u, p["w_gu"], name=f"{tag}_ffn_gate_up")
    s = _swiglu_fwd(ab, name=f"{tag}_ffn_act")
    f = _matmul(s, p["w_down"], name=f"{tag}_ffn_down")
    out = _rms_fwd(f, p["g3"], res=h, name=f"{tag}_ffn_out")
    return out, dict(h=h, u=u, ab=ab, s=s, f=f)


def _ffn_backward(dh, p, a, tag):
    df, dg3 = _rms_bwd(a["f"], p["g3"], dh, name=f"{tag}_ffn_out_bwd")
    ds = _matmul(df, p["w_down"], tb=True, name=f"{tag}_ffn_down_dx")
    dw_down = _matmul(a["s"], df, ta=True, name=f"{tag}_ffn_down_dw")
    dab = _swiglu_bwd(a["ab"], ds, name=f"{tag}_ffn_act_bwd")
    du = _matmul(dab, p["w_gu"], tb=True, name=f"{tag}_ffn_gate_up_dx")
    dw_gu = _matmul(a["u"], dab, ta=True, name=f"{tag}_ffn_gate_up_dw")
    dh_in, dg2 = _rms_bwd(a["h"], p["g2"], du, add=dh, name=f"{tag}_ffn_norm_bwd")
    return dh_in, dict(g2=dg2, g3=dg3, w_gu=dw_gu, w_down=dw_down)


def _mixer_forward(h, p, tag):
    u = _rms_fwd(h, p["g0"], name=f"{tag}_mix_norm")
    proj = _matmul(u, p["w_in"], name=f"{tag}_mix_in")
    qc = _conv_fwd(proj, p["qk_w"], p["qk_b"], name=f"{tag}_mix_qkconv")
    qk = _silu_fwd(qc, name=f"{tag}_mix_qkact")
    gl = _gates_fwd(proj, p["gate_b"], name=f"{tag}_mix_gates")
    hm, call, nall, mall = _mlstm_fwd(qk, proj, gl, name=f"{tag}_mlstm")
    hn = _hnorm_fwd(hm, proj, p["hnorm_g"], name=f"{tag}_mix_hnorm")
    hs, across = _sb_fwd(proj, name=f"{tag}_sb")
    mixed = jnp.concatenate([hn, hs], axis=1)
    y = _matmul(mixed, p["w_out"], name=f"{tag}_mix_out")
    out = _rms_fwd(y, p["g1"], res=h, name=f"{tag}_mix_res")
    return out, dict(h=h, u=u, proj=proj, qc=qc, qk=qk, gl=gl, hm=hm, call=call, nall=nall, mall=mall,
                     across=across, mixed=mixed, y=y)


def _mixer_backward(dh, p, a, tag):
    tp = dh.shape[0]
    dy, dg1 = _rms_bwd(a["y"], p["g1"], dh, name=f"{tag}_mix_res_bwd")
    dmixed = _matmul(dy, p["w_out"], tb=True, name=f"{tag}_mix_out_dx")
    dw_out = _matmul(a["mixed"], dy, ta=True, name=f"{tag}_mix_out_dw")
    dsq, dsk, dsv = _sb_bwd(a["proj"], a["across"], dmixed, name=f"{tag}_sb_bwd")
    dhm, do, dhg = _hnorm_bwd(a["hm"], a["proj"], p["hnorm_g"], dmixed, name=f"{tag}_mix_hnorm_bwd")
    dqk, dv, dgl = _mlstm_bwd(a["qk"], a["proj"], a["gl"], dhm, a["call"], a["nall"], a["mall"],
                              name=f"{tag}_mlstm_bwd")
    dpg, dgate_b = _gates_bwd(a["proj"], p["gate_b"], dgl, name=f"{tag}_mix_gates_bwd")
    dqc = _silu_bwd(a["qc"], dqk, name=f"{tag}_mix_qkact_bwd")
    dpqk, dqk_w, dqk_b = _conv_bwd(a["proj"], p["qk_w"], dqc, name=f"{tag}_mix_qkconv_bwd")
    dproj = jnp.concatenate(
        [dpqk, dv, do, dsq, dsk, dsv, dpg, jnp.zeros((tp, PROJ_WIDTH - GATE_COL - LANE), F32)], axis=1)
    du = _matmul(dproj, p["w_in"], tb=True, name=f"{tag}_mix_in_dx")
    dw_in = _matmul(a["u"], dproj, ta=True, name=f"{tag}_mix_in_dw")
    dh_in, dg0 = _rms_bwd(a["h"], p["g0"], du, add=dh, name=f"{tag}_mix_norm_bwd")
    return dh_in, dict(g0=dg0, g1=dg1, w_in=dw_in, qk_w=dqk_w, qk_b=dqk_b, gate_b=dgate_b, hnorm_g=dhg,
                       w_out=dw_out)


def _conformer_forward(h, p, tag):
    u = _rms_fwd(h, p["g0"], name=f"{tag}_conf_norm")
    z = _matmul(u, p["w_pw1"], name=f"{tag}_conf_pw1")
    y1 = _glu_fwd(z, p["b_pw1"], name=f"{tag}_conf_glu")
    y2 = _conv_fwd(y1, p["w_dw"], p["b_dw"], name=f"{tag}_conf_dw")
    y3 = _lnsilu_fwd(y2, p["ln_g"], p["ln_b"], name=f"{tag}_conf_ln")
    y4 = _matmul(y3, p["w_pw2"], name=f"{tag}_conf_pw2")
    out = _rms_fwd(y4, p["g1"], res=h, bias=p["b_pw2"], name=f"{tag}_conf_res")
    return out, dict(h=h, u=u, z=z, y1=y1, y2=y2, y3=y3, y4=y4)


def _conformer_backward(dh, p, a, tag):
    dy4, dg1, db_pw2 = _rms_bwd(a["y4"], p["g1"], dh, bias=p["b_pw2"], name=f"{tag}_conf_res_bwd")
    dy3 = _matmul(dy4, p["w_pw2"], tb=True, name=f"{tag}_conf_pw2_dx")
    dw_pw2 = _matmul(a["y3"], dy4, ta=True, name=f"{tag}_conf_pw2_dw")
    dy2, dln_g, dln_b = _lnsilu_bwd(a["y2"], p["ln_g"], p["ln_b"], dy3, name=f"{tag}_conf_ln_bwd")
    dy1, dw_dw, db_dw = _conv_bwd(a["y1"], p["w_dw"], dy2, name=f"{tag}_conf_dw_bwd")
    dz, db_pw1 = _glu_bwd(a["z"], p["b_pw1"], dy1, name=f"{tag}_conf_glu_bwd")
    du = _matmul(dz, p["w_pw1"], tb=True, name=f"{tag}_conf_pw1_dx")
    dw_pw1 = _matmul(a["u"], dz, ta=True, name=f"{tag}_conf_pw1_dw")
    dh_in, dg0 = _rms_bwd(a["h"], p["g0"], du, add=dh, name=f"{tag}_conf_norm_bwd")
    return dh_in, dict(g0=dg0, g1=dg1, w_pw1=dw_pw1, b_pw1=db_pw1, w_dw=dw_dw, b_dw=db_dw, ln_g=dln_g,
                       ln_b=dln_b, w_pw2=dw_pw2, b_pw2=db_pw2)


def _trunk_step(h0, target, layers):
    acts = []
    h = h0
    for li, p in enumerate(layers):
        tag = f"l{li}"
        h, a_mix = (_mixer_forward if li % 2 == 0 else _conformer_forward)(h, p["mix"], tag)
        h, a_ffn = _ffn_forward(h, p["ffn"], tag)
        acts.append((a_mix, a_ffn))
    dh, loss_cols = _loss_fwd_bwd(h, target, name="loss")
    grads = [None] * len(layers)
    for li in reversed(range(len(layers))):
        tag = f"l{li}"
        p = layers[li]
        dh, g_ffn = _ffn_backward(dh, p["ffn"], acts[li][1], tag)
        dh, g_mix = (_mixer_backward if li % 2 == 0 else _conformer_backward)(dh, p["mix"], acts[li][0], tag)
        grads[li] = dict(mix=g_mix, ffn=g_ffn)
    return loss_cols, dh, grads


_SPLIT = 2 * MQK + 2 * MV


def _prepare_layers(w):
    layers = []
    row = lambda v: v[None, :].astype(F32)
    for li in range(DEPTH):
        i = li // 2
        g = w["norm_g"][li].astype(F32)
        if li % 2 == 0:
            win = w["mix_w_in"][i]
            w_in = jnp.concatenate(
                [win[:, :_SPLIT], win[:, _SPLIT + 2 * MLSTM_HEADS:], win[:, _SPLIT:_SPLIT + 2 * MLSTM_HEADS],
                 jnp.zeros((D_MODEL, PROJ_WIDTH - IN_WIDTH), win.dtype)], axis=1)
            gate_b = jnp.pad(row(w["mix_gate_b"][i]), ((0, 0), (0, LANE - 2 * MLSTM_HEADS)))
            mix = dict(g0=g[0:1], g1=g[1:2], w_in=w_in, qk_w=w["mix_qk_conv_w"][i].astype(F32),
                       qk_b=row(w["mix_qk_conv_b"][i]), gate_b=gate_b, hnorm_g=row(w["mix_hnorm_g"][i]),
                       w_out=w["mix_w_out"][i])
        else:
            mix = dict(g0=g[0:1], g1=g[1:2], w_pw1=w["conv_w_pw1"][i], b_pw1=row(w["conv_b_pw1"][i]),
                       w_dw=w["conv_w_dw"][i].astype(F32), b_dw=row(w["conv_b_dw"][i]),
                       ln_g=row(w["conv_ln_g"][i]), ln_b=row(w["conv_ln_b"][i]), w_pw2=w["conv_w_pw2"][i],
                       b_pw2=row(w["conv_b_pw2"][i]))
        ffn = dict(g2=g[2:3], g3=g[3:4],
                   w_gu=jnp.concatenate([w["ffn_w_gate"][li], w["ffn_w_up"][li]], axis=1),
                   w_down=w["ffn_w_down"][li])
        layers.append(dict(mix=mix, ffn=ffn))
    return layers


def _collect_grads(grads):
    even = [grads[li]["mix"] for li in range(0, DEPTH, 2)]
    odd = [grads[li]["mix"] for li in range(1, DEPTH, 2)]
    ffn = [grads[li]["ffn"] for li in range(DEPTH)]
    st = lambda xs: jnp.stack(xs, axis=0)
    vec = lambda xs, k: st([x[k][0] for x in xs])
    out = {}
    out["norm_g"] = st([jnp.concatenate([grads[li]["mix"]["g0"], grads[li]["mix"]["g1"], grads[li]["ffn"]["g2"],
                                         grads[li]["ffn"]["g3"]], axis=0) for li in range(DEPTH)])
    out["mix_w_in"] = st([jnp.concatenate(
        [g["w_in"][:, :_SPLIT], g["w_in"][:, GATE_COL:GATE_COL + 2 * MLSTM_HEADS], g["w_in"][:, _SPLIT:GATE_COL]],
        axis=1) for g in even])
    out["mix_qk_conv_w"] = st([g["qk_w"] for g in even])
    out["mix_qk_conv_b"] = vec(even, "qk_b")
    out["mix_gate_b"] = st([g["gate_b"][0, :2 * MLSTM_HEADS] for g in even])
    out["mix_hnorm_g"] = vec(even, "hnorm_g")
    out["mix_w_out"] = st([g["w_out"] for g in even])
    out["conv_w_pw1"] = st([g["w_pw1"] for g in odd])
    out["conv_b_pw1"] = vec(odd, "b_pw1")
    out["conv_w_dw"] = st([g["w_dw"] for g in odd])
    out["conv_b_dw"] = vec(odd, "b_dw")
    out["conv_ln_g"] = vec(odd, "ln_g")
    out["conv_ln_b"] = vec(odd, "ln_b")
    out["conv_w_pw2"] = st([g["w_pw2"] for g in odd])
    out["conv_b_pw2"] = vec(odd, "b_pw2")
    out["ffn_w_gate"] = st([g["w_gu"][:, :FFN_HIDDEN] for g in ffn])
    out["ffn_w_up"] = st([g["w_gu"][:, FFN_HIDDEN:] for g in ffn])
    out["ffn_w_down"] = st([g["w_down"] for g in ffn])
    return out


def _local_step(x, target, w):
    seq = x.shape[0]
    h0 = jnp.concatenate([jnp.zeros((PAD_FRONT, D_MODEL), F32), w["meta"].astype(F32), x], axis=0)
    tgt = jnp.concatenate([jnp.zeros((PAD_FRONT + N_META, D_MODEL), F32), target], axis=0)
    loss_cols, dh0, grads = _trunk_step(h0, tgt, _prepare_layers(w))
    out = _collect_grads(grads)
    out["meta"] = dh0[PAD_FRONT:PAD_FRONT + N_META]
    loss = 0.5 * jnp.sum(loss_cols) / D_MODEL
    return loss, dh0[PAD_FRONT + N_META:PAD_FRONT + N_META + seq], out


def _elementwise(fn, arrays, out_dtypes, *, name):
    shape = arrays[0].shape
    cols = shape[-1]
    rows = 1
    for s in shape[:-1]:
        rows *= s
    flat = [a.reshape(rows, cols) for a in arrays]
    if rows * cols * 4 <= (1 << 20) or rows % SUBLANE:
        tr = rows
    else:
        tr = _divisor(rows, (512, 256, 128, 64, 32, 16, 8))
    n = len(flat)

    def body(*refs):
        outs = fn(*[r[...] for r in refs[:n]])
        for o_ref, o in zip(refs[n:], outs):
            o_ref[...] = o.astype(o_ref.dtype)

    spec = pl.BlockSpec((tr, cols), lambda i: (i, 0))
    outs = pl.pallas_call(
        body, name=name, grid=(rows // tr,), in_specs=[spec] * n, out_specs=[spec] * len(out_dtypes),
        out_shape=[jax.ShapeDtypeStruct((rows, cols), dt) for dt in out_dtypes],
        compiler_params=_params(("parallel",)),
    )(*flat)
    return [o.reshape(shape) for o in outs]


def _adamw(w, g, m, v, *, name):
    def fn(wv, gv, mv, vv):
        mn = ADAM_B1 * mv + (1.0 - ADAM_B1) * gv
        vn = ADAM_B2 * vv + (1.0 - ADAM_B2) * (gv * gv)
        m_hat = mn / (1.0 - ADAM_B1 ** ADAM_STEP)
        v_hat = vn / (1.0 - ADAM_B2 ** ADAM_STEP)
        return [-ADAM_LR * (m_hat / (jnp.sqrt(v_hat) + ADAM_EPS) + ADAM_WD * wv), mn, vn]

    return _elementwise(fn, [w, g, m, v], [F32, F32, F32], name=name)


MESH_ID = pl.DeviceIdType.MESH
ANY = pl.BlockSpec(memory_space=pl.ANY)


def _place():
    x, y, c = lax.axis_index("x"), lax.axis_index("y"), lax.axis_index("c")
    return x, y, c, [(1 - x, y), (x, 1 - y), (1 - x, 1 - y)]


def _remote(src, dst, send_sems, recv_sems, k, to):
    return pltpu.make_async_remote_copy(src_ref=src, dst_ref=dst, send_sem=send_sems.at[k], recv_sem=recv_sems.at[k],
                                        device_id=to, device_id_type=MESH_ID)


def _comm_call(body, arrays, out_shapes, n_remote, n_local, name):
    return pl.pallas_call(
        body, name=name, in_specs=[ANY] * len(arrays), out_specs=[ANY] * len(out_shapes), out_shape=out_shapes,
        scratch_shapes=[pltpu.SemaphoreType.DMA((n_remote,)), pltpu.SemaphoreType.DMA((n_remote,)),
                        pltpu.SemaphoreType.DMA((n_local,))],
        compiler_params=pltpu.CompilerParams(has_side_effects=True),
    )(*arrays)


def _gather_chips(shards, *, name):
    n = len(shards)

    def body(*refs):
        ins, outs = refs[:n], refs[n:2 * n]
        send_sems, recv_sems, local_sems = refs[2 * n:]
        x, y, c, chips = _place()
        me, sibling = 2 * x + y, (x, y, 1 - c)

        def half(a, slot, hc):
            hl = ins[a].shape[0] // 2
            return outs[a].at[slot].at[pl.ds(hc * hl, hl)]

        def mine(a):
            hl = ins[a].shape[0] // 2
            return ins[a].at[pl.ds(c * hl, hl)]

        own = [pltpu.make_async_copy(ins[a], outs[a].at[me], local_sems.at[a]) for a in range(n)]
        for cp in own:
            cp.start()
        sent = []
        for a in range(n):
            for j, (px, py) in enumerate(chips):
                sent.append(_remote(mine(a), half(a, me, c), send_sems, recv_sems, 6 * a + j, (px, py, c)))
                sent[-1].start()
        for a in range(n):
            for j, (px, py) in enumerate(chips):
                slot = 2 * px + py
                _remote(mine(a), half(a, slot, c), send_sems, recv_sems, 6 * a + j, (px, py, c)).wait_recv()
                sent.append(_remote(half(a, slot, c), half(a, slot, c), send_sems, recv_sems, 6 * a + 3 + j, sibling))
                sent[-1].start()
        for a in range(n):
            for j, (px, py) in enumerate(chips):
                slot = 2 * px + py
                _remote(mine(a), half(a, slot, 1 - c), send_sems, recv_sems, 6 * a + 3 + j, sibling).wait_recv()
        for cp in sent:
            cp.wait_send()
        for cp in own:
            cp.wait()

    out_shapes = [jax.ShapeDtypeStruct((4,) + s.shape, s.dtype) for s in shards]
    return _comm_call(body, shards, out_shapes, 6 * n, n, name)


def _swap_siblings(arrays, *, name):
    n = len(arrays)

    def body(*refs):
        ins, outs = refs[:n], refs[n:2 * n]
        send_sems, recv_sems, _ = refs[2 * n:]
        x, y, c, _chips = _place()
        cps = [_remote(ins[a].at[1 - c], outs[a], send_sems, recv_sems, a, (x, y, 1 - c)) for a in range(n)]
        for cp in cps:
            cp.start()
        for cp in cps:
            cp.wait()

    out_shapes = [jax.ShapeDtypeStruct(a.shape[1:], a.dtype) for a in arrays]
    return _comm_call(body, arrays, out_shapes, n, 1, name)


def _scatter_chips(parts, small, *, name):
    n = len(parts)

    def body(*refs):
        ins, small_in = refs[:n], refs[n]
        outs, small_out = refs[n + 1:2 * n + 1], refs[2 * n + 1]
        send_sems, recv_sems, local_sems = refs[2 * n + 2:]
        x, y, c, chips = _place()
        me8 = 4 * x + 2 * y + c
        own = pltpu.make_async_copy(small_in, small_out.at[me8], local_sems.at[0])
        own.start()
        cps = []
        for fx in range(2):
            for fy in range(2):
                for fc in range(2):
                    r = 4 * fx + 2 * fy + fc - 1
                    if r >= 0:
                        to = (x + fx - 2 * x * fx, y + fy - 2 * y * fy, c + fc - 2 * c * fc)
                        cps.append(_remote(small_in, small_out.at[me8], send_sems, recv_sems, r, to))
        for a in range(n):
            for j, (px, py) in enumerate(chips):
                cps.append(_remote(ins[a].at[2 * px + py], outs[a].at[j], send_sems, recv_sems, 7 + 3 * a + j,
                                   (px, py, c)))
        for cp in cps:
            cp.start()
        for cp in cps:
            cp.wait()
        own.wait()

    out_shapes = [jax.ShapeDtypeStruct((3,) + p.shape[1:], p.dtype) for p in parts]
    out_shapes.append(jax.ShapeDtypeStruct((8,) + small.shape, small.dtype))
    return _comm_call(body, list(parts) + [small], out_shapes, 7 + 3 * n, 1, name)


def _join_siblings(halves, *, name):
    n = len(halves)

    def body(*refs):
        ins, outs = refs[:n], refs[n:2 * n]
        send_sems, recv_sems, local_sems = refs[2 * n:]
        x, y, c, _chips = _place()
        own = [pltpu.make_async_copy(ins[a], outs[a].at[c], local_sems.at[a]) for a in range(n)]
        cps = [_remote(ins[a], outs[a].at[c], send_sems, recv_sems, a, (x, y, 1 - c)) for a in range(n)]
        for cp in own + cps:
            cp.start()
        for a in range(n):
            cps[a].wait_send()
            _remote(ins[a], outs[a].at[1 - c], send_sems, recv_sems, a, (x, y, 1 - c)).wait_recv()
        for cp in own:
            cp.wait()

    out_shapes = [jax.ShapeDtypeStruct((2,) + h.shape, h.dtype) for h in halves]
    return _comm_call(body, halves, out_shapes, n, n, name)


WEIGHTS = ("meta", "norm_g", "mix_w_in", "mix_qk_conv_w", "mix_qk_conv_b", "mix_gate_b", "mix_hnorm_g", "mix_w_out",
           "conv_w_pw1", "conv_b_pw1", "conv_w_dw", "conv_b_dw", "conv_ln_g", "conv_ln_b", "conv_w_pw2",
           "conv_b_pw2", "ffn_w_gate", "ffn_w_up", "ffn_w_down")
SHARD_AXIS = dict(meta=1, norm_g=2, mix_w_in=2, mix_qk_conv_w=2, mix_qk_conv_b=None, mix_gate_b=None,
                  mix_hnorm_g=None, mix_w_out=1, conv_w_pw1=2, conv_b_pw1=1, conv_w_dw=2, conv_b_dw=1, conv_ln_g=1,
                  conv_ln_b=1, conv_w_pw2=1, conv_b_pw2=1, ffn_w_gate=2, ffn_w_up=2, ffn_w_down=1)
MATRICES = ("mix_w_in", "mix_w_out", "conv_w_pw1", "conv_w_pw2", "ffn_w_gate", "ffn_w_up", "ffn_w_down")
VECTORS = tuple(n for n in WEIGHTS if n not in MATRICES)
GATHER_COLS = D_MODEL // 4


def _pack_rows(arrays, cols, pad_to):
    rows = [a.astype(F32).reshape(-1) for a in arrays]
    rows = [jnp.pad(r, (0, (-r.shape[0]) % cols)).reshape(-1, cols) for r in rows]
    packed = jnp.concatenate(rows, axis=0)
    return jnp.pad(packed, ((0, pad_to - packed.shape[0]), (0, 0))), [r.shape[0] for r in rows]


def _unpack_rows(packed, counts, shapes):
    out, at = [], 0
    for n, shape in zip(counts, shapes):
        size = 1
        for s in shape:
            size *= s
        out.append(packed[..., at:at + n, :].reshape(packed.shape[:-2] + (-1,))[..., :size]
                   .reshape(packed.shape[:-2] + tuple(shape)))
        at += n
    return out


def _gather_weights(local):
    sharded_vecs = [n for n in VECTORS if SHARD_AXIS[n] is not None]
    pack, counts = _pack_rows([local[n] for n in sharded_vecs], GATHER_COLS, 120)
    shards = [local[n].astype(MXU_DTYPE) for n in MATRICES] + [pack.reshape(2, 60, GATHER_COLS)]
    got = _gather_chips(shards, name="gather_weights")
    full = {n: local[n] for n in VECTORS if SHARD_AXIS[n] is None}
    for n, g in zip(MATRICES, got):
        full[n] = jnp.concatenate([g[k] for k in range(4)], axis=SHARD_AXIS[n])
    vecs = _unpack_rows(got[-1].reshape(4, 120, GATHER_COLS), counts, [local[n].shape for n in sharded_vecs])
    for n, v in zip(sharded_vecs, vecs):
        full[n] = jnp.moveaxis(v, 0, -2).reshape(v.shape[1:-1] + (4 * v.shape[-1],))
    return full


def _reduce_grads(grads):
    x, y, c = lax.axis_index("x"), lax.axis_index("y"), lax.axis_index("c")
    me = 2 * x + y
    stacked = []
    for n in MATRICES:
        g = jnp.stack(jnp.split(grads[n], 4, axis=SHARD_AXIS[n]), axis=0)
        g = g.reshape((4, 2, g.shape[1] // 2) + g.shape[2:])
        stacked.append(jnp.swapaxes(g, 0, 1))
    theirs = _swap_siblings(stacked, name="reduce_pair_swap")
    pair = [_elementwise(lambda a, b: [a + b], [lax.dynamic_index_in_dim(s, c, 0, keepdims=False), t], [F32],
                         name=f"reduce_pair_sum_{n}")[0] for n, s, t in zip(MATRICES, stacked, theirs)]
    shapes = [grads[n].shape for n in VECTORS]
    pack, counts = _pack_rows([grads[n] for n in VECTORS], D_MODEL, 120)
    got = _scatter_chips([p.astype(jnp.bfloat16) for p in pair], pack, name="reduce_chips")
    halves = []
    for n, p, r in zip(MATRICES, pair, got[:-1]):
        own = lax.dynamic_index_in_dim(p, me, 0, keepdims=False)
        halves.append(_elementwise(lambda a, b0, b1, b2: [((a + b0.astype(F32)) + b1.astype(F32)) + b2.astype(F32)],
                                   [own, r[0], r[1], r[2]], [F32], name=f"reduce_chip_sum_{n}")[0])
    joined = _join_siblings(halves, name="reduce_join")
    out = {n: j.reshape((j.shape[0] * j.shape[1],) + j.shape[2:]) for n, j in zip(MATRICES, joined)}
    small = got[-1]
    total = _elementwise(lambda *s: [functools.reduce(lambda a, b: a + b, s)], [small[k] for k in range(8)], [F32],
                         name="reduce_small_sum")[0]
    for n, v in zip(VECTORS, _unpack_rows(total, counts, shapes)):
        ax = SHARD_AXIS[n]
        if ax is not None:
            w = v.shape[ax] // 4
            v = lax.dynamic_slice_in_dim(v, me * w, w, axis=ax)
        out[n] = v
    return out


def kernel(x, meta, norm_g, mix_w_in, mix_qk_conv_w, mix_qk_conv_b, mix_gate_b, mix_hnorm_g, mix_w_out, conv_w_pw1, conv_b_pw1, conv_w_dw, conv_b_dw, conv_ln_g, conv_ln_b, conv_w_pw2, conv_b_pw2, ffn_w_gate, ffn_w_up, ffn_w_down, loss_target, m_meta, m_norm_g, m_mix_w_in, m_mix_qk_conv_w, m_mix_qk_conv_b, m_mix_gate_b, m_mix_hnorm_g, m_mix_w_out, m_conv_w_pw1, m_conv_b_pw1, m_conv_w_dw, m_conv_b_dw, m_conv_ln_g, m_conv_ln_b, m_conv_w_pw2, m_conv_b_pw2, m_ffn_w_gate, m_ffn_w_up, m_ffn_w_down, v_meta, v_norm_g, v_mix_w_in, v_mix_qk_conv_w, v_mix_qk_conv_b, v_mix_gate_b, v_mix_hnorm_g, v_mix_w_out, v_conv_w_pw1, v_conv_b_pw1, v_conv_w_dw, v_conv_b_dw, v_conv_ln_g, v_conv_ln_b, v_conv_w_pw2, v_conv_b_pw2, v_ffn_w_gate, v_ffn_w_up, v_ffn_w_down):
    given = dict(locals())
    local = {n: given[n] for n in WEIGHTS}
    full = _gather_weights(local)
    loss, grad_x, grads = _local_step(x[0], loss_target[0], full)
    loss = lax.psum(loss, ("x", "y", "c"))
    grad_w = _reduce_grads(grads)
    delta, new_m, new_v = {}, {}, {}
    for n in WEIGHTS:
        delta[n], new_m[n], new_v[n] = _adamw(local[n], grad_w[n], given["m_" + n], given["v_" + n], name=f"adamw_{n}")
    return (loss, grad_x[None], *[grad_w[n] for n in WEIGHTS], *[delta[n] for n in WEIGHTS],
            *[new_m[n] for n in WEIGHTS], *[new_v[n] for n in WEIGHTS])
```

```python
import functools

import jax
import jax.numpy as jnp
from jax import lax
from jax.experimental import pallas as pl
from jax.experimental.pallas import tpu as pltpu

F32 = jnp.float32
MXU_DTYPE = jnp.bfloat16

D_MODEL = 1024
N_META = 16
DEPTH = 4
MLSTM_HEADS = 4
MLSTM_DQK = 128
MLSTM_DV = 256
MLSTM_CHUNK = 64
QK_CONV_WIDTH = 4
GATE_SOFTCAP = 15.0
SB_HEADS = 4
SB_DH = 128
SB_BLOCK = 128
PAD_FRONT = SB_BLOCK - N_META
CONV_WIDTH = 31
FFN_HIDDEN = 2816
MQK = MLSTM_HEADS * MLSTM_DQK
MV = MLSTM_HEADS * MLSTM_DV
SBW = SB_HEADS * SB_DH
IN_WIDTH = 2 * MQK + 2 * MV + 2 * MLSTM_HEADS + 3 * SBW
MIX_WIDTH = MV + SBW
NEG = -1e30
EPS = 1e-6
PROJ_WIDTH = 5120
GATE_COL = 3 * MV + 3 * SBW
LANE = 128
SUBLANE = 8
CONV_HALO = 32
VMEM_LIMIT = 56 * 1024 * 1024

ADAM_LR = 0.001
ADAM_B1 = 0.9
ADAM_B2 = 0.999
ADAM_EPS = 1e-08
ADAM_WD = 0.01
ADAM_STEP = 10


def _divisor(n, cands):
    for c in cands:
        if n % c == 0:
            return c
    raise ValueError(f"no tile for {n} in {cands}")


ROW_TILE_BYTES = 20 * 1024 * 1024


def _row_tile(tp, width=D_MODEL):
    for c in (640, 512, 384, 320, 256, 128, 64):
        if tp % c == 0 and c * width * 8 <= ROW_TILE_BYTES:
            return c
    raise ValueError(f"no row tile for {tp} x {width}")


def _params(sem):
    return pltpu.CompilerParams(dimension_semantics=sem, vmem_limit_bytes=VMEM_LIMIT)


def _dot(a, b, dims):
    return lax.dot_general(a.astype(MXU_DTYPE), b.astype(MXU_DTYPE), (dims, ((), ())),
                           preferred_element_type=F32)


def _nn(a, b):
    return _dot(a, b, ((1,), (0,)))


def _nt(a, b):
    return _dot(a, b, ((1,), (1,)))


def _tn(a, b):
    return _dot(a, b, ((0,), (0,)))


def _sigmoid(x):
    return 1.0 / (1.0 + jnp.exp(-x))


def _softplus(x):
    return jnp.maximum(x, 0.0) + jnp.log(1.0 + jnp.exp(-jnp.abs(x)))


def _matmul(a, b, *, ta=False, tb=False, name):
    m, k = (a.shape[1], a.shape[0]) if ta else a.shape
    n = b.shape[0] if tb else b.shape[1]
    assert (b.shape[1] if tb else b.shape[0]) == k, (a.shape, b.shape, ta, tb)
    tm = _divisor(m, (640, 512, 384, 256, 128))
    tk = _divisor(k, (1408, 1024, 768, 640, 512, 384, 256, 128))
    tn = _divisor(n, (1408, 1024, 768, 512, 256, 128))
    nk = k // tk

    def body(a_ref, b_ref, o_ref, acc_ref):
        kk = pl.program_id(2)

        @pl.when(kk == 0)
        def _():
            acc_ref[...] = jnp.zeros_like(acc_ref)

        acc_ref[...] += _dot(a_ref[...], b_ref[...], ((0 if ta else 1,), (1 if tb else 0,)))

        @pl.when(kk == nk - 1)
        def _():
            o_ref[...] = acc_ref[...]

    a_spec = (pl.BlockSpec((tk, tm), lambda i, j, kk: (kk, i)) if ta
              else pl.BlockSpec((tm, tk), lambda i, j, kk: (i, kk)))
    b_spec = (pl.BlockSpec((tn, tk), lambda i, j, kk: (j, kk)) if tb
              else pl.BlockSpec((tk, tn), lambda i, j, kk: (kk, j)))
    return pl.pallas_call(
        body, name=name, grid=(m // tm, n // tn, nk),
        in_specs=[a_spec, b_spec],
        out_specs=pl.BlockSpec((tm, tn), lambda i, j, kk: (i, j)),
        out_shape=jax.ShapeDtypeStruct((m, n), F32),
        scratch_shapes=[pltpu.VMEM((tm, tn), F32)],
        compiler_params=_params(("parallel", "parallel", "arbitrary")),
    )(a, b)


def _rowwise(fn, rows, fulls, out_rows, out_accs, *, name, out_dtypes=None):
    tp = rows[0][0].shape[0]
    tm = _row_tile(tp, sum(w for _, _, w in rows) + sum(out_rows))
    nr, nf, no, na = len(rows), len(fulls), len(out_rows), len(out_accs)
    out_dtypes = out_dtypes or [F32] * no

    def body(*refs):
        i = pl.program_id(0)
        outs = fn(i * tm, *[r[...].astype(F32) for r in refs[:nr + nf]])
        for k in range(no):
            refs[nr + nf + k][...] = outs[k].astype(out_dtypes[k])
        for k in range(na):
            ref = refs[nr + nf + no + k]

            @pl.when(i == 0)
            def _(ref=ref):
                ref[...] = jnp.zeros_like(ref)

            ref[...] += outs[no + k]

    in_specs = [pl.BlockSpec((tm, w), functools.partial(lambda i, cb: (i, cb), cb=cb)) for _, cb, w in rows]
    in_specs += [pl.BlockSpec(f.shape, lambda i: (0, 0)) for f in fulls]
    out_specs = [pl.BlockSpec((tm, w), lambda i: (i, 0)) for w in out_rows]
    out_specs += [pl.BlockSpec(s, lambda i: (0, 0)) for s in out_accs]
    out_shape = [jax.ShapeDtypeStruct((tp, w), dt) for w, dt in zip(out_rows, out_dtypes)]
    out_shape += [jax.ShapeDtypeStruct(s, F32) for s in out_accs]
    return pl.pallas_call(
        body, name=name, grid=(tp // tm,), in_specs=in_specs, out_specs=out_specs, out_shape=out_shape,
        compiler_params=_params(("arbitrary",)),
    )(*[r[0] for r in rows], *fulls)


def _whole(a):
    return (a, 0, a.shape[1])


def _live(row0, tm):
    return (row0 + lax.broadcasted_iota(jnp.int32, (tm, 1), 0)) >= PAD_FRONT


def _rms_core(x, g):
    r = lax.rsqrt(jnp.mean(x * x, axis=-1, keepdims=True) + EPS)
    return x * r, r


def _rms_fwd(x, g, *, name, res=None, bias=None, out_dtype=F32):
    def fn(row0, *blk):
        it = iter(blk)
        xv = next(it)
        rv = next(it) if res is not None else None
        gv = next(it)
        if bias is not None:
            xv = xv + next(it)
        xh, _ = _rms_core(xv, gv)
        y = jnp.where(_live(row0, xv.shape[0]), xh * gv, 0.0)
        return [y + rv if rv is not None else y]

    rows = [_whole(x)] + ([_whole(res)] if res is not None else [])
    fulls = [g] + ([bias] if bias is not None else [])
    return _rowwise(fn, rows, fulls, [x.shape[1]], [], name=name, out_dtypes=[out_dtype])[0]


def _rms_bwd(x, g, dy, *, name, add=None, bias=None, out_dtype=F32):
    def fn(row0, *blk):
        it = iter(blk)
        xv, dyv = next(it), next(it)
        av = next(it) if add is not None else None
        gv = next(it)
        if bias is not None:
            xv = xv + next(it)
        dyv = jnp.where(_live(row0, xv.shape[0]), dyv, 0.0)
        xh, r = _rms_core(xv, gv)
        dyg = dyv * gv
        dx = r * (dyg - xh * jnp.mean(dyg * xh, axis=-1, keepdims=True))
        outs = [dx + av if av is not None else dx, jnp.sum(dyv * xh, axis=0, keepdims=True)]
        if bias is not None:
            outs.append(jnp.sum(dx, axis=0, keepdims=True))
        return outs

    rows = [_whole(x), _whole(dy)] + ([_whole(add)] if add is not None else [])
    fulls = [g] + ([bias] if bias is not None else [])
    c = x.shape[1]
    return _rowwise(fn, rows, fulls, [c], [(1, c)] * (2 if bias is not None else 1), name=name,
                    out_dtypes=[out_dtype])


def _swiglu_fwd(ab, *, name):
    h = ab.shape[1] // 2

    def fn(row0, a, b):
        return [a * _sigmoid(a) * b]

    return _rowwise(fn, [(ab, 0, h), (ab, 1, h)], [], [h], [], name=name, out_dtypes=[MXU_DTYPE])[0]


def _swiglu_bwd(ab, ds, *, name):
    h = ab.shape[1] // 2

    def fn(row0, a, b, d):
        sg = _sigmoid(a)
        return [jnp.concatenate([d * b * sg * (1.0 + a * (1.0 - sg)), d * a * sg], axis=1)]

    return _rowwise(fn, [(ab, 0, h), (ab, 1, h), _whole(ds)], [], [2 * h], [], name=name,
                    out_dtypes=[MXU_DTYPE])[0]


def _glu_fwd(z, b, *, name):
    h = z.shape[1] // 2

    def fn(row0, a, gt, bv):
        y = (a + bv[:, :h]) * _sigmoid(gt + bv[:, h:])
        return [jnp.where(_live(row0, a.shape[0]), y, 0.0)]

    return _rowwise(fn, [(z, 0, h), (z, 1, h)], [b], [h], [], name=name)[0]


def _glu_bwd(z, b, dy, *, name):
    h = z.shape[1] // 2

    def fn(row0, a, gt, d, bv):
        d = jnp.where(_live(row0, a.shape[0]), d, 0.0)
        sg = _sigmoid(gt + bv[:, h:])
        dz = jnp.concatenate([d * sg, d * (a + bv[:, :h]) * sg * (1.0 - sg)], axis=1)
        return [dz, jnp.sum(dz, axis=0, keepdims=True)]

    return _rowwise(fn, [(z, 0, h), (z, 1, h), _whole(dy)], [b], [2 * h], [(1, 2 * h)], name=name,
                    out_dtypes=[MXU_DTYPE])


def _ln_core(x):
    mu = jnp.mean(x, axis=-1, keepdims=True)
    xc = x - mu
    r = lax.rsqrt(jnp.mean(xc * xc, axis=-1, keepdims=True) + EPS)
    return xc * r, r


def _lnsilu_fwd(x, g, b, *, name):
    def fn(row0, xv, gv, bv):
        xh, _ = _ln_core(xv)
        v = xh * gv + bv
        return [v * _sigmoid(v)]

    return _rowwise(fn, [_whole(x)], [g, b], [x.shape[1]], [], name=name, out_dtypes=[MXU_DTYPE])[0]


def _lnsilu_bwd(x, g, b, dy, *, name):
    def fn(row0, xv, d, gv, bv):
        xh, r = _ln_core(xv)
        v = xh * gv + bv
        sg = _sigmoid(v)
        dv = d * sg * (1.0 + v * (1.0 - sg))
        dxh = dv * gv
        dx = r * (dxh - jnp.mean(dxh, axis=-1, keepdims=True) - xh * jnp.mean(dxh * xh, axis=-1, keepdims=True))
        return [dx, jnp.sum(dv * xh, axis=0, keepdims=True), jnp.sum(dv, axis=0, keepdims=True)]

    c = x.shape[1]
    return _rowwise(fn, [_whole(x), _whole(dy)], [g, b], [c], [(1, c), (1, c)], name=name)


def _silu_fwd(x, *, name):
    return _rowwise(lambda row0, v: [v * _sigmoid(v)], [_whole(x)], [], [x.shape[1]], [], name=name)[0]


def _silu_bwd(x, dy, *, name):
    def fn(row0, v, d):
        sg = _sigmoid(v)
        return [d * sg * (1.0 + v * (1.0 - sg))]

    return _rowwise(fn, [_whole(x), _whole(dy)], [], [x.shape[1]], [], name=name)[0]


def _gate_parts(row0, pg, gb):
    lane = lax.broadcasted_iota(jnp.int32, pg.shape, 1)
    th = jnp.tanh((pg + gb) / GATE_SOFTCAP)
    s = GATE_SOFTCAP * th
    return lane, th, s, _live(row0, pg.shape[0])


def _gates_fwd(proj, gate_b, *, name):
    def fn(row0, pg, gb):
        lane, th, s, live = _gate_parts(row0, pg, gb)
        li = jnp.where(live, s, NEG)
        lf = jnp.where(live, -_softplus(-s), 0.0)
        return [jnp.where(lane < MLSTM_HEADS, li, jnp.where(lane < 2 * MLSTM_HEADS, lf, 0.0))]

    return _rowwise(fn, [(proj, GATE_COL // LANE, LANE)], [gate_b], [LANE], [], name=name)[0]


def _gates_bwd(proj, gate_b, dgl, *, name):
    def fn(row0, pg, d, gb):
        lane, th, s, live = _gate_parts(row0, pg, gb)
        ds = jnp.where(lane < MLSTM_HEADS, d, d * _sigmoid(-s))
        ds = jnp.where(live & (lane < 2 * MLSTM_HEADS), ds, 0.0)
        dp = ds * (1.0 - th * th)
        return [dp, jnp.sum(dp, axis=0, keepdims=True)]

    return _rowwise(fn, [(proj, GATE_COL // LANE, LANE), _whole(dgl)], [gate_b], [LANE], [(1, LANE)], name=name)


def _head_rms(h):
    parts = [h[:, i * MLSTM_DV:(i + 1) * MLSTM_DV] for i in range(MLSTM_HEADS)]
    rs = [lax.rsqrt(jnp.mean(p * p, axis=-1, keepdims=True) + EPS) for p in parts]
    return parts, rs


def _hnorm_fwd(hm, proj, g, *, name):
    def fn(row0, h, o, gv):
        parts, rs = _head_rms(h)
        xh = jnp.concatenate([p * r for p, r in zip(parts, rs)], axis=1)
        return [xh * gv * _sigmoid(o)]

    return _rowwise(fn, [_whole(hm), (proj, 2, MV)], [g], [MV], [], name=name)[0]


def _hnorm_bwd(hm, proj, g, dmixed, *, name):
    def fn(row0, h, o, d, gv):
        parts, rs = _head_rms(h)
        so = _sigmoid(o)
        dn = d * so
        dxs, xhs = [], []
        for i, (p, r) in enumerate(zip(parts, rs)):
            sl = slice(i * MLSTM_DV, (i + 1) * MLSTM_DV)
            xh = p * r
            dyg = dn[:, sl] * gv[:, sl]
            dxs.append(r * (dyg - xh * jnp.mean(dyg * xh, axis=-1, keepdims=True)))
            xhs.append(xh)
        xh = jnp.concatenate(xhs, axis=1)
        return [jnp.concatenate(dxs, axis=1), d * xh * gv * so * (1.0 - so), jnp.sum(dn * xh, axis=0, keepdims=True)]

    return _rowwise(fn, [_whole(hm), (proj, 2, MV), (dmixed, 0, MV)], [g], [MV, MV], [(1, MV)], name=name)


def _loss_fwd_bwd(h, target, *, name):
    first = PAD_FRONT + N_META

    def fn(row0, hv, tv):
        rows = row0 + lax.broadcasted_iota(jnp.int32, (hv.shape[0], 1), 0)
        e = jnp.where(rows >= first, hv - tv, 0.0)
        return [e * (1.0 / D_MODEL), jnp.sum(e * e, axis=0, keepdims=True)]

    return _rowwise(fn, [_whole(h), _whole(target)], [], [D_MODEL], [(1, D_MODEL)], name=name)


def _conv_tiles(tp, c):
    return _row_tile(tp), _divisor(c, (256, 128))


def _conv_fwd(x, w, b, *, name):
    tp, (k, c) = x.shape[0], w.shape
    tm, tc = _conv_tiles(tp, c)
    base = CONV_HALO - (k - 1)

    def body(x_ref, xp_ref, w_ref, b_ref, o_ref, win):
        i = pl.program_id(1)
        win[0:CONV_HALO, :] = jnp.where(i > 0, xp_ref[tm - CONV_HALO:tm, :], 0.0)
        win[CONV_HALO:CONV_HALO + tm, :] = x_ref[...]
        acc = jnp.broadcast_to(b_ref[...], (tm, tc))
        for j in range(k):
            acc = acc + w_ref[j:j + 1, :] * win[base + j:base + j + tm, :]
        o_ref[...] = acc

    return pl.pallas_call(
        body, name=name, grid=(c // tc, tp // tm),
        in_specs=[pl.BlockSpec((tm, tc), lambda cc, i: (i, cc)),
                  pl.BlockSpec((tm, tc), lambda cc, i: (jnp.maximum(i - 1, 0), cc)),
                  pl.BlockSpec((k, tc), lambda cc, i: (0, cc)),
                  pl.BlockSpec((1, tc), lambda cc, i: (0, cc))],
        out_specs=pl.BlockSpec((tm, tc), lambda cc, i: (i, cc)),
        out_shape=jax.ShapeDtypeStruct((tp, c), F32),
        scratch_shapes=[pltpu.VMEM((CONV_HALO + tm, tc), F32)],
        compiler_params=_params(("parallel", "arbitrary")),
    )(x, x, w, b)


def _conv_bwd(x, w, dy, *, name):
    tp, (k, c) = x.shape[0], w.shape
    tm, tc = _conv_tiles(tp, c)
    nt = tp // tm
    base = CONV_HALO - (k - 1)

    def body(x_ref, xp_ref, d_ref, dn_ref, w_ref, dx_ref, dw_ref, db_ref, winx, wind):
        i = pl.program_id(1)
        winx[0:CONV_HALO, :] = jnp.where(i > 0, xp_ref[tm - CONV_HALO:tm, :], 0.0)
        winx[CONV_HALO:CONV_HALO + tm, :] = x_ref[...]
        d = d_ref[...]
        wind[0:tm, :] = d
        wind[tm:tm + CONV_HALO, :] = jnp.where(i < nt - 1, dn_ref[0:CONV_HALO, :], 0.0)

        @pl.when(i == 0)
        def _():
            dw_ref[...] = jnp.zeros_like(dw_ref)
            db_ref[...] = jnp.zeros_like(db_ref)

        acc = jnp.zeros((tm, tc), F32)
        for j in range(k):
            acc = acc + w_ref[j:j + 1, :] * wind[k - 1 - j:k - 1 - j + tm, :]
            dw_ref[j:j + 1, :] += jnp.sum(d * winx[base + j:base + j + tm, :], axis=0, keepdims=True)
        dx_ref[...] = acc
        db_ref[...] += jnp.sum(d, axis=0, keepdims=True)

    return pl.pallas_call(
        body, name=name, grid=(c // tc, nt),
        in_specs=[pl.BlockSpec((tm, tc), lambda cc, i: (i, cc)),
                  pl.BlockSpec((tm, tc), lambda cc, i: (jnp.maximum(i - 1, 0), cc)),
                  pl.BlockSpec((tm, tc), lambda cc, i: (i, cc)),
                  pl.BlockSpec((tm, tc), lambda cc, i: (jnp.minimum(i + 1, nt - 1), cc)),
                  pl.BlockSpec((k, tc), lambda cc, i: (0, cc))],
        out_specs=[pl.BlockSpec((tm, tc), lambda cc, i: (i, cc)),
                   pl.BlockSpec((k, tc), lambda cc, i: (0, cc)),
                   pl.BlockSpec((1, tc), lambda cc, i: (0, cc))],
        out_shape=[jax.ShapeDtypeStruct((tp, c), F32), jax.ShapeDtypeStruct((k, c), F32),
                   jax.ShapeDtypeStruct((1, c), F32)],
        scratch_shapes=[pltpu.VMEM((CONV_HALO + tm, tc), F32), pltpu.VMEM((CONV_HALO + tm, tc), F32)],
        compiler_params=_params(("parallel", "arbitrary")),
    )(x, x, dy, dy, w)


def _chunk_masks():
    L = MLSTM_CHUNK
    r = lax.broadcasted_iota(jnp.int32, (L, L), 0)
    c = lax.broadcasted_iota(jnp.int32, (L, L), 1)
    return r == c, c <= r, r <= c


def _to_row(col, eye):
    return jnp.sum(jnp.where(eye, col, 0.0), axis=0, keepdims=True)


def _to_col(row, eye):
    return jnp.sum(jnp.where(eye, row, 0.0), axis=1, keepdims=True)


def _mlstm_chunk(q, k, v, li_c, lf_c, c_st, n_st, m_st, masks):
    eye, low, up = masks
    li_r, lf_r = _to_row(li_c, eye), _to_row(lf_c, eye)
    b_c = jnp.sum(jnp.where(low, lf_r, 0.0), axis=1, keepdims=True)
    b_r = jnp.sum(jnp.where(up, lf_c, 0.0), axis=0, keepdims=True)
    g = jnp.sum(lf_c, axis=0, keepdims=True)
    dm = jnp.where(low, b_c - b_r + li_r, NEG)
    inter = b_c + m_st
    mt = jnp.maximum(inter, jnp.max(dm, axis=1, keepdims=True))
    wi = jnp.exp(dm - mt)
    wint = jnp.exp(inter - mt)
    s = _nt(q, k) * wi
    qc = _nn(q, c_st)
    qn = jnp.sum(q * n_st, axis=1, keepdims=True)
    num = _nn(s, v) + wint * qc
    den = jnp.sum(s, axis=1, keepdims=True) + wint * qn
    floor = jnp.exp(-mt)
    a_c = g - b_c + li_c
    a_r = g - b_r + li_r
    mnew = jnp.maximum(g + m_st, jnp.max(a_r, axis=1, keepdims=True))
    wa_c = jnp.exp(a_c - mnew)
    wc = jnp.exp(g + m_st - mnew)
    return dict(wi=wi, wint=wint, s=s, qc=qc, qn=qn, num=num, den=den, floor=floor, mnew=mnew, wa_c=wa_c, wc=wc)


def _mlstm_fwd(qk, proj, gl, *, name):
    tp = qk.shape[0]
    L, H, dk, dv = MLSTM_CHUNK, MLSTM_HEADS, MLSTM_DQK, MLSTM_DV
    nc = tp // L

    def body(qk_ref, v_ref, gl_ref, h_ref, call_ref, nall_ref, mall_ref, c_s, n_s, m_s):
        @pl.when(pl.program_id(0) == 0)
        def _():
            c_s[...] = jnp.zeros_like(c_s)
            n_s[...] = jnp.zeros_like(n_s)
            m_s[...] = jnp.zeros_like(m_s)

        masks = _chunk_masks()
        gates = gl_ref[...]
        for h in range(H):
            c_st, n_st, m_row = c_s[h], n_s[h], m_s[h]
            call_ref[0, h] = c_st
            nall_ref[0, h] = n_st
            mall_ref[0, h] = m_row
            q = qk_ref[:, h * dk:(h + 1) * dk] * (dk ** -0.5)
            k = qk_ref[:, MQK + h * dk:MQK + (h + 1) * dk]
            v = v_ref[:, h * dv:(h + 1) * dv]
            f = _mlstm_chunk(q, k, v, gates[:, h:h + 1], gates[:, H + h:H + h + 1], c_st, n_st, m_row[:, 0:1], masks)
            h_ref[:, h * dv:(h + 1) * dv] = f["num"] / jnp.maximum(jnp.abs(f["den"]), f["floor"])
            kw = k * f["wa_c"]
            c_s[h] = f["wc"] * c_st + _tn(kw, v)
            n_s[h] = f["wc"] * n_st + jnp.sum(kw, axis=0, keepdims=True)
            m_s[h] = jnp.broadcast_to(f["mnew"], (1, LANE))

    return pl.pallas_call(
        body, name=name, grid=(nc,),
        in_specs=[pl.BlockSpec((L, 2 * MQK), lambda i: (i, 0)),
                  pl.BlockSpec((L, MV), lambda i: (i, 1)),
                  pl.BlockSpec((L, LANE), lambda i: (i, 0))],
        out_specs=[pl.BlockSpec((L, MV), lambda i: (i, 0)),
                   pl.BlockSpec((1, H, dk, dv), lambda i: (i, 0, 0, 0)),
                   pl.BlockSpec((1, H, 1, dk), lambda i: (i, 0, 0, 0)),
                   pl.BlockSpec((1, H, 1, LANE), lambda i: (i, 0, 0, 0))],
        out_shape=[jax.ShapeDtypeStruct((tp, MV), F32),
                   jax.ShapeDtypeStruct((nc, H, dk, dv), F32),
                   jax.ShapeDtypeStruct((nc, H, 1, dk), F32),
                   jax.ShapeDtypeStruct((nc, H, 1, LANE), F32)],
        scratch_shapes=[pltpu.VMEM((H, dk, dv), F32), pltpu.VMEM((H, 1, dk), F32), pltpu.VMEM((H, 1, LANE), F32)],
        compiler_params=_params(("arbitrary",)),
    )(qk, proj, gl)


def _mlstm_bwd(qk, proj, gl, dmix, call, nall, mall, *, name):
    tp = qk.shape[0]
    L, H, dk, dv = MLSTM_CHUNK, MLSTM_HEADS, MLSTM_DQK, MLSTM_DV
    nc = tp // L

    def body(qk_ref, v_ref, gl_ref, dh_ref, call_ref, nall_ref, mall_ref, dqk_ref, dv_ref, dgl_ref, dc_s, dn_s):
        @pl.when(pl.program_id(0) == 0)
        def _():
            dc_s[...] = jnp.zeros_like(dc_s)
            dn_s[...] = jnp.zeros_like(dn_s)

        masks = _chunk_masks()
        eye, low, up = masks
        gates = gl_ref[...]
        lane = lax.broadcasted_iota(jnp.int32, (L, LANE), 1)
        dgl = jnp.zeros((L, LANE), F32)
        for h in range(H):
            c_st, n_st, m_st = call_ref[0, h], nall_ref[0, h], mall_ref[0, h][:, 0:1]
            q = qk_ref[:, h * dk:(h + 1) * dk] * (dk ** -0.5)
            k = qk_ref[:, MQK + h * dk:MQK + (h + 1) * dk]
            v = v_ref[:, h * dv:(h + 1) * dv]
            dh = dh_ref[:, h * dv:(h + 1) * dv]
            dcn, dnn = dc_s[h], dn_s[h]
            f = _mlstm_chunk(q, k, v, gates[:, h:h + 1], gates[:, H + h:H + h + 1], c_st, n_st, m_st, masks)
            wint, s, wa_c, wc = f["wint"], f["s"], f["wa_c"], f["wc"]

            scale = jnp.maximum(jnp.abs(f["den"]), f["floor"])
            r = 1.0 / scale
            dnum = dh * r
            dscale = -jnp.sum(dh * f["num"], axis=1, keepdims=True) * r * r
            dden = jnp.where(jnp.abs(f["den"]) > f["floor"], dscale * jnp.sign(f["den"]), 0.0)
            ds = _nt(dnum, v) + dden
            wd = wint * dnum
            dwint = jnp.sum(dnum * f["qc"], axis=1, keepdims=True) + dden * f["qn"]
            dd = ds * s
            da_mat = ds * f["wi"]
            dq = _nt(wd, c_st) + (dden * wint) * n_st + _nn(da_mat, k)
            dk_ = _tn(da_mat, q)
            dv_ = _tn(s, dnum)
            dc_acc = _tn(q, wd)
            dn_acc = jnp.sum(q * (dden * wint), axis=0, keepdims=True)

            kd = _nn(k, dcn)
            dk_ = dk_ + wa_c * (_nt(v, dcn) + dnn)
            dv_ = dv_ + wa_c * kd
            dwa = jnp.sum(kd * v, axis=1, keepdims=True) + jnp.sum(k * dnn, axis=1, keepdims=True)
            dwc = jnp.sum(jnp.sum(dcn * c_st, axis=1, keepdims=True), axis=0, keepdims=True) \
                + jnp.sum(dnn * n_st, axis=1, keepdims=True)
            da_c = dwa * wa_c
            dg = jnp.sum(da_c, axis=0, keepdims=True) + dwc * wc

            dd_cols = jnp.sum(dd, axis=0, keepdims=True)
            db_c = dwint * wint + jnp.sum(dd, axis=1, keepdims=True) - da_c
            db_r = _to_row(db_c, eye) - dd_cols
            dlf = jnp.sum(jnp.where(up, db_r, 0.0), axis=1, keepdims=True) + dg
            dli = da_c + _to_col(dd_cols, eye)

            dc_s[h] = wc * dcn + dc_acc
            dn_s[h] = wc * dnn + dn_acc
            dqk_ref[:, h * dk:(h + 1) * dk] = dq * (dk ** -0.5)
            dqk_ref[:, MQK + h * dk:MQK + (h + 1) * dk] = dk_
            dv_ref[:, h * dv:(h + 1) * dv] = dv_
            dgl = dgl + jnp.where(lane == h, dli, 0.0) + jnp.where(lane == H + h, dlf, 0.0)
        dgl_ref[...] = dgl

    rev = lambda i: nc - 1 - i
    return pl.pallas_call(
        body, name=name, grid=(nc,),
        in_specs=[pl.BlockSpec((L, 2 * MQK), lambda i: (rev(i), 0)),
                  pl.BlockSpec((L, MV), lambda i: (rev(i), 1)),
                  pl.BlockSpec((L, LANE), lambda i: (rev(i), 0)),
                  pl.BlockSpec((L, MV), lambda i: (rev(i), 0)),
                  pl.BlockSpec((1, H, dk, dv), lambda i: (rev(i), 0, 0, 0)),
                  pl.BlockSpec((1, H, 1, dk), lambda i: (rev(i), 0, 0, 0)),
                  pl.BlockSpec((1, H, 1, LANE), lambda i: (rev(i), 0, 0, 0))],
        out_specs=[pl.BlockSpec((L, 2 * MQK), lambda i: (rev(i), 0)),
                   pl.BlockSpec((L, MV), lambda i: (rev(i), 0)),
                   pl.BlockSpec((L, LANE), lambda i: (rev(i), 0))],
        out_shape=[jax.ShapeDtypeStruct((tp, 2 * MQK), F32), jax.ShapeDtypeStruct((tp, MV), F32),
                   jax.ShapeDtypeStruct((tp, LANE), F32)],
        scratch_shapes=[pltpu.VMEM((H, dk, dv), F32), pltpu.VMEM((H, 1, dk), F32)],
        compiler_params=_params(("arbitrary",)),
    )(qk, proj, gl, dmix, call, nall, mall)


SB_Q0 = 3 * MV // LANE
SB_K0 = SB_Q0 + SB_HEADS
SB_V0 = SB_K0 + SB_HEADS


def _cumsum_dot(x, tri):
    hi = x.astype(jnp.bfloat16)
    lo = (x - hi.astype(F32)).astype(jnp.bfloat16)
    dims = (((1,), (0,)), ((), ()))
    return (lax.dot_general(hi, tri, dims, preferred_element_type=F32)
            + lax.dot_general(lo, tri, dims, preferred_element_type=F32))


def _sb_query_blocks(nq):
    return _divisor(nq, (5, 4, 3, 2, 1))


def _sb_mask(tile, g, tq):
    t_idx = tile * tq + lax.broadcasted_iota(jnp.int32, (tq, tq), 0)
    s_idx = g * tq + lax.broadcasted_iota(jnp.int32, (tq, tq), 1)
    return (s_idx < t_idx) & (s_idx >= PAD_FRONT)


def _sb_blocks(x):
    return [x[:, k * SB_BLOCK:(k + 1) * SB_BLOCK] for k in range(x.shape[1] // SB_BLOCK)]


def _sb_logits(qb, kg, tri, mask):
    z = _nt(qb, kg) * (SB_DH ** -0.5)
    l = jnp.minimum(-z, 0.0) - jnp.log(1.0 + jnp.exp(-jnp.abs(z)))
    if mask is not None:
        l = jnp.where(mask, l, 0.0)
    return z, l, [_cumsum_dot(b, tri) for b in _sb_blocks(l)]


def _sb_weights(z, withins, runs, mask):
    e = jnp.exp(z + jnp.concatenate([w + r for w, r in zip(withins, runs)], axis=1))
    return e if mask is None else jnp.where(mask, e, 0.0)


def _sb_segments(tile, group, descending):
    def diagonal():
        group(tile, True)

    def interior():
        def it(gg, c):
            group(tile - 1 - gg if descending else 1 + gg, False)
            return c
        lax.fori_loop(0, jnp.maximum(tile - 1, 0), it, 0)

    def first():
        @pl.when(tile > 0)
        def _():
            group(0, True)

    for part in ((diagonal, interior, first) if descending else (first, interior, diagonal)):
        part()


def _sb_fwd(proj, *, name):
    tp = proj.shape[0]
    B, H = SB_BLOCK, SB_HEADS
    nq = tp // B
    assert nq <= LANE and B == LANE
    r = _sb_query_blocks(nq)
    tq = r * B

    def body(q_ref, k_ref, v_ref, o_ref, ac_ref, run_s):
        tile = pl.program_id(1)
        qb = q_ref[...].astype(MXU_DTYPE)
        lane = lax.broadcasted_iota(jnp.int32, (tq, LANE), 1)
        tri = (lax.broadcasted_iota(jnp.int32, (B, B), 0) >= lax.broadcasted_iota(jnp.int32, (B, B), 1)
               ).astype(jnp.bfloat16)
        o_ref[...] = jnp.zeros_like(o_ref)
        ac_ref[0, 0] = jnp.zeros((tq, LANE), F32)
        run_s[...] = jnp.zeros_like(run_s)

        def group(g, masked):
            grows = pl.ds(pl.multiple_of(g * tq, tq), tq)
            mask = _sb_mask(tile, g, tq) if masked else None
            z, l, withins = _sb_logits(qb, k_ref[grows, :], tri, mask)
            run, saved, runs = run_s[...], ac_ref[0, 0], [None] * r
            for k in reversed(range(r)):
                runs[k] = run
                saved = jnp.where(lane == g * r + k, run, saved)
                run = run + withins[k][:, 0:1]
            o_ref[...] += _nn(_sb_weights(z, withins, runs, mask), v_ref[grows, :])
            ac_ref[0, 0] = saved
            run_s[...] = run

        _sb_segments(tile, group, descending=True)

    return pl.pallas_call(
        body, name=name, grid=(H, nq // r), scratch_shapes=[pltpu.VMEM((tq, LANE), F32)],
        in_specs=[pl.BlockSpec((tq, SB_DH), lambda h, i: (i, SB_Q0 + h)),
                  pl.BlockSpec((tp, SB_DH), lambda h, i: (0, SB_K0 + h)),
                  pl.BlockSpec((tp, SB_DH), lambda h, i: (0, SB_V0 + h))],
        out_specs=[pl.BlockSpec((tq, SB_DH), lambda h, i: (i, h)),
                   pl.BlockSpec((1, 1, tq, LANE), lambda h, i: (h, i, 0, 0))],
        out_shape=[jax.ShapeDtypeStruct((tp, SBW), F32), jax.ShapeDtypeStruct((H, nq // r, tq, LANE), F32)],
        compiler_params=_params(("parallel", "arbitrary")),
    )(proj, proj, proj)


def _sb_bwd(proj, across, dmix, *, name):
    tp = proj.shape[0]
    B, H = SB_BLOCK, SB_HEADS
    nq = tp // B
    r = _sb_query_blocks(nq)
    tq = r * B
    do0 = MV // LANE

    def body(q_ref, k_ref, v_ref, ac_ref, do_ref, dq_ref, dk_ref, dv_ref, gpre_s):
        tile = pl.program_id(1)

        @pl.when(tile == 0)
        def _():
            dk_ref[...] = jnp.zeros_like(dk_ref)
            dv_ref[...] = jnp.zeros_like(dv_ref)

        dq_ref[...] = jnp.zeros_like(dq_ref)
        gpre_s[...] = jnp.zeros_like(gpre_s)
        qb, dob = q_ref[...].astype(MXU_DTYPE), do_ref[...].astype(MXU_DTYPE)
        lane = lax.broadcasted_iota(jnp.int32, (tq, LANE), 1)
        rr = lax.broadcasted_iota(jnp.int32, (B, B), 0)
        cc = lax.broadcasted_iota(jnp.int32, (B, B), 1)
        tri = (rr >= cc).astype(jnp.bfloat16)
        prefix = (rr <= cc).astype(jnp.bfloat16)
        scale = SB_DH ** -0.5

        def group(g, masked):
            grows = pl.ds(pl.multiple_of(g * tq, tq), tq)
            kg, vg = k_ref[grows, :], v_ref[grows, :]
            mask = _sb_mask(tile, g, tq) if masked else None
            z, l, withins = _sb_logits(qb, kg, tri, mask)
            saved = ac_ref[0, 0]
            runs = [jnp.sum(jnp.where(lane == g * r + k, saved, 0.0), axis=1, keepdims=True) for k in range(r)]
            w = _sb_weights(z, withins, runs, mask)
            dv_ref[grows, :] += _tn(w, dob)
            gw = _nt(dob, vg) * w
            gpre, gcum = gpre_s[...], []
            for gc in [_cumsum_dot(b, prefix) for b in _sb_blocks(gw)]:
                gcum.append(gc + gpre)
                gpre = gpre + gc[:, B - 1:B]
            beta_g = jnp.exp(z + l) * jnp.concatenate(gcum, axis=1)
            if masked:
                beta_g = jnp.where(mask, beta_g, 0.0)
            dz = ((gw - beta_g) * scale).astype(MXU_DTYPE)
            dk_ref[grows, :] += _tn(dz, qb)
            dq_ref[...] += _nn(dz, kg)
            gpre_s[...] = gpre

        _sb_segments(tile, group, descending=False)

    return pl.pallas_call(
        body, name=name, grid=(H, nq // r), scratch_shapes=[pltpu.VMEM((tq, LANE), F32)],
        in_specs=[pl.BlockSpec((tq, SB_DH), lambda h, i: (i, SB_Q0 + h)),
                  pl.BlockSpec((tp, SB_DH), lambda h, i: (0, SB_K0 + h)),
                  pl.BlockSpec((tp, SB_DH), lambda h, i: (0, SB_V0 + h)),
                  pl.BlockSpec((1, 1, tq, LANE), lambda h, i: (h, i, 0, 0)),
                  pl.BlockSpec((tq, SB_DH), lambda h, i: (i, do0 + h))],
        out_specs=[pl.BlockSpec((tq, SB_DH), lambda h, i: (i, h)),
                   pl.BlockSpec((tp, SB_DH), lambda h, i: (0, h)),
                   pl.BlockSpec((tp, SB_DH), lambda h, i: (0, h))],
        out_shape=[jax.ShapeDtypeStruct((tp, SBW), F32)] * 3,
        compiler_params=_params(("parallel", "arbitrary")),
    )(proj, proj, proj, across, dmix)


def _ffn_forward(h, p, tag):
    u = _rms_fwd(h, p["g2"], out_dtype=MXU_DTYPE, name=f"{tag}_ffn_norm")
    ab = _matmul(u, p["w_gu"], name=f"{tag}_ffn_gate_up")
    s = _swiglu_fwd(ab, name=f"{tag}_ffn_act")
    f = _matmul(s, p["w_down"], name=f"{tag}_ffn_down")
    out = _rms_fwd(f, p["g3"], res=h, name=f"{tag}_ffn_out")
    return out, dict(h=h, u=u, ab=ab, s=s, f=f)


def _ffn_backward(dh, p, a, tag):
    df, dg3 = _rms_bwd(a["f"], p["g3"], dh, out_dtype=MXU_DTYPE, name=f"{tag}_ffn_out_bwd")
    ds = _matmul(df, p["w_down"], tb=True, name=f"{tag}_ffn_down_dx")
    dw_down = _matmul(a["s"], df, ta=True, name=f"{tag}_ffn_down_dw")
    dab = _swiglu_bwd(a["ab"], ds, name=f"{tag}_ffn_act_bwd")
    du = _matmul(dab, p["w_gu"], tb=True, name=f"{tag}_ffn_gate_up_dx")
    dw_gu = _matmul(a["u"], dab, ta=True, name=f"{tag}_ffn_gate_up_dw")
    dh_in, dg2 = _rms_bwd(a["h"], p["g2"], du, add=dh, name=f"{tag}_ffn_norm_bwd")
    return dh_in, dict(g2=dg2, g3=dg3, w_gu=dw_gu, w_down=dw_down)


def _mixer_forward(h, p, tag):
    u = _rms_fwd(h, p["g0"], out_dtype=MXU_DTYPE, name=f"{tag}_mix_norm")
    proj = _matmul(u, p["w_in"], name=f"{tag}_mix_in")
    qc = _conv_fwd(proj, p["qk_w"], p["qk_b"], name=f"{tag}_mix_qkconv")
    qk = _silu_fwd(qc, name=f"{tag}_mix_qkact")
    gl = _gates_fwd(proj, p["gate_b"], name=f"{tag}_mix_gates")
    hm, call, nall, mall = _mlstm_fwd(qk, proj, gl, name=f"{tag}_mlstm")
    hn = _hnorm_fwd(hm, proj, p["hnorm_g"], name=f"{tag}_mix_hnorm")
    hs, across = _sb_fwd(proj, name=f"{tag}_sb")
    mixed = jnp.concatenate([hn, hs], axis=1).astype(MXU_DTYPE)
    y = _matmul(mixed, p["w_out"], name=f"{tag}_mix_out")
    out = _rms_fwd(y, p["g1"], res=h, name=f"{tag}_mix_res")
    return out, dict(h=h, u=u, proj=proj, qc=qc, qk=qk, gl=gl, hm=hm, call=call, nall=nall, mall=mall,
                     across=across, mixed=mixed, y=y)


def _mixer_backward(dh, p, a, tag):
    tp = dh.shape[0]
    dy, dg1 = _rms_bwd(a["y"], p["g1"], dh, out_dtype=MXU_DTYPE, name=f"{tag}_mix_res_bwd")
    dmixed = _matmul(dy, p["w_out"], tb=True, name=f"{tag}_mix_out_dx")
    dw_out = _matmul(a["mixed"], dy, ta=True, name=f"{tag}_mix_out_dw")
    dsq, dsk, dsv = _sb_bwd(a["proj"], a["across"], dmixed, name=f"{tag}_sb_bwd")
    dhm, do, dhg = _hnorm_bwd(a["hm"], a["proj"], p["hnorm_g"], dmixed, name=f"{tag}_mix_hnorm_bwd")
    dqk, dv, dgl = _mlstm_bwd(a["qk"], a["proj"], a["gl"], dhm, a["call"], a["nall"], a["mall"],
                              name=f"{tag}_mlstm_bwd")
    dpg, dgate_b = _gates_bwd(a["proj"], p["gate_b"], dgl, name=f"{tag}_mix_gates_bwd")
    dqc = _silu_bwd(a["qc"], dqk, name=f"{tag}_mix_qkact_bwd")
    dpqk, dqk_w, dqk_b = _conv_bwd(a["proj"], p["qk_w"], dqc, name=f"{tag}_mix_qkconv_bwd")
    dproj = jnp.concatenate(
        [dpqk, dv, do, dsq, dsk, dsv, dpg, jnp.zeros((tp, PROJ_WIDTH - GATE_COL - LANE), F32)], axis=1
    ).astype(MXU_DTYPE)
    du = _matmul(dproj, p["w_in"], tb=True, name=f"{tag}_mix_in_dx")
    dw_in = _matmul(a["u"], dproj, ta=True, name=f"{tag}_mix_in_dw")
    dh_in, dg0 = _rms_bwd(a["h"], p["g0"], du, add=dh, name=f"{tag}_mix_norm_bwd")
    return dh_in, dict(g0=dg0, g1=dg1, w_in=dw_in, qk_w=dqk_w, qk_b=dqk_b, gate_b=dgate_b, hnorm_g=dhg,
                       w_out=dw_out)


def _conformer_forward(h, p, tag):
    u = _rms_fwd(h, p["g0"], out_dtype=MXU_DTYPE, name=f"{tag}_conf_norm")
    z = _matmul(u, p["w_pw1"], name=f"{tag}_conf_pw1")
    y1 = _glu_fwd(z, p["b_pw1"], name=f"{tag}_conf_glu")
    y2 = _conv_fwd(y1, p["w_dw"], p["b_dw"], name=f"{tag}_conf_dw")
    y3 = _lnsilu_fwd(y2, p["ln_g"], p["ln_b"], name=f"{tag}_conf_ln")
    y4 = _matmul(y3, p["w_pw2"], name=f"{tag}_conf_pw2")
    out = _rms_fwd(y4, p["g1"], res=h, bias=p["b_pw2"], name=f"{tag}_conf_res")
    return out, dict(h=h, u=u, z=z, y1=y1, y2=y2, y3=y3, y4=y4)


def _conformer_backward(dh, p, a, tag):
    dy4, dg1, db_pw2 = _rms_bwd(a["y4"], p["g1"], dh, bias=p["b_pw2"], out_dtype=MXU_DTYPE,
                                name=f"{tag}_conf_res_bwd")
    dy3 = _matmul(dy4, p["w_pw2"], tb=True, name=f"{tag}_conf_pw2_dx")
    dw_pw2 = _matmul(a["y3"], dy4, ta=True, name=f"{tag}_conf_pw2_dw")
    dy2, dln_g, dln_b = _lnsilu_bwd(a["y2"], p["ln_g"], p["ln_b"], dy3, name=f"{tag}_conf_ln_bwd")
    dy1, dw_dw, db_dw = _conv_bwd(a["y1"], p["w_dw"], dy2, name=f"{tag}_conf_dw_bwd")
    dz, db_pw1 = _glu_bwd(a["z"], p["b_pw1"], dy1, name=f"{tag}_conf_glu_bwd")
    du = _matmul(dz, p["w_pw1"], tb=True, name=f"{tag}_conf_pw1_dx")
    dw_pw1 = _matmul(a["u"], dz, ta=True, name=f"{tag}_conf_pw1_dw")
    dh_in, dg0 = _rms_bwd(a["h"], p["g0"], du, add=dh, name=f"{tag}_conf_norm_bwd")
    return dh_in, dict(g0=dg0, g1=dg1, w_pw1=dw_pw1, b_pw1=db_pw1, w_dw=dw_dw, b_dw=db_dw, ln_g=dln_g,
                       ln_b=dln_b, w_pw2=dw_pw2, b_pw2=db_pw2)


def _trunk_step(h0, target, layers):
    acts = []
    h = h0
    for li, p in enumerate(layers):
        tag = f"l{li}"
        h, a_mix = (_mixer_forward if li % 2 == 0 else _conformer_forward)(h, p["mix"], tag)
        h, a_ffn = _ffn_forward(h, p["ffn"], tag)
        acts.append((a_mix, a_ffn))
    dh, loss_cols = _loss_fwd_bwd(h, target, name="loss")
    grads = [None] * len(layers)
    for li in reversed(range(len(layers))):
        tag = f"l{li}"
        p = layers[li]
        dh, g_ffn = _ffn_backward(dh, p["ffn"], acts[li][1], tag)
        dh, g_mix = (_mixer_backward if li % 2 == 0 else _conformer_backward)(dh, p["mix"], acts[li][0], tag)
        grads[li] = dict(mix=g_mix, ffn=g_ffn)
    return loss_cols, dh, grads


_SPLIT = 2 * MQK + 2 * MV


def _prepare_layers(w):
    layers = []
    row = lambda v: v[None, :].astype(F32)
    for li in range(DEPTH):
        i = li // 2
        g = w["norm_g"][li].astype(F32)
        if li % 2 == 0:
            win = w["mix_w_in"][i]
            w_in = jnp.concatenate(
                [win[:, :_SPLIT], win[:, _SPLIT + 2 * MLSTM_HEADS:], win[:, _SPLIT:_SPLIT + 2 * MLSTM_HEADS],
                 jnp.zeros((D_MODEL, PROJ_WIDTH - IN_WIDTH), win.dtype)], axis=1)
            gate_b = jnp.pad(row(w["mix_gate_b"][i]), ((0, 0), (0, LANE - 2 * MLSTM_HEADS)))
            mix = dict(g0=g[0:1], g1=g[1:2], w_in=w_in, qk_w=w["mix_qk_conv_w"][i].astype(F32),
                       qk_b=row(w["mix_qk_conv_b"][i]), gate_b=gate_b, hnorm_g=row(w["mix_hnorm_g"][i]),
                       w_out=w["mix_w_out"][i])
        else:
            mix = dict(g0=g[0:1], g1=g[1:2], w_pw1=w["conv_w_pw1"][i], b_pw1=row(w["conv_b_pw1"][i]),
                       w_dw=w["conv_w_dw"][i].astype(F32), b_dw=row(w["conv_b_dw"][i]),
                       ln_g=row(w["conv_ln_g"][i]), ln_b=row(w["conv_ln_b"][i]), w_pw2=w["conv_w_pw2"][i],
                       b_pw2=row(w["conv_b_pw2"][i]))
        ffn = dict(g2=g[2:3], g3=g[3:4],
                   w_gu=jnp.concatenate([w["ffn_w_gate"][li], w["ffn_w_up"][li]], axis=1),
                   w_down=w["ffn_w_down"][li])
        layers.append(dict(mix=mix, ffn=ffn))
    return layers


def _collect_grads(grads):
    even = [grads[li]["mix"] for li in range(0, DEPTH, 2)]
    odd = [grads[li]["mix"] for li in range(1, DEPTH, 2)]
    ffn = [grads[li]["ffn"] for li in range(DEPTH)]
    st = lambda xs: jnp.stack(xs, axis=0)
    vec = lambda xs, k: st([x[k][0] for x in xs])
    out = {}
    out["norm_g"] = st([jnp.concatenate([grads[li]["mix"]["g0"], grads[li]["mix"]["g1"], grads[li]["ffn"]["g2"],
                                         grads[li]["ffn"]["g3"]], axis=0) for li in range(DEPTH)])
    out["mix_w_in"] = st([jnp.concatenate(
        [g["w_in"][:, :_SPLIT], g["w_in"][:, GATE_COL:GATE_COL + 2 * MLSTM_HEADS], g["w_in"][:, _SPLIT:GATE_COL]],
        axis=1) for g in even])
    out["mix_qk_conv_w"] = st([g["qk_w"] for g in even])
    out["mix_qk_conv_b"] = vec(even, "qk_b")
    out["mix_gate_b"] = st([g["gate_b"][0, :2 * MLSTM_HEADS] for g in even])
    out["mix_hnorm_g"] = vec(even, "hnorm_g")
    out["mix_w_out"] = st([g["w_out"] for g in even])
    out["conv_w_pw1"] = st([g["w_pw1"] for g in odd])
    out["conv_b_pw1"] = vec(odd, "b_pw1")
    out["conv_w_dw"] = st([g["w_dw"] for g in odd])
    out["conv_b_dw"] = vec(odd, "b_dw")
    out["conv_ln_g"] = vec(odd, "ln_g")
    out["conv_ln_b"] = vec(odd, "ln_b")
    out["conv_w_pw2"] = st([g["w_pw2"] for g in odd])
    out["conv_b_pw2"] = vec(odd, "b_pw2")
    out["ffn_w_gate"] = st([g["w_gu"][:, :FFN_HIDDEN] for g in ffn])
    out["ffn_w_up"] = st([g["w_gu"][:, FFN_HIDDEN:] for g in ffn])
    out["ffn_w_down"] = st([g["w_down"] for g in ffn])
    return out


def _local_step(x, target, w):
    seq = x.shape[0]
    h0 = jnp.concatenate([jnp.zeros((PAD_FRONT, D_MODEL), F32), w["meta"].astype(F32), x], axis=0)
    tgt = jnp.concatenate([jnp.zeros((PAD_FRONT + N_META, D_MODEL), F32), target], axis=0)
    loss_cols, dh0, grads = _trunk_step(h0, tgt, _prepare_layers(w))
    out = _collect_grads(grads)
    out["meta"] = dh0[PAD_FRONT:PAD_FRONT + N_META]
    loss = 0.5 * jnp.sum(loss_cols) / D_MODEL
    return loss, dh0[PAD_FRONT + N_META:PAD_FRONT + N_META + seq], out


def _elementwise(fn, arrays, out_dtypes, *, name):
    shape = arrays[0].shape
    cols = shape[-1]
    rows = 1
    for s in shape[:-1]:
        rows *= s
    flat = [a.reshape(rows, cols) for a in arrays]
    if rows * cols * 4 <= (1 << 20) or rows % SUBLANE:
        tr = rows
    else:
        tr = _divisor(rows, (512, 256, 128, 64, 32, 16, 8))
    n = len(flat)

    def body(*refs):
        outs = fn(*[r[...] for r in refs[:n]])
        for o_ref, o in zip(refs[n:], outs):
            o_ref[...] = o.astype(o_ref.dtype)

    spec = pl.BlockSpec((tr, cols), lambda i: (i, 0))
    outs = pl.pallas_call(
        body, name=name, grid=(rows // tr,), in_specs=[spec] * n, out_specs=[spec] * len(out_dtypes),
        out_shape=[jax.ShapeDtypeStruct((rows, cols), dt) for dt in out_dtypes],
        compiler_params=_params(("parallel",)),
    )(*flat)
    return [o.reshape(shape) for o in outs]


def _adamw(w, g, m, v, *, name):
    def fn(wv, gv, mv, vv):
        mn = ADAM_B1 * mv + (1.0 - ADAM_B1) * gv
        vn = ADAM_B2 * vv + (1.0 - ADAM_B2) * (gv * gv)
        m_hat = mn / (1.0 - ADAM_B1 ** ADAM_STEP)
        v_hat = vn / (1.0 - ADAM_B2 ** ADAM_STEP)
        return [-ADAM_LR * (m_hat / (jnp.sqrt(v_hat) + ADAM_EPS) + ADAM_WD * wv), mn, vn]

    return _elementwise(fn, [w, g, m, v], [F32, F32, F32], name=name)


MESH_ID = pl.DeviceIdType.MESH
ANY = pl.BlockSpec(memory_space=pl.ANY)


def _place():
    x, y, c = lax.axis_index("x"), lax.axis_index("y"), lax.axis_index("c")
    return x, y, c, [(1 - x, y), (x, 1 - y), (1 - x, 1 - y)]


def _remote(src, dst, send_sems, recv_sems, k, to):
    return pltpu.make_async_remote_copy(src_ref=src, dst_ref=dst, send_sem=send_sems.at[k], recv_sem=recv_sems.at[k],
                                        device_id=to, device_id_type=MESH_ID)


def _comm_call(body, arrays, out_shapes, n_remote, n_local, name):
    return pl.pallas_call(
        body, name=name, in_specs=[ANY] * len(arrays), out_specs=[ANY] * len(out_shapes), out_shape=out_shapes,
        scratch_shapes=[pltpu.SemaphoreType.DMA((n_remote,)), pltpu.SemaphoreType.DMA((n_remote,)),
                        pltpu.SemaphoreType.DMA((n_local,))],
        compiler_params=pltpu.CompilerParams(has_side_effects=True),
    )(*arrays)


def _gather_chips(shards, *, name):
    n = len(shards)

    def body(*refs):
        ins, outs = refs[:n], refs[n:2 * n]
        send_sems, recv_sems, local_sems = refs[2 * n:]
        x, y, c, chips = _place()
        me, sibling = 2 * x + y, (x, y, 1 - c)

        def half(a, slot, hc):
            hl = ins[a].shape[0] // 2
            return outs[a].at[slot].at[pl.ds(hc * hl, hl)]

        def mine(a):
            hl = ins[a].shape[0] // 2
            return ins[a].at[pl.ds(c * hl, hl)]

        sent = []
        for a in range(n):
            for j, (px, py) in enumerate(chips):
                sent.append(_remote(mine(a), half(a, me, c), send_sems, recv_sems, 6 * a + j, (px, py, c)))
                sent[-1].start()
        for a in range(n):
            for j, (px, py) in enumerate(chips):
                slot = 2 * px + py
                _remote(mine(a), half(a, slot, c), send_sems, recv_sems, 6 * a + j, (px, py, c)).wait_recv()
                sent.append(_remote(half(a, slot, c), half(a, slot, c), send_sems, recv_sems, 6 * a + 3 + j, sibling))
                sent[-1].start()
        for a in range(n):
            for j, (px, py) in enumerate(chips):
                slot = 2 * px + py
                _remote(mine(a), half(a, slot, 1 - c), send_sems, recv_sems, 6 * a + 3 + j, sibling).wait_recv()
        for cp in sent:
            cp.wait_send()

    out_shapes = [jax.ShapeDtypeStruct((4,) + s.shape, s.dtype) for s in shards]
    return _comm_call(body, shards, out_shapes, 6 * n, 1, name)


def _swap_siblings(arrays, *, by_core, name):
    n = len(arrays)

    def body(*refs):
        ins, outs = refs[:n], refs[n:2 * n]
        send_sems, recv_sems, _ = refs[2 * n:]
        x, y, c, _chips = _place()
        cps = [_remote(ins[a].at[1 - c] if by_core else ins[a], outs[a], send_sems, recv_sems, a, (x, y, 1 - c))
               for a in range(n)]
        for cp in cps:
            cp.start()
        for cp in cps:
            cp.wait()

    out_shapes = [jax.ShapeDtypeStruct(a.shape[1:] if by_core else a.shape, a.dtype) for a in arrays]
    return _comm_call(body, arrays, out_shapes, n, 1, name)


def _scatter_chips(parts, small, *, name):
    n = len(parts)

    def body(*refs):
        ins, small_in = refs[:n], refs[n]
        outs, small_out = refs[n + 1:2 * n + 1], refs[2 * n + 1]
        send_sems, recv_sems, local_sems = refs[2 * n + 2:]
        x, y, c, chips = _place()
        me8 = 4 * x + 2 * y + c
        own = pltpu.make_async_copy(small_in, small_out.at[me8], local_sems.at[0])
        own.start()
        cps = []
        for fx in range(2):
            for fy in range(2):
                for fc in range(2):
                    r = 4 * fx + 2 * fy + fc - 1
                    if r >= 0:
                        to = (x + fx - 2 * x * fx, y + fy - 2 * y * fy, c + fc - 2 * c * fc)
                        cps.append(_remote(small_in, small_out.at[me8], send_sems, recv_sems, r, to))
        for a in range(n):
            for j, (px, py) in enumerate(chips):
                cps.append(_remote(ins[a].at[2 * px + py], outs[a].at[j], send_sems, recv_sems, 7 + 3 * a + j,
                                   (px, py, c)))
        for cp in cps:
            cp.start()
        for cp in cps:
            cp.wait()
        own.wait()

    out_shapes = [jax.ShapeDtypeStruct((3,) + p.shape[1:], p.dtype) for p in parts]
    out_shapes.append(jax.ShapeDtypeStruct((8,) + small.shape, small.dtype))
    return _comm_call(body, list(parts) + [small], out_shapes, 7 + 3 * n, 1, name)


WEIGHTS = ("meta", "norm_g", "mix_w_in", "mix_qk_conv_w", "mix_qk_conv_b", "mix_gate_b", "mix_hnorm_g", "mix_w_out",
           "conv_w_pw1", "conv_b_pw1", "conv_w_dw", "conv_b_dw", "conv_ln_g", "conv_ln_b", "conv_w_pw2",
           "conv_b_pw2", "ffn_w_gate", "ffn_w_up", "ffn_w_down")
SHARD_AXIS = dict(meta=1, norm_g=2, mix_w_in=2, mix_qk_conv_w=2, mix_qk_conv_b=None, mix_gate_b=None,
                  mix_hnorm_g=None, mix_w_out=1, conv_w_pw1=2, conv_b_pw1=1, conv_w_dw=2, conv_b_dw=1, conv_ln_g=1,
                  conv_ln_b=1, conv_w_pw2=1, conv_b_pw2=1, ffn_w_gate=2, ffn_w_up=2, ffn_w_down=1)
MATRICES = ("mix_w_in", "mix_w_out", "conv_w_pw1", "conv_w_pw2", "ffn_w_gate", "ffn_w_up", "ffn_w_down")
VECTORS = tuple(n for n in WEIGHTS if n not in MATRICES)
GATHER_COLS = D_MODEL // 4


def _pack_rows(arrays, cols, pad_to):
    rows = [a.astype(F32).reshape(-1) for a in arrays]
    rows = [jnp.pad(r, (0, (-r.shape[0]) % cols)).reshape(-1, cols) for r in rows]
    packed = jnp.concatenate(rows, axis=0)
    return jnp.pad(packed, ((0, pad_to - packed.shape[0]), (0, 0))), [r.shape[0] for r in rows]


def _unpack_rows(packed, counts, shapes):
    out, at = [], 0
    for n, shape in zip(counts, shapes):
        size = 1
        for s in shape:
            size *= s
        out.append(packed[..., at:at + n, :].reshape(packed.shape[:-2] + (-1,))[..., :size]
                   .reshape(packed.shape[:-2] + tuple(shape)))
        at += n
    return out


def _gather_weights(local):
    sharded_vecs = [n for n in VECTORS if SHARD_AXIS[n] is not None]
    pack, counts = _pack_rows([local[n] for n in sharded_vecs], GATHER_COLS, 120)
    shards = [local[n].astype(MXU_DTYPE) for n in MATRICES] + [pack.reshape(2, 60, GATHER_COLS)]
    me = 2 * lax.axis_index("x") + lax.axis_index("y")
    got = [lax.dynamic_update_index_in_dim(g, s, me, 0) for g, s in zip(_gather_chips(shards, name="gather_weights"),
                                                                        shards)]
    full = {n: local[n] for n in VECTORS if SHARD_AXIS[n] is None}
    for n, g in zip(MATRICES, got):
        full[n] = jnp.concatenate([g[k] for k in range(4)], axis=SHARD_AXIS[n])
    vecs = _unpack_rows(got[-1].reshape(4, 120, GATHER_COLS), counts, [local[n].shape for n in sharded_vecs])
    for n, v in zip(sharded_vecs, vecs):
        full[n] = jnp.moveaxis(v, 0, -2).reshape(v.shape[1:-1] + (4 * v.shape[-1],))
    return full


def _reduce_grads(grads):
    x, y, c = lax.axis_index("x"), lax.axis_index("y"), lax.axis_index("c")
    me = 2 * x + y
    stacked = []
    for n in MATRICES:
        g = jnp.stack(jnp.split(grads[n], 4, axis=SHARD_AXIS[n]), axis=0)
        g = g.reshape((4, 2, g.shape[1] // 2) + g.shape[2:])
        stacked.append(jnp.swapaxes(g, 0, 1))
    theirs = _swap_siblings(stacked, by_core=True, name="reduce_pair_swap")
    pair = [_elementwise(lambda a, b: [a + b], [lax.dynamic_index_in_dim(s, c, 0, keepdims=False), t], [F32],
                         name=f"reduce_pair_sum_{n}")[0] for n, s, t in zip(MATRICES, stacked, theirs)]
    shapes = [grads[n].shape for n in VECTORS]
    pack, counts = _pack_rows([grads[n] for n in VECTORS], D_MODEL, 120)
    got = _scatter_chips([p.astype(jnp.bfloat16) for p in pair], pack, name="reduce_chips")
    halves = []
    for n, p, r in zip(MATRICES, pair, got[:-1]):
        own = lax.dynamic_index_in_dim(p, me, 0, keepdims=False)
        halves.append(_elementwise(lambda a, b0, b1, b2: [((a + b0.astype(F32)) + b1.astype(F32)) + b2.astype(F32)],
                                   [own, r[0], r[1], r[2]], [F32], name=f"reduce_chip_sum_{n}")[0])
    others = _swap_siblings(halves, by_core=False, name="reduce_join")
    out = {n: jnp.where(c == 0, jnp.concatenate([h, o], axis=0), jnp.concatenate([o, h], axis=0))
           for n, h, o in zip(MATRICES, halves, others)}
    small = got[-1]
    total = _elementwise(lambda *s: [functools.reduce(lambda a, b: a + b, s)], [small[k] for k in range(8)], [F32],
                         name="reduce_small_sum")[0]
    for n, v in zip(VECTORS, _unpack_rows(total, counts, shapes)):
        ax = SHARD_AXIS[n]
        if ax is not None:
            w = v.shape[ax] // 4
            v = lax.dynamic_slice_in_dim(v, me * w, w, axis=ax)
        out[n] = v
    return out


def kernel(x, meta, norm_g, mix_w_in, mix_qk_conv_w, mix_qk_conv_b, mix_gate_b, mix_hnorm_g, mix_w_out, conv_w_pw1, conv_b_pw1, conv_w_dw, conv_b_dw, conv_ln_g, conv_ln_b, conv_w_pw2, conv_b_pw2, ffn_w_gate, ffn_w_up, ffn_w_down, loss_target, m_meta, m_norm_g, m_mix_w_in, m_mix_qk_conv_w, m_mix_qk_conv_b, m_mix_gate_b, m_mix_hnorm_g, m_mix_w_out, m_conv_w_pw1, m_conv_b_pw1, m_conv_w_dw, m_conv_b_dw, m_conv_ln_g, m_conv_ln_b, m_conv_w_pw2, m_conv_b_pw2, m_ffn_w_gate, m_ffn_w_up, m_ffn_w_down, v_meta, v_norm_g, v_mix_w_in, v_mix_qk_conv_w, v_mix_qk_conv_b, v_mix_gate_b, v_mix_hnorm_g, v_mix_w_out, v_conv_w_pw1, v_conv_b_pw1, v_conv_w_dw, v_conv_b_dw, v_conv_ln_g, v_conv_ln_b, v_conv_w_pw2, v_conv_b_pw2, v_ffn_w_gate, v_ffn_w_up, v_ffn_w_down):
    given = dict(locals())
    local = {n: given[n] for n in WEIGHTS}
    full = _gather_weights(local)
    loss, grad_x, grads = _local_step(x[0], loss_target[0], full)
    loss = lax.psum(loss, ("x", "y", "c"))
    grad_w = _reduce_grads(grads)
    delta, new_m, new_v = {}, {}, {}
    for n in WEIGHTS:
        delta[n], new_m[n], new_v[n] = _adamw(local[n], grad_w[n], given["m_" + n], given["v_" + n], name=f"adamw_{n}")
    return (loss, grad_x[None], *[grad_w[n] for n in WEIGHTS], *[delta[n] for n in WEIGHTS],
            *[new_m[n] for n in WEIGHTS], *[new_v[n] for n in WEIGHTS])
```

```python
import functools

import jax
import jax.numpy as jnp
from jax import lax
from jax.experimental import pallas as pl
from jax.experimental.pallas import tpu as pltpu

F32 = jnp.float32
MXU_DTYPE = jnp.bfloat16

D_MODEL = 1024
N_META = 16
DEPTH = 4
MLSTM_HEADS = 4
MLSTM_DQK = 128
MLSTM_DV = 256
MLSTM_CHUNK = 64
QK_CONV_WIDTH = 4
GATE_SOFTCAP = 15.0
SB_HEADS = 4
SB_DH = 128
SB_BLOCK = 128
PAD_FRONT = SB_BLOCK - N_META
CONV_WIDTH = 31
FFN_HIDDEN = 2816
MQK = MLSTM_HEADS * MLSTM_DQK
MV = MLSTM_HEADS * MLSTM_DV
SBW = SB_HEADS * SB_DH
IN_WIDTH = 2 * MQK + 2 * MV + 2 * MLSTM_HEADS + 3 * SBW
MIX_WIDTH = MV + SBW
NEG = -1e30
EPS = 1e-6
PROJ_WIDTH = 5120
GATE_COL = 3 * MV + 3 * SBW
LANE = 128
SUBLANE = 8
CONV_HALO = 32
VMEM_LIMIT = 56 * 1024 * 1024

ADAM_LR = 0.001
ADAM_B1 = 0.9
ADAM_B2 = 0.999
ADAM_EPS = 1e-08
ADAM_WD = 0.01
ADAM_STEP = 10


def _divisor(n, cands):
    for c in cands:
        if n % c == 0:
            return c
    raise ValueError(f"no tile for {n} in {cands}")


ROW_TILE_BYTES = 20 * 1024 * 1024


def _row_tile(tp, width=D_MODEL):
    for c in (640, 512, 384, 320, 256, 128, 64):
        if tp % c == 0 and c * width * 8 <= ROW_TILE_BYTES:
            return c
    raise ValueError(f"no row tile for {tp} x {width}")


def _params(sem):
    return pltpu.CompilerParams(dimension_semantics=sem, vmem_limit_bytes=VMEM_LIMIT)


def _dot(a, b, dims):
    return lax.dot_general(a.astype(MXU_DTYPE), b.astype(MXU_DTYPE), (dims, ((), ())),
                           preferred_element_type=F32)


def _nn(a, b):
    return _dot(a, b, ((1,), (0,)))


def _nt(a, b):
    return _dot(a, b, ((1,), (1,)))


def _tn(a, b):
    return _dot(a, b, ((0,), (0,)))


def _sigmoid(x):
    return 1.0 / (1.0 + jnp.exp(-x))


def _softplus(x):
    return jnp.maximum(x, 0.0) + jnp.log(1.0 + jnp.exp(-jnp.abs(x)))


MATMUL_VMEM_BYTES = 40 * 1024 * 1024


def _matmul_tiles(m, n, k, a_bytes, b_bytes):
    tm = _divisor(m, (1040, 1024, 1408, 768, 640, 512, 384, 256, 128))
    tn = _divisor(n, (1408, 1280, 1024, 768, 512, 256, 128))
    for tk in (5632, 5120, 2816, 2560, 2048, 1664, 1536, 1408, 1280, 1040, 1024, 768, 640, 512, 384, 256, 128):
        if k % tk:
            continue
        need = 2 * (tm * tk * a_bytes + tk * tn * b_bytes + tm * tn * 4) + (tm * tn * 4 if tk < k else 0)
        if need <= MATMUL_VMEM_BYTES:
            return tm, tn, tk
    raise ValueError(f"no matmul tiles for {m}x{n}x{k}")


def _matmul(a, b, *, ta=False, tb=False, name):
    m, k = (a.shape[1], a.shape[0]) if ta else a.shape
    n = b.shape[0] if tb else b.shape[1]
    assert (b.shape[1] if tb else b.shape[0]) == k, (a.shape, b.shape, ta, tb)
    tm, tn, tk = _matmul_tiles(m, n, k, a.dtype.itemsize, b.dtype.itemsize)
    nk = k // tk
    dims = ((0 if ta else 1,), (1 if tb else 0,))

    def body(a_ref, b_ref, o_ref, *acc):
        if nk == 1:
            o_ref[...] = _dot(a_ref[...], b_ref[...], dims)
            return
        acc_ref, kk = acc[0], pl.program_id(2)

        @pl.when(kk == 0)
        def _():
            acc_ref[...] = jnp.zeros_like(acc_ref)

        acc_ref[...] += _dot(a_ref[...], b_ref[...], dims)

        @pl.when(kk == nk - 1)
        def _():
            o_ref[...] = acc_ref[...]

    a_spec = (pl.BlockSpec((tk, tm), lambda i, j, kk: (kk, i)) if ta
              else pl.BlockSpec((tm, tk), lambda i, j, kk: (i, kk)))
    b_spec = (pl.BlockSpec((tn, tk), lambda i, j, kk: (j, kk)) if tb
              else pl.BlockSpec((tk, tn), lambda i, j, kk: (kk, j)))
    return pl.pallas_call(
        body, name=name, grid=(m // tm, n // tn, nk),
        in_specs=[a_spec, b_spec],
        out_specs=pl.BlockSpec((tm, tn), lambda i, j, kk: (i, j)),
        out_shape=jax.ShapeDtypeStruct((m, n), F32),
        scratch_shapes=[pltpu.VMEM((tm, tn), F32)] if nk > 1 else [],
        compiler_params=_params(("parallel", "parallel", "arbitrary")),
    )(a, b)


def _rowwise(fn, rows, fulls, out_rows, out_accs, *, name, out_dtypes=None):
    tp = rows[0][0].shape[0]
    tm = _row_tile(tp, sum(w for _, _, w in rows) + sum(out_rows))
    nr, nf, no, na = len(rows), len(fulls), len(out_rows), len(out_accs)
    out_dtypes = out_dtypes or [F32] * no

    def body(*refs):
        i = pl.program_id(0)
        outs = fn(i * tm, *[r[...].astype(F32) for r in refs[:nr + nf]])
        for k in range(no):
            refs[nr + nf + k][...] = outs[k].astype(out_dtypes[k])
        for k in range(na):
            ref = refs[nr + nf + no + k]

            @pl.when(i == 0)
            def _(ref=ref):
                ref[...] = jnp.zeros_like(ref)

            ref[...] += outs[no + k]

    in_specs = [pl.BlockSpec((tm, w), functools.partial(lambda i, cb: (i, cb), cb=cb)) for _, cb, w in rows]
    in_specs += [pl.BlockSpec(f.shape, lambda i: (0, 0)) for f in fulls]
    out_specs = [pl.BlockSpec((tm, w), lambda i: (i, 0)) for w in out_rows]
    out_specs += [pl.BlockSpec(s, lambda i: (0, 0)) for s in out_accs]
    out_shape = [jax.ShapeDtypeStruct((tp, w), dt) for w, dt in zip(out_rows, out_dtypes)]
    out_shape += [jax.ShapeDtypeStruct(s, F32) for s in out_accs]
    return pl.pallas_call(
        body, name=name, grid=(tp // tm,), in_specs=in_specs, out_specs=out_specs, out_shape=out_shape,
        compiler_params=_params(("arbitrary",)),
    )(*[r[0] for r in rows], *fulls)


def _whole(a):
    return (a, 0, a.shape[1])


def _live(row0, tm):
    return (row0 + lax.broadcasted_iota(jnp.int32, (tm, 1), 0)) >= PAD_FRONT


def _rms_core(x, g):
    r = lax.rsqrt(jnp.mean(x * x, axis=-1, keepdims=True) + EPS)
    return x * r, r


def _rms_fwd(x, g, *, name, res=None, bias=None, out_dtype=F32):
    def fn(row0, *blk):
        it = iter(blk)
        xv = next(it)
        rv = next(it) if res is not None else None
        gv = next(it)
        if bias is not None:
            xv = xv + next(it)
        xh, _ = _rms_core(xv, gv)
        y = jnp.where(_live(row0, xv.shape[0]), xh * gv, 0.0)
        return [y + rv if rv is not None else y]

    rows = [_whole(x)] + ([_whole(res)] if res is not None else [])
    fulls = [g] + ([bias] if bias is not None else [])
    return _rowwise(fn, rows, fulls, [x.shape[1]], [], name=name, out_dtypes=[out_dtype])[0]


def _rms_bwd(x, g, dy, *, name, add=None, bias=None, out_dtype=F32):
    def fn(row0, *blk):
        it = iter(blk)
        xv, dyv = next(it), next(it)
        av = next(it) if add is not None else None
        gv = next(it)
        if bias is not None:
            xv = xv + next(it)
        dyv = jnp.where(_live(row0, xv.shape[0]), dyv, 0.0)
        xh, r = _rms_core(xv, gv)
        dyg = dyv * gv
        dx = r * (dyg - xh * jnp.mean(dyg * xh, axis=-1, keepdims=True))
        outs = [dx + av if av is not None else dx, jnp.sum(dyv * xh, axis=0, keepdims=True)]
        if bias is not None:
            outs.append(jnp.sum(dx, axis=0, keepdims=True))
        return outs

    rows = [_whole(x), _whole(dy)] + ([_whole(add)] if add is not None else [])
    fulls = [g] + ([bias] if bias is not None else [])
    c = x.shape[1]
    return _rowwise(fn, rows, fulls, [c], [(1, c)] * (2 if bias is not None else 1), name=name,
                    out_dtypes=[out_dtype])


def _swiglu_fwd(ab, *, name):
    h = ab.shape[1] // 2

    def fn(row0, a, b):
        return [a * _sigmoid(a) * b]

    return _rowwise(fn, [(ab, 0, h), (ab, 1, h)], [], [h], [], name=name, out_dtypes=[MXU_DTYPE])[0]


def _swiglu_bwd(ab, ds, *, name):
    h = ab.shape[1] // 2

    def fn(row0, a, b, d):
        sg = _sigmoid(a)
        return [jnp.concatenate([d * b * sg * (1.0 + a * (1.0 - sg)), d * a * sg], axis=1)]

    return _rowwise(fn, [(ab, 0, h), (ab, 1, h), _whole(ds)], [], [2 * h], [], name=name,
                    out_dtypes=[MXU_DTYPE])[0]


def _glu_fwd(z, b, *, name):
    h = z.shape[1] // 2

    def fn(row0, a, gt, bv):
        y = (a + bv[:, :h]) * _sigmoid(gt + bv[:, h:])
        return [jnp.where(_live(row0, a.shape[0]), y, 0.0)]

    return _rowwise(fn, [(z, 0, h), (z, 1, h)], [b], [h], [], name=name)[0]


def _glu_bwd(z, b, dy, *, name):
    h = z.shape[1] // 2

    def fn(row0, a, gt, d, bv):
        d = jnp.where(_live(row0, a.shape[0]), d, 0.0)
        sg = _sigmoid(gt + bv[:, h:])
        dz = jnp.concatenate([d * sg, d * (a + bv[:, :h]) * sg * (1.0 - sg)], axis=1)
        return [dz, jnp.sum(dz, axis=0, keepdims=True)]

    return _rowwise(fn, [(z, 0, h), (z, 1, h), _whole(dy)], [b], [2 * h], [(1, 2 * h)], name=name,
                    out_dtypes=[MXU_DTYPE])


def _ln_core(x):
    mu = jnp.mean(x, axis=-1, keepdims=True)
    xc = x - mu
    r = lax.rsqrt(jnp.mean(xc * xc, axis=-1, keepdims=True) + EPS)
    return xc * r, r


def _lnsilu_fwd(x, g, b, *, name):
    def fn(row0, xv, gv, bv):
        xh, _ = _ln_core(xv)
        v = xh * gv + bv
        return [v * _sigmoid(v)]

    return _rowwise(fn, [_whole(x)], [g, b], [x.shape[1]], [], name=name, out_dtypes=[MXU_DTYPE])[0]


def _lnsilu_bwd(x, g, b, dy, *, name):
    def fn(row0, xv, d, gv, bv):
        xh, r = _ln_core(xv)
        v = xh * gv + bv
        sg = _sigmoid(v)
        dv = d * sg * (1.0 + v * (1.0 - sg))
        dxh = dv * gv
        dx = r * (dxh - jnp.mean(dxh, axis=-1, keepdims=True) - xh * jnp.mean(dxh * xh, axis=-1, keepdims=True))
        return [dx, jnp.sum(dv * xh, axis=0, keepdims=True), jnp.sum(dv, axis=0, keepdims=True)]

    c = x.shape[1]
    return _rowwise(fn, [_whole(x), _whole(dy)], [g, b], [c], [(1, c), (1, c)], name=name)


def _silu_fwd(x, *, name):
    return _rowwise(lambda row0, v: [v * _sigmoid(v)], [_whole(x)], [], [x.shape[1]], [], name=name)[0]


def _silu_bwd(x, dy, *, name):
    def fn(row0, v, d):
        sg = _sigmoid(v)
        return [d * sg * (1.0 + v * (1.0 - sg))]

    return _rowwise(fn, [_whole(x), _whole(dy)], [], [x.shape[1]], [], name=name)[0]


def _gate_parts(row0, pg, gb):
    lane = lax.broadcasted_iota(jnp.int32, pg.shape, 1)
    th = jnp.tanh((pg + gb) / GATE_SOFTCAP)
    s = GATE_SOFTCAP * th
    return lane, th, s, _live(row0, pg.shape[0])


def _gates_fwd(proj, gate_b, *, name):
    def fn(row0, pg, gb):
        lane, th, s, live = _gate_parts(row0, pg, gb)
        li = jnp.where(live, s, NEG)
        lf = jnp.where(live, -_softplus(-s), 0.0)
        return [jnp.where(lane < MLSTM_HEADS, li, jnp.where(lane < 2 * MLSTM_HEADS, lf, 0.0))]

    return _rowwise(fn, [(proj, GATE_COL // LANE, LANE)], [gate_b], [LANE], [], name=name)[0]


def _gates_bwd(proj, gate_b, dgl, *, name):
    def fn(row0, pg, d, gb):
        lane, th, s, live = _gate_parts(row0, pg, gb)
        ds = jnp.where(lane < MLSTM_HEADS, d, d * _sigmoid(-s))
        ds = jnp.where(live & (lane < 2 * MLSTM_HEADS), ds, 0.0)
        dp = ds * (1.0 - th * th)
        return [dp, jnp.sum(dp, axis=0, keepdims=True)]

    return _rowwise(fn, [(proj, GATE_COL // LANE, LANE), _whole(dgl)], [gate_b], [LANE], [(1, LANE)], name=name)


def _head_rms(h):
    parts = [h[:, i * MLSTM_DV:(i + 1) * MLSTM_DV] for i in range(MLSTM_HEADS)]
    rs = [lax.rsqrt(jnp.mean(p * p, axis=-1, keepdims=True) + EPS) for p in parts]
    return parts, rs


def _hnorm_fwd(hm, proj, g, *, name):
    def fn(row0, h, o, gv):
        parts, rs = _head_rms(h)
        xh = jnp.concatenate([p * r for p, r in zip(parts, rs)], axis=1)
        return [xh * gv * _sigmoid(o)]

    return _rowwise(fn, [_whole(hm), (proj, 2, MV)], [g], [MV], [], name=name)[0]


def _hnorm_bwd(hm, proj, g, dmixed, *, name):
    def fn(row0, h, o, d, gv):
        parts, rs = _head_rms(h)
        so = _sigmoid(o)
        dn = d * so
        dxs, xhs = [], []
        for i, (p, r) in enumerate(zip(parts, rs)):
            sl = slice(i * MLSTM_DV, (i + 1) * MLSTM_DV)
            xh = p * r
            dyg = dn[:, sl] * gv[:, sl]
            dxs.append(r * (dyg - xh * jnp.mean(dyg * xh, axis=-1, keepdims=True)))
            xhs.append(xh)
        xh = jnp.concatenate(xhs, axis=1)
        return [jnp.concatenate(dxs, axis=1), d * xh * gv * so * (1.0 - so), jnp.sum(dn * xh, axis=0, keepdims=True)]

    return _rowwise(fn, [_whole(hm), (proj, 2, MV), (dmixed, 0, MV)], [g], [MV, MV], [(1, MV)], name=name)


def _loss_fwd_bwd(h, target, *, name):
    first = PAD_FRONT + N_META

    def fn(row0, hv, tv):
        rows = row0 + lax.broadcasted_iota(jnp.int32, (hv.shape[0], 1), 0)
        e = jnp.where(rows >= first, hv - tv, 0.0)
        return [e * (1.0 / D_MODEL), jnp.sum(e * e, axis=0, keepdims=True)]

    return _rowwise(fn, [_whole(h), _whole(target)], [], [D_MODEL], [(1, D_MODEL)], name=name)


CONV_SUB = 64


def _conv_tiles(tp, c):
    return _row_tile(tp), _divisor(c, (256, 128))


def _conv_shift(win, shifted, tm):
    n = tm + CONV_HALO - SUBLANE
    for b in range(1, SUBLANE):
        shifted[b - 1, :, :] = win[b:b + n, :]


def _conv_window(win, shifted, offset, r0, rows):
    a, b = divmod(offset, SUBLANE)
    lo = a * SUBLANE + r0
    return win[lo:lo + rows, :] if b == 0 else shifted[b - 1, lo:lo + rows, :]


def _conv_fwd(x, w, b, *, name):
    tp, (k, c) = x.shape[0], w.shape
    tm, tc = _conv_tiles(tp, c)
    base = CONV_HALO - (k - 1)

    def body(x_ref, xp_ref, w_ref, b_ref, o_ref, win, shifted):
        i = pl.program_id(1)
        win[0:CONV_HALO, :] = jnp.where(i > 0, _mxu_rounded(xp_ref[tm - CONV_HALO:tm, :]), 0.0)
        win[CONV_HALO:CONV_HALO + tm, :] = _mxu_rounded(x_ref[...])
        _conv_shift(win, shifted, tm)
        for r0 in range(0, tm, CONV_SUB):
            acc = jnp.broadcast_to(b_ref[...], (CONV_SUB, tc))
            for j in range(k):
                acc = acc + _mxu_rounded(w_ref[j:j + 1, :]) *_conv_window(win, shifted, base + j, r0, CONV_SUB)
            o_ref[r0:r0 + CONV_SUB, :] = acc

    return pl.pallas_call(
        body, name=name, grid=(c // tc, tp // tm),
        in_specs=[pl.BlockSpec((tm, tc), lambda cc, i: (i, cc)),
                  pl.BlockSpec((tm, tc), lambda cc, i: (jnp.maximum(i - 1, 0), cc)),
                  pl.BlockSpec((k, tc), lambda cc, i: (0, cc)),
                  pl.BlockSpec((1, tc), lambda cc, i: (0, cc))],
        out_specs=pl.BlockSpec((tm, tc), lambda cc, i: (i, cc)),
        out_shape=jax.ShapeDtypeStruct((tp, c), F32),
        scratch_shapes=[pltpu.VMEM((CONV_HALO + tm, tc), F32),
                        pltpu.VMEM((SUBLANE - 1, CONV_HALO + tm - SUBLANE, tc), F32)],
        compiler_params=_params(("parallel", "arbitrary")),
    )(x, x, w, b)


def _conv_bwd(x, w, dy, *, name):
    tp, (k, c) = x.shape[0], w.shape
    tm, tc = _conv_tiles(tp, c)
    nt = tp // tm
    base = CONV_HALO - (k - 1)

    def body(x_ref, xp_ref, d_ref, dn_ref, w_ref, dx_ref, dw_ref, db_ref, winx, wind, shx, shd):
        i = pl.program_id(1)
        winx[0:CONV_HALO, :] = jnp.where(i > 0, _mxu_rounded(xp_ref[tm - CONV_HALO:tm, :]), 0.0)
        winx[CONV_HALO:CONV_HALO + tm, :] = _mxu_rounded(x_ref[...])
        d = d_ref[...]
        wind[0:tm, :] = _mxu_rounded(d)
        wind[tm:tm + CONV_HALO, :] = jnp.where(i < nt - 1, _mxu_rounded(dn_ref[0:CONV_HALO, :]), 0.0)
        _conv_shift(winx, shx, tm)
        _conv_shift(wind, shd, tm)

        @pl.when(i == 0)
        def _():
            dw_ref[...] = jnp.zeros_like(dw_ref)
            db_ref[...] = jnp.zeros_like(db_ref)

        for r0 in range(0, tm, CONV_SUB):
            acc = jnp.zeros((CONV_SUB, tc), F32)
            for j in range(k):
                acc = acc + _mxu_rounded(w_ref[j:j + 1, :]) *_conv_window(wind, shd, k - 1 - j, r0, CONV_SUB)
            dx_ref[r0:r0 + CONV_SUB, :] = acc
        for j in range(k):
            part = jnp.zeros((SUBLANE, tc), F32)
            for r0 in range(0, tm, CONV_SUB):
                p = wind[r0:r0 + CONV_SUB, :] * _conv_window(winx, shx, base + j, r0, CONV_SUB)
                part = part + jnp.sum(p.reshape(CONV_SUB // SUBLANE, SUBLANE, tc), axis=0)
            dw_ref[j:j + 1, :] += jnp.sum(part, axis=0, keepdims=True)
        db_ref[...] += jnp.sum(d, axis=0, keepdims=True)

    return pl.pallas_call(
        body, name=name, grid=(c // tc, nt),
        in_specs=[pl.BlockSpec((tm, tc), lambda cc, i: (i, cc)),
                  pl.BlockSpec((tm, tc), lambda cc, i: (jnp.maximum(i - 1, 0), cc)),
                  pl.BlockSpec((tm, tc), lambda cc, i: (i, cc)),
                  pl.BlockSpec((tm, tc), lambda cc, i: (jnp.minimum(i + 1, nt - 1), cc)),
                  pl.BlockSpec((k, tc), lambda cc, i: (0, cc))],
        out_specs=[pl.BlockSpec((tm, tc), lambda cc, i: (i, cc)),
                   pl.BlockSpec((k, tc), lambda cc, i: (0, cc)),
                   pl.BlockSpec((1, tc), lambda cc, i: (0, cc))],
        out_shape=[jax.ShapeDtypeStruct((tp, c), F32), jax.ShapeDtypeStruct((k, c), F32),
                   jax.ShapeDtypeStruct((1, c), F32)],
        scratch_shapes=[pltpu.VMEM((CONV_HALO + tm, tc), F32), pltpu.VMEM((CONV_HALO + tm, tc), F32),
                        pltpu.VMEM((SUBLANE - 1, CONV_HALO + tm - SUBLANE, tc), F32),
                        pltpu.VMEM((SUBLANE - 1, CONV_HALO + tm - SUBLANE, tc), F32)],
        compiler_params=_params(("parallel", "arbitrary")),
    )(x, x, dy, dy, w)


def _chunk_masks():
    L = MLSTM_CHUNK
    r = lax.broadcasted_iota(jnp.int32, (L, L), 0)
    c = lax.broadcasted_iota(jnp.int32, (L, L), 1)
    return r == c, c <= r, r <= c


def _to_row(col, eye):
    return jnp.sum(jnp.where(eye, col, 0.0), axis=0, keepdims=True)


def _to_col(row, eye):
    return jnp.sum(jnp.where(eye, row, 0.0), axis=1, keepdims=True)


def _mxu_rounded(x):
    return x.astype(MXU_DTYPE).astype(F32)


def _mlstm_group(nc):
    return _divisor(nc, (5, 4, 3, 2, 1))


def _mlstm_chunk(q, k, v, li_c, lf_c, c_st, n_st, m_st, masks):
    eye, low, up = masks
    li_r, lf_r = _to_row(li_c, eye), _to_row(lf_c, eye)
    b_c = jnp.sum(jnp.where(low, lf_r, 0.0), axis=1, keepdims=True)
    b_r = jnp.sum(jnp.where(up, lf_c, 0.0), axis=0, keepdims=True)
    g = jnp.sum(lf_c, axis=0, keepdims=True)
    dm = jnp.where(low, b_c - b_r + li_r, NEG)
    inter = b_c + m_st
    mt = jnp.maximum(inter, jnp.max(dm, axis=1, keepdims=True))
    wi = jnp.exp(dm - mt)
    wint = jnp.exp(inter - mt)
    s = _nt(q, k) * wi
    qc = _nn(q, c_st)
    qn = jnp.sum(_mxu_rounded(q) * _mxu_rounded(n_st), axis=1, keepdims=True)
    num = _nn(s, v) + wint * qc
    den = jnp.sum(s, axis=1, keepdims=True) + wint * qn
    floor = jnp.exp(-mt)
    a_c = g - b_c + li_c
    a_r = g - b_r + li_r
    mnew = jnp.maximum(g + m_st, jnp.max(a_r, axis=1, keepdims=True))
    wa_c = jnp.exp(a_c - mnew)
    wc = jnp.exp(g + m_st - mnew)
    return dict(wi=wi, wint=wint, s=s, qc=qc, qn=qn, num=num, den=den, floor=floor, mnew=mnew, wa_c=wa_c, wc=wc)


def _mlstm_fwd(qk, proj, gl, *, name):
    tp = qk.shape[0]
    L, H, dk, dv = MLSTM_CHUNK, MLSTM_HEADS, MLSTM_DQK, MLSTM_DV
    nc = tp // L
    G = _mlstm_group(nc)

    def body(qk_ref, v_ref, gl_ref, h_ref, call_ref, nall_ref, mall_ref, c_s, n_s, m_s):
        @pl.when(pl.program_id(0) == 0)
        def _():
            c_s[...] = jnp.zeros_like(c_s)
            n_s[...] = jnp.zeros_like(n_s)
            m_s[...] = jnp.zeros_like(m_s)

        masks = _chunk_masks()
        for h in range(H):
            c_st, n_st, m_row = c_s[h], n_s[h], m_s[h]
            for ci in range(G):
                rows = slice(ci * L, (ci + 1) * L)
                call_ref[ci, h] = c_st
                nall_ref[ci, h] = n_st
                mall_ref[ci, h] = m_row
                q = qk_ref[rows, h * dk:(h + 1) * dk] * (dk ** -0.5)
                k = qk_ref[rows, MQK + h * dk:MQK + (h + 1) * dk]
                v = v_ref[rows, h * dv:(h + 1) * dv]
                gates = gl_ref[rows, :]
                f = _mlstm_chunk(q, k, v, gates[:, h:h + 1], gates[:, H + h:H + h + 1], c_st, n_st,
                                 m_row[:, 0:1], masks)
                h_ref[rows, h * dv:(h + 1) * dv] = f["num"] / jnp.maximum(jnp.abs(f["den"]), f["floor"])
                kw = k * f["wa_c"]
                c_st = f["wc"] * c_st + _tn(kw, v)
                n_st = f["wc"] * n_st + jnp.sum(_mxu_rounded(k) * _mxu_rounded(f["wa_c"]), axis=0, keepdims=True)
                m_row = jnp.broadcast_to(f["mnew"], (1, LANE))
            c_s[h], n_s[h], m_s[h] = c_st, n_st, m_row

    return pl.pallas_call(
        body, name=name, grid=(nc // G,),
        in_specs=[pl.BlockSpec((G * L, 2 * MQK), lambda i: (i, 0)),
                  pl.BlockSpec((G * L, MV), lambda i: (i, 1)),
                  pl.BlockSpec((G * L, LANE), lambda i: (i, 0))],
        out_specs=[pl.BlockSpec((G * L, MV), lambda i: (i, 0)),
                   pl.BlockSpec((G, H, dk, dv), lambda i: (i, 0, 0, 0)),
                   pl.BlockSpec((G, H, 1, dk), lambda i: (i, 0, 0, 0)),
                   pl.BlockSpec((G, H, 1, LANE), lambda i: (i, 0, 0, 0))],
        out_shape=[jax.ShapeDtypeStruct((tp, MV), F32),
                   jax.ShapeDtypeStruct((nc, H, dk, dv), F32),
                   jax.ShapeDtypeStruct((nc, H, 1, dk), F32),
                   jax.ShapeDtypeStruct((nc, H, 1, LANE), F32)],
        scratch_shapes=[pltpu.VMEM((H, dk, dv), F32), pltpu.VMEM((H, 1, dk), F32), pltpu.VMEM((H, 1, LANE), F32)],
        compiler_params=_params(("arbitrary",)),
    )(qk, proj, gl)


def _mlstm_bwd(qk, proj, gl, dmix, call, nall, mall, *, name):
    tp = qk.shape[0]
    L, H, dk, dv = MLSTM_CHUNK, MLSTM_HEADS, MLSTM_DQK, MLSTM_DV
    nc = tp // L
    G = _mlstm_group(nc)

    def body(qk_ref, v_ref, gl_ref, dh_ref, call_ref, nall_ref, mall_ref, dqk_ref, dv_ref, dgl_ref, dc_s, dn_s):
        @pl.when(pl.program_id(0) == 0)
        def _():
            dc_s[...] = jnp.zeros_like(dc_s)
            dn_s[...] = jnp.zeros_like(dn_s)

        masks = _chunk_masks()
        eye, low, up = masks
        lane = lax.broadcasted_iota(jnp.int32, (L, LANE), 1)
        dgl = [jnp.zeros((L, LANE), F32) for _ in range(G)]
        for h, ci in [(h, ci) for h in range(H) for ci in reversed(range(G))]:
            rows = slice(ci * L, (ci + 1) * L)
            c_st, n_st, m_st = call_ref[ci, h], nall_ref[ci, h], mall_ref[ci, h][:, 0:1]
            q = qk_ref[rows, h * dk:(h + 1) * dk] * (dk ** -0.5)
            k = qk_ref[rows, MQK + h * dk:MQK + (h + 1) * dk]
            v = v_ref[rows, h * dv:(h + 1) * dv]
            dh = dh_ref[rows, h * dv:(h + 1) * dv]
            if ci == G - 1:
                dcn, dnn = dc_s[h], dn_s[h]
            gates = gl_ref[rows, :]
            f = _mlstm_chunk(q, k, v, gates[:, h:h + 1], gates[:, H + h:H + h + 1], c_st, n_st, m_st, masks)
            wint, s, wa_c, wc = f["wint"], f["s"], f["wa_c"], f["wc"]

            scale = jnp.maximum(jnp.abs(f["den"]), f["floor"])
            r = 1.0 / scale
            dnum = dh * r
            dscale = -jnp.sum(dh * f["num"], axis=1, keepdims=True) * r * r
            dden = jnp.where(jnp.abs(f["den"]) > f["floor"], dscale * jnp.sign(f["den"]), 0.0)
            ds = _nt(dnum, v) + dden
            wd = wint * dnum
            dwint = jnp.sum(dnum * f["qc"], axis=1, keepdims=True) + dden * f["qn"]
            dd = ds * s
            da_mat = ds * f["wi"]
            dq = _nt(wd, c_st) + (dden * wint) * n_st + _nn(da_mat, k)
            dk_ = _tn(da_mat, q)
            dv_ = _tn(s, dnum)
            dc_acc = _tn(q, wd)
            dn_acc = jnp.sum(q * (dden * wint), axis=0, keepdims=True)

            kd = _nn(k, dcn)
            dk_ = dk_ + wa_c * (_nt(v, dcn) + dnn)
            dv_ = dv_ + wa_c * kd
            dwa = jnp.sum(kd * v, axis=1, keepdims=True) + jnp.sum(k * dnn, axis=1, keepdims=True)
            dwc = jnp.sum(jnp.sum(dcn * c_st, axis=1, keepdims=True), axis=0, keepdims=True) \
                + jnp.sum(dnn * n_st, axis=1, keepdims=True)
            da_c = dwa * wa_c
            dg = jnp.sum(da_c, axis=0, keepdims=True) + dwc * wc

            dd_cols = jnp.sum(dd, axis=0, keepdims=True)
            db_c = dwint * wint + jnp.sum(dd, axis=1, keepdims=True) - da_c
            db_r = _to_row(db_c, eye) - dd_cols
            dlf = jnp.sum(jnp.where(up, db_r, 0.0), axis=1, keepdims=True) + dg
            dli = da_c + _to_col(dd_cols, eye)

            dcn, dnn = wc * dcn + dc_acc, wc * dnn + dn_acc
            if ci == 0:
                dc_s[h], dn_s[h] = dcn, dnn
            dqk_ref[rows, h * dk:(h + 1) * dk] = dq * (dk ** -0.5)
            dqk_ref[rows, MQK + h * dk:MQK + (h + 1) * dk] = dk_
            dv_ref[rows, h * dv:(h + 1) * dv] = dv_
            dgl[ci] = dgl[ci] + jnp.where(lane == h, dli, 0.0) + jnp.where(lane == H + h, dlf, 0.0)
        for ci in range(G):
            dgl_ref[ci * L:(ci + 1) * L, :] = dgl[ci]

    rev = lambda i: nc // G - 1 - i
    return pl.pallas_call(
        body, name=name, grid=(nc // G,),
        in_specs=[pl.BlockSpec((G * L, 2 * MQK), lambda i: (rev(i), 0)),
                  pl.BlockSpec((G * L, MV), lambda i: (rev(i), 1)),
                  pl.BlockSpec((G * L, LANE), lambda i: (rev(i), 0)),
                  pl.BlockSpec((G * L, MV), lambda i: (rev(i), 0)),
                  pl.BlockSpec((G, H, dk, dv), lambda i: (rev(i), 0, 0, 0)),
                  pl.BlockSpec((G, H, 1, dk), lambda i: (rev(i), 0, 0, 0)),
                  pl.BlockSpec((G, H, 1, LANE), lambda i: (rev(i), 0, 0, 0))],
        out_specs=[pl.BlockSpec((G * L, 2 * MQK), lambda i: (rev(i), 0)),
                   pl.BlockSpec((G * L, MV), lambda i: (rev(i), 0)),
                   pl.BlockSpec((G * L, LANE), lambda i: (rev(i), 0))],
        out_shape=[jax.ShapeDtypeStruct((tp, 2 * MQK), F32), jax.ShapeDtypeStruct((tp, MV), F32),
                   jax.ShapeDtypeStruct((tp, LANE), F32)],
        scratch_shapes=[pltpu.VMEM((H, dk, dv), F32), pltpu.VMEM((H, 1, dk), F32)],
        compiler_params=_params(("arbitrary",)),
    )(qk, proj, gl, dmix, call, nall, mall)


SB_EXP_CAP = 80.0
SB_Q0 = 3 * MV // LANE
SB_K0 = SB_Q0 + SB_HEADS
SB_V0 = SB_K0 + SB_HEADS


def _cumsum_dot(x, tri):
    hi = x.astype(jnp.bfloat16)
    lo = (x - hi.astype(F32)).astype(jnp.bfloat16)
    dims = (((1,), (0,)), ((), ()))
    return (lax.dot_general(hi, tri, dims, preferred_element_type=F32)
            + lax.dot_general(lo, tri, dims, preferred_element_type=F32))


def _sb_query_blocks(nq):
    return _divisor(nq, (5, 4, 3, 2, 1))


def _sb_mask(tile, g, tq):
    t_idx = tile * tq + lax.broadcasted_iota(jnp.int32, (tq, tq), 0)
    s_idx = g * tq + lax.broadcasted_iota(jnp.int32, (tq, tq), 1)
    return (s_idx < t_idx) & (s_idx >= PAD_FRONT)


def _sb_blocks(x):
    return [x[:, k * SB_BLOCK:(k + 1) * SB_BLOCK] for k in range(x.shape[1] // SB_BLOCK)]


def _sb_logits(qb, kg, tri, mask):
    z = _nt(qb, kg) * (SB_DH ** -0.5)
    zc = jnp.minimum(z, SB_EXP_CAP)
    l = (zc - z) - jnp.log(1.0 + jnp.exp(zc))
    if mask is not None:
        l = jnp.where(mask, l, 0.0)
    return z, l, [_cumsum_dot(b, tri) for b in _sb_blocks(l)]


def _sb_weights(z, withins, runs, mask):
    e = jnp.exp(z + jnp.concatenate([w + r for w, r in zip(withins, runs)], axis=1))
    return e if mask is None else jnp.where(mask, e, 0.0)


def _sb_segments(tile, group, descending):
    def diagonal():
        group(tile, True)

    def interior():
        def it(gg, c):
            group(tile - 1 - gg if descending else 1 + gg, False)
            return c
        lax.fori_loop(0, jnp.maximum(tile - 1, 0), it, 0)

    def first():
        @pl.when(tile > 0)
        def _():
            group(0, True)

    for part in ((diagonal, interior, first) if descending else (first, interior, diagonal)):
        part()


def _sb_fwd(proj, *, name):
    tp = proj.shape[0]
    B, H = SB_BLOCK, SB_HEADS
    nq = tp // B
    assert nq <= LANE and B == LANE
    r = _sb_query_blocks(nq)
    tq = r * B

    def body(q_ref, k_ref, v_ref, o_ref, ac_ref, run_s):
        tile = pl.program_id(1)
        qb = q_ref[...].astype(MXU_DTYPE)
        lane = lax.broadcasted_iota(jnp.int32, (tq, LANE), 1)
        tri = (lax.broadcasted_iota(jnp.int32, (B, B), 0) >= lax.broadcasted_iota(jnp.int32, (B, B), 1)
               ).astype(jnp.bfloat16)
        o_ref[...] = jnp.zeros_like(o_ref)
        ac_ref[0, 0] = jnp.zeros((tq, LANE), F32)
        run_s[...] = jnp.zeros_like(run_s)

        def group(g, masked):
            grows = pl.ds(pl.multiple_of(g * tq, tq), tq)
            mask = _sb_mask(tile, g, tq) if masked else None
            z, l, withins = _sb_logits(qb, k_ref[grows, :], tri, mask)
            run, saved, runs = run_s[...], ac_ref[0, 0], [None] * r
            for k in reversed(range(r)):
                runs[k] = run
                saved = jnp.where(lane == g * r + k, run, saved)
                run = run + withins[k][:, 0:1]
            o_ref[...] += _nn(_sb_weights(z, withins, runs, mask), v_ref[grows, :])
            ac_ref[0, 0] = saved
            run_s[...] = run

        _sb_segments(tile, group, descending=True)

    return pl.pallas_call(
        body, name=name, grid=(H, nq // r), scratch_shapes=[pltpu.VMEM((tq, LANE), F32)],
        in_specs=[pl.BlockSpec((tq, SB_DH), lambda h, i: (i, SB_Q0 + h)),
                  pl.BlockSpec((tp, SB_DH), lambda h, i: (0, SB_K0 + h)),
                  pl.BlockSpec((tp, SB_DH), lambda h, i: (0, SB_V0 + h))],
        out_specs=[pl.BlockSpec((tq, SB_DH), lambda h, i: (i, h)),
                   pl.BlockSpec((1, 1, tq, LANE), lambda h, i: (h, i, 0, 0))],
        out_shape=[jax.ShapeDtypeStruct((tp, SBW), F32), jax.ShapeDtypeStruct((H, nq // r, tq, LANE), F32)],
        compiler_params=_params(("parallel", "arbitrary")),
    )(proj, proj, proj)


def _sb_bwd(proj, across, dmix, *, name):
    tp = proj.shape[0]
    B, H = SB_BLOCK, SB_HEADS
    nq = tp // B
    r = _sb_query_blocks(nq)
    tq = r * B
    do0 = MV // LANE

    def body(q_ref, k_ref, v_ref, ac_ref, do_ref, dq_ref, dk_ref, dv_ref, gpre_s):
        tile = pl.program_id(1)

        @pl.when(tile == 0)
        def _():
            dk_ref[...] = jnp.zeros_like(dk_ref)
            dv_ref[...] = jnp.zeros_like(dv_ref)

        dq_ref[...] = jnp.zeros_like(dq_ref)
        gpre_s[...] = jnp.zeros_like(gpre_s)
        qb, dob = q_ref[...].astype(MXU_DTYPE), do_ref[...].astype(MXU_DTYPE)
        lane = lax.broadcasted_iota(jnp.int32, (tq, LANE), 1)
        rr = lax.broadcasted_iota(jnp.int32, (B, B), 0)
        cc = lax.broadcasted_iota(jnp.int32, (B, B), 1)
        tri = (rr >= cc).astype(jnp.bfloat16)
        prefix = (rr <= cc).astype(jnp.bfloat16)
        scale = SB_DH ** -0.5

        def group(g, masked):
            grows = pl.ds(pl.multiple_of(g * tq, tq), tq)
            kg, vg = k_ref[grows, :], v_ref[grows, :]
            mask = _sb_mask(tile, g, tq) if masked else None
            z, l, withins = _sb_logits(qb, kg, tri, mask)
            saved = ac_ref[0, 0]
            runs = [jnp.sum(jnp.where(lane == g * r + k, saved, 0.0), axis=1, keepdims=True) for k in range(r)]
            w = _sb_weights(z, withins, runs, mask)
            dv_ref[grows, :] += _tn(w, dob)
            gw = _nt(dob, vg) * w
            gpre, gcum = gpre_s[...], []
            for gc in [_cumsum_dot(b, prefix) for b in _sb_blocks(gw)]:
                gcum.append(gc + gpre)
                gpre = gpre + gc[:, B - 1:B]
            beta_g = jnp.exp(z + l) * jnp.concatenate(gcum, axis=1)
            if masked:
                beta_g = jnp.where(mask, beta_g, 0.0)
            dz = ((gw - beta_g) * scale).astype(MXU_DTYPE)
            dk_ref[grows, :] += _tn(dz, qb)
            dq_ref[...] += _nn(dz, kg)
            gpre_s[...] = gpre

        _sb_segments(tile, group, descending=False)

    return pl.pallas_call(
        body, name=name, grid=(H, nq // r), scratch_shapes=[pltpu.VMEM((tq, LANE), F32)],
        in_specs=[pl.BlockSpec((tq, SB_DH), lambda h, i: (i, SB_Q0 + h)),
                  pl.BlockSpec((tp, SB_DH), lambda h, i: (0, SB_K0 + h)),
                  pl.BlockSpec((tp, SB_DH), lambda h, i: (0, SB_V0 + h)),
                  pl.BlockSpec((1, 1, tq, LANE), lambda h, i: (h, i, 0, 0)),
                  pl.BlockSpec((tq, SB_DH), lambda h, i: (i, do0 + h))],
        out_specs=[pl.BlockSpec((tq, SB_DH), lambda h, i: (i, h)),
                   pl.BlockSpec((tp, SB_DH), lambda h, i: (0, h)),
                   pl.BlockSpec((tp, SB_DH), lambda h, i: (0, h))],
        out_shape=[jax.ShapeDtypeStruct((tp, SBW), F32)] * 3,
        compiler_params=_params(("parallel", "arbitrary")),
    )(proj, proj, proj, across, dmix)


def _ffn_forward(h, p, tag):
    u = _rms_fwd(h, p["g2"], out_dtype=MXU_DTYPE, name=f"{tag}_ffn_norm")
    ab = _matmul(u, p["w_gu"], name=f"{tag}_ffn_gate_up")
    s = _swiglu_fwd(ab, name=f"{tag}_ffn_act")
    f = _matmul(s, p["w_down"], name=f"{tag}_ffn_down")
    out = _rms_fwd(f, p["g3"], res=h, name=f"{tag}_ffn_out")
    return out, dict(h=h, u=u, ab=ab, s=s, f=f)


def _ffn_backward(dh, p, a, tag):
    df, dg3 = _rms_bwd(a["f"], p["g3"], dh, out_dtype=MXU_DTYPE, name=f"{tag}_ffn_out_bwd")
    ds = _matmul(df, p["w_down"], tb=True, name=f"{tag}_ffn_down_dx")
    dw_down = _matmul(a["s"], df, ta=True, name=f"{tag}_ffn_down_dw")
    dab = _swiglu_bwd(a["ab"], ds, name=f"{tag}_ffn_act_bwd")
    du = _matmul(dab, p["w_gu"], tb=True, name=f"{tag}_ffn_gate_up_dx")
    dw_gu = _matmul(a["u"], dab, ta=True, name=f"{tag}_ffn_gate_up_dw")
    dh_in, dg2 = _rms_bwd(a["h"], p["g2"], du, add=dh, name=f"{tag}_ffn_norm_bwd")
    return dh_in, dict(g2=dg2, g3=dg3, w_gu=dw_gu, w_down=dw_down)


def _mixer_forward(h, p, tag):
    u = _rms_fwd(h, p["g0"], out_dtype=MXU_DTYPE, name=f"{tag}_mix_norm")
    proj = _matmul(u, p["w_in"], name=f"{tag}_mix_in")
    qc = _conv_fwd(proj, p["qk_w"], p["qk_b"], name=f"{tag}_mix_qkconv")
    qk = _silu_fwd(qc, name=f"{tag}_mix_qkact")
    gl = _gates_fwd(proj, p["gate_b"], name=f"{tag}_mix_gates")
    hm, call, nall, mall = _mlstm_fwd(qk, proj, gl, name=f"{tag}_mlstm")
    hn = _hnorm_fwd(hm, proj, p["hnorm_g"], name=f"{tag}_mix_hnorm")
    hs, across = _sb_fwd(proj, name=f"{tag}_sb")
    mixed = jnp.concatenate([hn, hs], axis=1).astype(MXU_DTYPE)
    y = _matmul(mixed, p["w_out"], name=f"{tag}_mix_out")
    out = _rms_fwd(y, p["g1"], res=h, name=f"{tag}_mix_res")
    return out, dict(h=h, u=u, proj=proj, qc=qc, qk=qk, gl=gl, hm=hm, call=call, nall=nall, mall=mall,
                     across=across, mixed=mixed, y=y)


def _mixer_backward(dh, p, a, tag):
    tp = dh.shape[0]
    dy, dg1 = _rms_bwd(a["y"], p["g1"], dh, out_dtype=MXU_DTYPE, name=f"{tag}_mix_res_bwd")
    dmixed = _matmul(dy, p["w_out"], tb=True, name=f"{tag}_mix_out_dx")
    dw_out = _matmul(a["mixed"], dy, ta=True, name=f"{tag}_mix_out_dw")
    dsq, dsk, dsv = _sb_bwd(a["proj"], a["across"], dmixed, name=f"{tag}_sb_bwd")
    dhm, do, dhg = _hnorm_bwd(a["hm"], a["proj"], p["hnorm_g"], dmixed, name=f"{tag}_mix_hnorm_bwd")
    dqk, dv, dgl = _mlstm_bwd(a["qk"], a["proj"], a["gl"], dhm, a["call"], a["nall"], a["mall"],
                              name=f"{tag}_mlstm_bwd")
    dpg, dgate_b = _gates_bwd(a["proj"], p["gate_b"], dgl, name=f"{tag}_mix_gates_bwd")
    dqc = _silu_bwd(a["qc"], dqk, name=f"{tag}_mix_qkact_bwd")
    dpqk, dqk_w, dqk_b = _conv_bwd(a["proj"], p["qk_w"], dqc, name=f"{tag}_mix_qkconv_bwd")
    dproj = jnp.concatenate(
        [dpqk, dv, do, dsq, dsk, dsv, dpg, jnp.zeros((tp, PROJ_WIDTH - GATE_COL - LANE), F32)], axis=1
    ).astype(MXU_DTYPE)
    du = _matmul(dproj, p["w_in"], tb=True, name=f"{tag}_mix_in_dx")
    dw_in = _matmul(a["u"], dproj, ta=True, name=f"{tag}_mix_in_dw")
    dh_in, dg0 = _rms_bwd(a["h"], p["g0"], du, add=dh, name=f"{tag}_mix_norm_bwd")
    return dh_in, dict(g0=dg0, g1=dg1, w_in=dw_in, qk_w=dqk_w, qk_b=dqk_b, gate_b=dgate_b, hnorm_g=dhg,
                       w_out=dw_out)


def _conformer_forward(h, p, tag):
    u = _rms_fwd(h, p["g0"], out_dtype=MXU_DTYPE, name=f"{tag}_conf_norm")
    z = _matmul(u, p["w_pw1"], name=f"{tag}_conf_pw1")
    y1 = _glu_fwd(z, p["b_pw1"], name=f"{tag}_conf_glu")
    y2 = _conv_fwd(y1, p["w_dw"], p["b_dw"], name=f"{tag}_conf_dw")
    y3 = _lnsilu_fwd(y2, p["ln_g"], p["ln_b"], name=f"{tag}_conf_ln")
    y4 = _matmul(y3, p["w_pw2"], name=f"{tag}_conf_pw2")
    out = _rms_fwd(y4, p["g1"], res=h, bias=p["b_pw2"], name=f"{tag}_conf_res")
    return out, dict(h=h, u=u, z=z, y1=y1, y2=y2, y3=y3, y4=y4)


def _conformer_backward(dh, p, a, tag):
    dy4, dg1, db_pw2 = _rms_bwd(a["y4"], p["g1"], dh, bias=p["b_pw2"], out_dtype=MXU_DTYPE,
                                name=f"{tag}_conf_res_bwd")
    dy3 = _matmul(dy4, p["w_pw2"], tb=True, name=f"{tag}_conf_pw2_dx")
    dw_pw2 = _matmul(a["y3"], dy4, ta=True, name=f"{tag}_conf_pw2_dw")
    dy2, dln_g, dln_b = _lnsilu_bwd(a["y2"], p["ln_g"], p["ln_b"], dy3, name=f"{tag}_conf_ln_bwd")
    dy1, dw_dw, db_dw = _conv_bwd(a["y1"], p["w_dw"], dy2, name=f"{tag}_conf_dw_bwd")
    dz, db_pw1 = _glu_bwd(a["z"], p["b_pw1"], dy1, name=f"{tag}_conf_glu_bwd")
    du = _matmul(dz, p["w_pw1"], tb=True, name=f"{tag}_conf_pw1_dx")
    dw_pw1 = _matmul(a["u"], dz, ta=True, name=f"{tag}_conf_pw1_dw")
    dh_in, dg0 = _rms_bwd(a["h"], p["g0"], du, add=dh, name=f"{tag}_conf_norm_bwd")
    return dh_in, dict(g0=dg0, g1=dg1, w_pw1=dw_pw1, b_pw1=db_pw1, w_dw=dw_dw, b_dw=db_dw, ln_g=dln_g,
                       ln_b=dln_b, w_pw2=dw_pw2, b_pw2=db_pw2)


def _trunk_step(h0, target, layers):
    acts = []
    h = h0
    for li, p in enumerate(layers):
        tag = f"l{li}"
        h, a_mix = (_mixer_forward if li % 2 == 0 else _conformer_forward)(h, p["mix"], tag)
        h, a_ffn = _ffn_forward(h, p["ffn"], tag)
        acts.append((a_mix, a_ffn))
    dh, loss_cols = _loss_fwd_bwd(h, target, name="loss")
    grads = [None] * len(layers)
    for li in reversed(range(len(layers))):
        tag = f"l{li}"
        p = layers[li]
        dh, g_ffn = _ffn_backward(dh, p["ffn"], acts[li][1], tag)
        dh, g_mix = (_mixer_backward if li % 2 == 0 else _conformer_backward)(dh, p["mix"], acts[li][0], tag)
        grads[li] = dict(mix=g_mix, ffn=g_ffn)
    return loss_cols, dh, grads


_SPLIT = 2 * MQK + 2 * MV


def _prepare_layers(w):
    layers = []
    row = lambda v: v[None, :].astype(F32)
    for li in range(DEPTH):
        i = li // 2
        g = w["norm_g"][li].astype(F32)
        if li % 2 == 0:
            win = w["mix_w_in"][i]
            w_in = jnp.concatenate(
                [win[:, :_SPLIT], win[:, _SPLIT + 2 * MLSTM_HEADS:], win[:, _SPLIT:_SPLIT + 2 * MLSTM_HEADS],
                 jnp.zeros((D_MODEL, PROJ_WIDTH - IN_WIDTH), win.dtype)], axis=1)
            gate_b = jnp.pad(row(w["mix_gate_b"][i]), ((0, 0), (0, LANE - 2 * MLSTM_HEADS)))
            mix = dict(g0=g[0:1], g1=g[1:2], w_in=w_in, qk_w=w["mix_qk_conv_w"][i].astype(F32),
                       qk_b=row(w["mix_qk_conv_b"][i]), gate_b=gate_b, hnorm_g=row(w["mix_hnorm_g"][i]),
                       w_out=w["mix_w_out"][i])
        else:
            mix = dict(g0=g[0:1], g1=g[1:2], w_pw1=w["conv_w_pw1"][i], b_pw1=row(w["conv_b_pw1"][i]),
                       w_dw=w["conv_w_dw"][i].astype(F32), b_dw=row(w["conv_b_dw"][i]),
                       ln_g=row(w["conv_ln_g"][i]), ln_b=row(w["conv_ln_b"][i]), w_pw2=w["conv_w_pw2"][i],
                       b_pw2=row(w["conv_b_pw2"][i]))
        ffn = dict(g2=g[2:3], g3=g[3:4],
                   w_gu=jnp.concatenate([w["ffn_w_gate"][li], w["ffn_w_up"][li]], axis=1),
                   w_down=w["ffn_w_down"][li])
        layers.append(dict(mix=mix, ffn=ffn))
    return layers


def _collect_grads(grads):
    even = [grads[li]["mix"] for li in range(0, DEPTH, 2)]
    odd = [grads[li]["mix"] for li in range(1, DEPTH, 2)]
    ffn = [grads[li]["ffn"] for li in range(DEPTH)]
    st = lambda xs: jnp.stack(xs, axis=0)
    vec = lambda xs, k: st([x[k][0] for x in xs])
    out = {}
    out["norm_g"] = st([jnp.concatenate([grads[li]["mix"]["g0"], grads[li]["mix"]["g1"], grads[li]["ffn"]["g2"],
                                         grads[li]["ffn"]["g3"]], axis=0) for li in range(DEPTH)])
    out["mix_w_in"] = st([jnp.concatenate(
        [g["w_in"][:, :_SPLIT], g["w_in"][:, GATE_COL:GATE_COL + 2 * MLSTM_HEADS], g["w_in"][:, _SPLIT:GATE_COL]],
        axis=1) for g in even])
    out["mix_qk_conv_w"] = st([g["qk_w"] for g in even])
    out["mix_qk_conv_b"] = vec(even, "qk_b")
    out["mix_gate_b"] = st([g["gate_b"][0, :2 * MLSTM_HEADS] for g in even])
    out["mix_hnorm_g"] = vec(even, "hnorm_g")
    out["mix_w_out"] = st([g["w_out"] for g in even])
    out["conv_w_pw1"] = st([g["w_pw1"] for g in odd])
    out["conv_b_pw1"] = vec(odd, "b_pw1")
    out["conv_w_dw"] = st([g["w_dw"] for g in odd])
    out["conv_b_dw"] = vec(odd, "b_dw")
    out["conv_ln_g"] = vec(odd, "ln_g")
    out["conv_ln_b"] = vec(odd, "ln_b")
    out["conv_w_pw2"] = st([g["w_pw2"] for g in odd])
    out["conv_b_pw2"] = vec(odd, "b_pw2")
    out["ffn_w_gate"] = st([g["w_gu"][:, :FFN_HIDDEN] for g in ffn])
    out["ffn_w_up"] = st([g["w_gu"][:, FFN_HIDDEN:] for g in ffn])
    out["ffn_w_down"] = st([g["w_down"] for g in ffn])
    return out


def _local_step(x, target, w):
    seq = x.shape[0]
    h0 = jnp.concatenate([jnp.zeros((PAD_FRONT, D_MODEL), F32), w["meta"].astype(F32), x], axis=0)
    tgt = jnp.concatenate([jnp.zeros((PAD_FRONT + N_META, D_MODEL), F32), target], axis=0)
    loss_cols, dh0, grads = _trunk_step(h0, tgt, _prepare_layers(w))
    out = _collect_grads(grads)
    out["meta"] = dh0[PAD_FRONT:PAD_FRONT + N_META]
    loss = 0.5 * jnp.sum(loss_cols) / D_MODEL
    return loss, dh0[PAD_FRONT + N_META:PAD_FRONT + N_META + seq], out


def _elementwise(fn, arrays, out_dtypes, *, name):
    shape = arrays[0].shape
    cols = shape[-1]
    rows = 1
    for s in shape[:-1]:
        rows *= s
    flat = [a.reshape(rows, cols) for a in arrays]
    if rows * cols * 4 <= (1 << 20) or rows % SUBLANE:
        tr = rows
    else:
        tr = _divisor(rows, (512, 256, 128, 64, 32, 16, 8))
    n = len(flat)

    def body(*refs):
        outs = fn(*[r[...] for r in refs[:n]])
        for o_ref, o in zip(refs[n:], outs):
            o_ref[...] = o.astype(o_ref.dtype)

    spec = pl.BlockSpec((tr, cols), lambda i: (i, 0))
    outs = pl.pallas_call(
        body, name=name, grid=(rows // tr,), in_specs=[spec] * n, out_specs=[spec] * len(out_dtypes),
        out_shape=[jax.ShapeDtypeStruct((rows, cols), dt) for dt in out_dtypes],
        compiler_params=_params(("parallel",)),
    )(*flat)
    return [o.reshape(shape) for o in outs]


def _adamw(w, g, m, v, *, name):
    def fn(wv, gv, mv, vv):
        mn = ADAM_B1 * mv + (1.0 - ADAM_B1) * gv
        vn = ADAM_B2 * vv + (1.0 - ADAM_B2) * (gv * gv)
        m_hat = mn / (1.0 - ADAM_B1 ** ADAM_STEP)
        v_hat = vn / (1.0 - ADAM_B2 ** ADAM_STEP)
        return [-ADAM_LR * (m_hat / (jnp.sqrt(v_hat) + ADAM_EPS) + ADAM_WD * wv), mn, vn]

    return _elementwise(fn, [w, g, m, v], [F32, F32, F32], name=name)


MESH_ID = pl.DeviceIdType.MESH
ANY = pl.BlockSpec(memory_space=pl.ANY)


def _place():
    x, y, c = lax.axis_index("x"), lax.axis_index("y"), lax.axis_index("c")
    return x, y, c, [(1 - x, y), (x, 1 - y), (1 - x, 1 - y)]


def _remote(src, dst, send_sems, recv_sems, k, to):
    return pltpu.make_async_remote_copy(src_ref=src, dst_ref=dst, send_sem=send_sems.at[k], recv_sem=recv_sems.at[k],
                                        device_id=to, device_id_type=MESH_ID)


def _comm_call(body, arrays, out_shapes, n_remote, n_local, name):
    return pl.pallas_call(
        body, name=name, in_specs=[ANY] * len(arrays), out_specs=[ANY] * len(out_shapes), out_shape=out_shapes,
        scratch_shapes=[pltpu.SemaphoreType.DMA((n_remote,)), pltpu.SemaphoreType.DMA((n_remote,)),
                        pltpu.SemaphoreType.DMA((n_local,))],
        compiler_params=pltpu.CompilerParams(has_side_effects=True),
    )(*arrays)


def _gather_chips(shards, *, name):
    n = len(shards)

    def body(*refs):
        ins, outs = refs[:n], refs[n:2 * n]
        send_sems, recv_sems, local_sems = refs[2 * n:]
        x, y, c, chips = _place()
        me, sibling = 2 * x + y, (x, y, 1 - c)

        def half(a, slot, hc):
            hl = ins[a].shape[0] // 2
            return outs[a].at[slot].at[pl.ds(hc * hl, hl)]

        def mine(a):
            hl = ins[a].shape[0] // 2
            return ins[a].at[pl.ds(c * hl, hl)]

        sent = []
        for a in range(n):
            for j, (px, py) in enumerate(chips):
                sent.append(_remote(mine(a), half(a, me, c), send_sems, recv_sems, 6 * a + j, (px, py, c)))
                sent[-1].start()
        for a in range(n):
            for j, (px, py) in enumerate(chips):
                slot = 2 * px + py
                _remote(mine(a), half(a, slot, c), send_sems, recv_sems, 6 * a + j, (px, py, c)).wait_recv()
                sent.append(_remote(half(a, slot, c), half(a, slot, c), send_sems, recv_sems, 6 * a + 3 + j, sibling))
                sent[-1].start()
        for a in range(n):
            for j, (px, py) in enumerate(chips):
                slot = 2 * px + py
                _remote(mine(a), half(a, slot, 1 - c), send_sems, recv_sems, 6 * a + 3 + j, sibling).wait_recv()
        for cp in sent:
            cp.wait_send()

    out_shapes = [jax.ShapeDtypeStruct((4,) + s.shape, s.dtype) for s in shards]
    return _comm_call(body, shards, out_shapes, 6 * n, 1, name)


def _swap_siblings(arrays, *, by_core, name):
    n = len(arrays)

    def body(*refs):
        ins, outs = refs[:n], refs[n:2 * n]
        send_sems, recv_sems, _ = refs[2 * n:]
        x, y, c, _chips = _place()
        cps = [_remote(ins[a].at[1 - c] if by_core else ins[a], outs[a], send_sems, recv_sems, a, (x, y, 1 - c))
               for a in range(n)]
        for cp in cps:
            cp.start()
        for cp in cps:
            cp.wait()

    out_shapes = [jax.ShapeDtypeStruct(a.shape[1:] if by_core else a.shape, a.dtype) for a in arrays]
    return _comm_call(body, arrays, out_shapes, n, 1, name)


def _scatter_chips(parts, small, *, name):
    n = len(parts)

    def body(*refs):
        ins, small_in = refs[:n], refs[n]
        outs, small_out = refs[n + 1:2 * n + 1], refs[2 * n + 1]
        send_sems, recv_sems, local_sems = refs[2 * n + 2:]
        x, y, c, chips = _place()
        me8 = 4 * x + 2 * y + c
        own = pltpu.make_async_copy(small_in, small_out.at[me8], local_sems.at[0])
        own.start()
        cps = []
        for fx in range(2):
            for fy in range(2):
                for fc in range(2):
                    r = 4 * fx + 2 * fy + fc - 1
                    if r >= 0:
                        to = (x + fx - 2 * x * fx, y + fy - 2 * y * fy, c + fc - 2 * c * fc)
                        cps.append(_remote(small_in, small_out.at[me8], send_sems, recv_sems, r, to))
        for a in range(n):
            for j, (px, py) in enumerate(chips):
                cps.append(_remote(ins[a].at[2 * px + py], outs[a].at[j], send_sems, recv_sems, 7 + 3 * a + j,
                                   (px, py, c)))
        for cp in cps:
            cp.start()
        for cp in cps:
            cp.wait()
        own.wait()

    out_shapes = [jax.ShapeDtypeStruct((3,) + p.shape[1:], p.dtype) for p in parts]
    out_shapes.append(jax.ShapeDtypeStruct((8,) + small.shape, small.dtype))
    return _comm_call(body, list(parts) + [small], out_shapes, 7 + 3 * n, 1, name)


WEIGHTS = ("meta", "norm_g", "mix_w_in", "mix_qk_conv_w", "mix_qk_conv_b", "mix_gate_b", "mix_hnorm_g", "mix_w_out",
           "conv_w_pw1", "conv_b_pw1", "conv_w_dw", "conv_b_dw", "conv_ln_g", "conv_ln_b", "conv_w_pw2",
           "conv_b_pw2", "ffn_w_gate", "ffn_w_up", "ffn_w_down")
SHARD_AXIS = dict(meta=1, norm_g=2, mix_w_in=2, mix_qk_conv_w=2, mix_qk_conv_b=None, mix_gate_b=None,
                  mix_hnorm_g=None, mix_w_out=1, conv_w_pw1=2, conv_b_pw1=1, conv_w_dw=2, conv_b_dw=1, conv_ln_g=1,
                  conv_ln_b=1, conv_w_pw2=1, conv_b_pw2=1, ffn_w_gate=2, ffn_w_up=2, ffn_w_down=1)
MATRICES = ("mix_w_in", "mix_w_out", "conv_w_pw1", "conv_w_pw2", "ffn_w_gate", "ffn_w_up", "ffn_w_down")
VECTORS = tuple(n for n in WEIGHTS if n not in MATRICES)
GATHER_COLS = D_MODEL // 4


def _pack_rows(arrays, cols, pad_to):
    rows = [a.astype(F32).reshape(-1) for a in arrays]
    rows = [jnp.pad(r, (0, (-r.shape[0]) % cols)).reshape(-1, cols) for r in rows]
    packed = jnp.concatenate(rows, axis=0)
    return jnp.pad(packed, ((0, pad_to - packed.shape[0]), (0, 0))), [r.shape[0] for r in rows]


def _unpack_rows(packed, counts, shapes):
    out, at = [], 0
    for n, shape in zip(counts, shapes):
        size = 1
        for s in shape:
            size *= s
        out.append(packed[..., at:at + n, :].reshape(packed.shape[:-2] + (-1,))[..., :size]
                   .reshape(packed.shape[:-2] + tuple(shape)))
        at += n
    return out


def _gather_weights(local):
    sharded_vecs = [n for n in VECTORS if SHARD_AXIS[n] is not None]
    pack, counts = _pack_rows([local[n] for n in sharded_vecs], GATHER_COLS, 120)
    shards = [local[n].astype(MXU_DTYPE) for n in MATRICES] + [pack.reshape(2, 60, GATHER_COLS)]
    me = 2 * lax.axis_index("x") + lax.axis_index("y")
    got = [lax.dynamic_update_index_in_dim(g, s, me, 0) for g, s in zip(_gather_chips(shards, name="gather_weights"),
                                                                        shards)]
    full = {n: local[n] for n in VECTORS if SHARD_AXIS[n] is None}
    for n, g in zip(MATRICES, got):
        full[n] = jnp.concatenate([g[k] for k in range(4)], axis=SHARD_AXIS[n])
    vecs = _unpack_rows(got[-1].reshape(4, 120, GATHER_COLS), counts, [local[n].shape for n in sharded_vecs])
    for n, v in zip(sharded_vecs, vecs):
        full[n] = jnp.moveaxis(v, 0, -2).reshape(v.shape[1:-1] + (4 * v.shape[-1],))
    return full


def _reduce_grads(grads):
    x, y, c = lax.axis_index("x"), lax.axis_index("y"), lax.axis_index("c")
    me = 2 * x + y
    stacked = []
    for n in MATRICES:
        g = jnp.stack(jnp.split(grads[n], 4, axis=SHARD_AXIS[n]), axis=0)
        g = g.reshape((4, 2, g.shape[1] // 2) + g.shape[2:])
        stacked.append(jnp.swapaxes(g, 0, 1))
    theirs = _swap_siblings(stacked, by_core=True, name="reduce_pair_swap")
    pair = [_elementwise(lambda a, b: [a + b], [lax.dynamic_index_in_dim(s, c, 0, keepdims=False), t], [F32],
                         name=f"reduce_pair_sum_{n}")[0] for n, s, t in zip(MATRICES, stacked, theirs)]
    shapes = [grads[n].shape for n in VECTORS]
    pack, counts = _pack_rows([grads[n] for n in VECTORS], D_MODEL, 120)
    got = _scatter_chips([p.astype(jnp.bfloat16) for p in pair], pack, name="reduce_chips")
    halves = []
    for n, p, r in zip(MATRICES, pair, got[:-1]):
        own = lax.dynamic_index_in_dim(p, me, 0, keepdims=False)
        halves.append(_elementwise(lambda a, b0, b1, b2: [((a + b0.astype(F32)) + b1.astype(F32)) + b2.astype(F32)],
                                   [own, r[0], r[1], r[2]], [F32], name=f"reduce_chip_sum_{n}")[0])
    others = _swap_siblings(halves, by_core=False, name="reduce_join")
    out = {n: jnp.where(c == 0, jnp.concatenate([h, o], axis=0), jnp.concatenate([o, h], axis=0))
           for n, h, o in zip(MATRICES, halves, others)}
    small = got[-1]
    total = _elementwise(lambda *s: [functools.reduce(lambda a, b: a + b, s)], [small[k] for k in range(8)], [F32],
                         name="reduce_small_sum")[0]
    for n, v in zip(VECTORS, _unpack_rows(total, counts, shapes)):
        ax = SHARD_AXIS[n]
        if ax is not None:
            w = v.shape[ax] // 4
            v = lax.dynamic_slice_in_dim(v, me * w, w, axis=ax)
        out[n] = v
    return out


def kernel(x, meta, norm_g, mix_w_in, mix_qk_conv_w, mix_qk_conv_b, mix_gate_b, mix_hnorm_g, mix_w_out, conv_w_pw1, conv_b_pw1, conv_w_dw, conv_b_dw, conv_ln_g, conv_ln_b, conv_w_pw2, conv_b_pw2, ffn_w_gate, ffn_w_up, ffn_w_down, loss_target, m_meta, m_norm_g, m_mix_w_in, m_mix_qk_conv_w, m_mix_qk_conv_b, m_mix_gate_b, m_mix_hnorm_g, m_mix_w_out, m_conv_w_pw1, m_conv_b_pw1, m_conv_w_dw, m_conv_b_dw, m_conv_ln_g, m_conv_ln_b, m_conv_w_pw2, m_conv_b_pw2, m_ffn_w_gate, m_ffn_w_up, m_ffn_w_down, v_meta, v_norm_g, v_mix_w_in, v_mix_qk_conv_w, v_mix_qk_conv_b, v_mix_gate_b, v_mix_hnorm_g, v_mix_w_out, v_conv_w_pw1, v_conv_b_pw1, v_conv_w_dw, v_conv_b_dw, v_conv_ln_g, v_conv_ln_b, v_conv_w_pw2, v_conv_b_pw2, v_ffn_w_gate, v_ffn_w_up, v_ffn_w_down):
    given = dict(locals())
    local = {n: given[n] for n in WEIGHTS}
    full = _gather_weights(local)
    loss, grad_x, grads = _local_step(x[0], loss_target[0], full)
    loss = lax.psum(loss, ("x", "y", "c"))
    grad_w = _reduce_grads(grads)
    delta, new_m, new_v = {}, {}, {}
    for n in WEIGHTS:
        delta[n], new_m[n], new_v[n] = _adamw(local[n], grad_w[n], given["m_" + n], given["v_" + n], name=f"adamw_{n}")
    return (loss, grad_x[None], *[grad_w[n] for n in WEIGHTS], *[delta[n] for n in WEIGHTS],
            *[new_m[n] for n in WEIGHTS], *[new_v[n] for n in WEIGHTS])
```

```python
import functools

import jax
import jax.numpy as jnp
from jax import lax
from jax.experimental import pallas as pl
from jax.experimental.pallas import tpu as pltpu

F32 = jnp.float32
MXU_DTYPE = jnp.bfloat16

D_MODEL = 1024
N_META = 16
DEPTH = 4
MLSTM_HEADS = 4
MLSTM_DQK = 128
MLSTM_DV = 256
MLSTM_CHUNK = 64
QK_CONV_WIDTH = 4
GATE_SOFTCAP = 15.0
SB_HEADS = 4
SB_DH = 128
SB_BLOCK = 128
PAD_FRONT = SB_BLOCK - N_META
CONV_WIDTH = 31
FFN_HIDDEN = 2816
MQK = MLSTM_HEADS * MLSTM_DQK
MV = MLSTM_HEADS * MLSTM_DV
SBW = SB_HEADS * SB_DH
IN_WIDTH = 2 * MQK + 2 * MV + 2 * MLSTM_HEADS + 3 * SBW
MIX_WIDTH = MV + SBW
NEG = -1e30
EPS = 1e-6
PROJ_WIDTH = 5120
GATE_COL = 3 * MV + 3 * SBW
LANE = 128
SUBLANE = 8
CONV_HALO = 32
VMEM_LIMIT = 56 * 1024 * 1024

ADAM_LR = 0.001
ADAM_B1 = 0.9
ADAM_B2 = 0.999
ADAM_EPS = 1e-08
ADAM_WD = 0.01
ADAM_STEP = 10


def _divisor(n, cands):
    for c in cands:
        if n % c == 0:
            return c
    raise ValueError(f"no tile for {n} in {cands}")


ROW_TILE_BYTES = 20 * 1024 * 1024


def _row_tile(tp, width=D_MODEL):
    for c in (640, 512, 384, 320, 256, 128, 64):
        if tp % c == 0 and c * width * 8 <= ROW_TILE_BYTES:
            return c
    raise ValueError(f"no row tile for {tp} x {width}")


def _params(sem):
    return pltpu.CompilerParams(dimension_semantics=sem, vmem_limit_bytes=VMEM_LIMIT)


def _dot(a, b, dims):
    return lax.dot_general(a.astype(MXU_DTYPE), b.astype(MXU_DTYPE), (dims, ((), ())),
                           preferred_element_type=F32)


def _nn(a, b):
    return _dot(a, b, ((1,), (0,)))


def _nt(a, b):
    return _dot(a, b, ((1,), (1,)))


def _tn(a, b):
    return _dot(a, b, ((0,), (0,)))


def _sigmoid(x):
    return 1.0 / (1.0 + jnp.exp(-x))


def _softplus(x):
    return jnp.maximum(x, 0.0) + jnp.log(1.0 + jnp.exp(-jnp.abs(x)))


MATMUL_VMEM_BYTES = 40 * 1024 * 1024


def _matmul_tiles(m, n, k, a_bytes, b_bytes):
    tm = _divisor(m, (1040, 1024, 1408, 768, 640, 512, 384, 256, 128))
    tn = _divisor(n, (1408, 1280, 1024, 768, 512, 256, 128))
    for tk in (5632, 5120, 2816, 2560, 2048, 1664, 1536, 1408, 1280, 1040, 1024, 768, 640, 512, 384, 256, 128):
        if k % tk:
            continue
        need = 2 * (tm * tk * a_bytes + tk * tn * b_bytes + tm * tn * 4) + (tm * tn * 4 if tk < k else 0)
        if need <= MATMUL_VMEM_BYTES:
            return tm, tn, tk
    raise ValueError(f"no matmul tiles for {m}x{n}x{k}")


def _matmul(a, b, *, ta=False, tb=False, name):
    m, k = (a.shape[1], a.shape[0]) if ta else a.shape
    n = b.shape[0] if tb else b.shape[1]
    assert (b.shape[1] if tb else b.shape[0]) == k, (a.shape, b.shape, ta, tb)
    tm, tn, tk = _matmul_tiles(m, n, k, a.dtype.itemsize, b.dtype.itemsize)
    nk = k // tk
    dims = ((0 if ta else 1,), (1 if tb else 0,))

    def body(a_ref, b_ref, o_ref, *acc):
        if nk == 1:
            o_ref[...] = _dot(a_ref[...], b_ref[...], dims)
            return
        acc_ref, kk = acc[0], pl.program_id(2)

        @pl.when(kk == 0)
        def _():
            acc_ref[...] = jnp.zeros_like(acc_ref)

        acc_ref[...] += _dot(a_ref[...], b_ref[...], dims)

        @pl.when(kk == nk - 1)
        def _():
            o_ref[...] = acc_ref[...]

    a_spec = (pl.BlockSpec((tk, tm), lambda i, j, kk: (kk, i)) if ta
              else pl.BlockSpec((tm, tk), lambda i, j, kk: (i, kk)))
    b_spec = (pl.BlockSpec((tn, tk), lambda i, j, kk: (j, kk)) if tb
              else pl.BlockSpec((tk, tn), lambda i, j, kk: (kk, j)))
    return pl.pallas_call(
        body, name=name, grid=(m // tm, n // tn, nk),
        in_specs=[a_spec, b_spec],
        out_specs=pl.BlockSpec((tm, tn), lambda i, j, kk: (i, j)),
        out_shape=jax.ShapeDtypeStruct((m, n), F32),
        scratch_shapes=[pltpu.VMEM((tm, tn), F32)] if nk > 1 else [],
        compiler_params=_params(("parallel", "parallel", "arbitrary")),
    )(a, b)


def _rowwise(fn, rows, fulls, out_rows, out_accs, *, name, out_dtypes=None):
    tp = rows[0][0].shape[0]
    tm = _row_tile(tp, sum(w for _, _, w in rows) + sum(out_rows))
    nr, nf, no, na = len(rows), len(fulls), len(out_rows), len(out_accs)
    out_dtypes = out_dtypes or [F32] * no

    def body(*refs):
        i = pl.program_id(0)
        outs = fn(i * tm, *[r[...].astype(F32) for r in refs[:nr + nf]])
        for k in range(no):
            refs[nr + nf + k][...] = outs[k].astype(out_dtypes[k])
        for k in range(na):
            ref = refs[nr + nf + no + k]

            @pl.when(i == 0)
            def _(ref=ref):
                ref[...] = jnp.zeros_like(ref)

            ref[...] += outs[no + k]

    in_specs = [pl.BlockSpec((tm, w), functools.partial(lambda i, cb: (i, cb), cb=cb)) for _, cb, w in rows]
    in_specs += [pl.BlockSpec(f.shape, lambda i: (0, 0)) for f in fulls]
    out_specs = [pl.BlockSpec((tm, w), lambda i: (i, 0)) for w in out_rows]
    out_specs += [pl.BlockSpec(s, lambda i: (0, 0)) for s in out_accs]
    out_shape = [jax.ShapeDtypeStruct((tp, w), dt) for w, dt in zip(out_rows, out_dtypes)]
    out_shape += [jax.ShapeDtypeStruct(s, F32) for s in out_accs]
    return pl.pallas_call(
        body, name=name, grid=(tp // tm,), in_specs=in_specs, out_specs=out_specs, out_shape=out_shape,
        compiler_params=_params(("arbitrary",)),
    )(*[r[0] for r in rows], *fulls)


def _whole(a):
    return (a, 0, a.shape[1])


def _live(row0, tm):
    return (row0 + lax.broadcasted_iota(jnp.int32, (tm, 1), 0)) >= PAD_FRONT


def _rms_core(x, g):
    r = lax.rsqrt(jnp.mean(x * x, axis=-1, keepdims=True) + EPS)
    return x * r, r


def _rms_fwd(x, g, *, name, res=None, bias=None, out_dtype=F32):
    def fn(row0, *blk):
        it = iter(blk)
        xv = next(it)
        rv = next(it) if res is not None else None
        gv = next(it)
        if bias is not None:
            xv = xv + next(it)
        xh, _ = _rms_core(xv, gv)
        y = jnp.where(_live(row0, xv.shape[0]), xh * gv, 0.0)
        return [y + rv if rv is not None else y]

    rows = [_whole(x)] + ([_whole(res)] if res is not None else [])
    fulls = [g] + ([bias] if bias is not None else [])
    return _rowwise(fn, rows, fulls, [x.shape[1]], [], name=name, out_dtypes=[out_dtype])[0]


def _rms_bwd(x, g, dy, *, name, add=None, bias=None, dy2=None, out_dtype=F32):
    def fn(row0, *blk):
        it = iter(blk)
        xv, dyv = next(it), next(it)
        if dy2 is not None:
            dyv = dyv + next(it)
        av = next(it) if add is not None else None
        gv = next(it)
        if bias is not None:
            xv = xv + next(it)
        dyv = jnp.where(_live(row0, xv.shape[0]), dyv, 0.0)
        xh, r = _rms_core(xv, gv)
        dyg = dyv * gv
        dx = r * (dyg - xh * jnp.mean(dyg * xh, axis=-1, keepdims=True))
        outs = [dx + av if av is not None else dx, jnp.sum(dyv * xh, axis=0, keepdims=True)]
        if bias is not None:
            outs.append(jnp.sum(dx, axis=0, keepdims=True))
        return outs

    rows = [_whole(x), _whole(dy)] + ([_whole(dy2)] if dy2 is not None else []) \
        + ([_whole(add)] if add is not None else [])
    fulls = [g] + ([bias] if bias is not None else [])
    c = x.shape[1]
    return _rowwise(fn, rows, fulls, [c], [(1, c)] * (2 if bias is not None else 1), name=name,
                    out_dtypes=[out_dtype])


def _ffn_tiles(m, n):
    return _divisor(m, (640, 384, 256, 128)), _divisor(n, (1408, 1024, 768, 512, 256, 128))


def _ffn_in(u, wg, wu, *, name):
    (m, k), n = u.shape, wg.shape[1]
    tm, tn = _ffn_tiles(m, n)

    def body(u_ref, wg_ref, wu_ref, a_ref, b_ref, s_ref):
        x = u_ref[...]
        a, b = _nn(x, wg_ref[...]), _nn(x, wu_ref[...])
        a_ref[...] = a
        b_ref[...] = b
        s_ref[...] = (a * _sigmoid(a) * b).astype(s_ref.dtype)

    w_spec = pl.BlockSpec((k, tn), lambda i, j: (0, j))
    o_spec = pl.BlockSpec((tm, tn), lambda i, j: (i, j))
    return pl.pallas_call(
        body, name=name, grid=(m // tm, n // tn),
        in_specs=[pl.BlockSpec((tm, k), lambda i, j: (i, 0)), w_spec, w_spec], out_specs=[o_spec] * 3,
        out_shape=[jax.ShapeDtypeStruct((m, n), F32), jax.ShapeDtypeStruct((m, n), F32),
                   jax.ShapeDtypeStruct((m, n), MXU_DTYPE)],
        compiler_params=_params(("parallel", "parallel")),
    )(u, wg, wu)


def _ffn_down_dx(df, wd, a, b, *, name):
    (m, k), n = df.shape, wd.shape[0]
    tm, tn = _ffn_tiles(m, n)

    def body(d_ref, w_ref, a_ref, b_ref, da_ref, db_ref):
        ds = _nt(d_ref[...], w_ref[...])
        av, bv = a_ref[...], b_ref[...]
        sg = _sigmoid(av)
        da_ref[...] = (ds * bv * sg * (1.0 + av * (1.0 - sg))).astype(da_ref.dtype)
        db_ref[...] = (ds * av * sg).astype(db_ref.dtype)

    t_spec = pl.BlockSpec((tm, tn), lambda i, j: (i, j))
    return pl.pallas_call(
        body, name=name, grid=(m // tm, n // tn),
        in_specs=[pl.BlockSpec((tm, k), lambda i, j: (i, 0)), pl.BlockSpec((tn, k), lambda i, j: (j, 0)), t_spec,
                  t_spec],
        out_specs=[t_spec, t_spec], out_shape=[jax.ShapeDtypeStruct((m, n), MXU_DTYPE)] * 2,
        compiler_params=_params(("parallel", "parallel")),
    )(df, wd, a, b)


def _glu_fwd(z, b, *, name):
    h = z.shape[1] // 2

    def fn(row0, a, gt, bv):
        y = (a + bv[:, :h]) * _sigmoid(gt + bv[:, h:])
        return [jnp.where(_live(row0, a.shape[0]), y, 0.0)]

    return _rowwise(fn, [(z, 0, h), (z, 1, h)], [b], [h], [], name=name)[0]


def _glu_bwd(z, b, dy, *, name):
    h = z.shape[1] // 2

    def fn(row0, a, gt, d, bv):
        d = jnp.where(_live(row0, a.shape[0]), d, 0.0)
        sg = _sigmoid(gt + bv[:, h:])
        dz = jnp.concatenate([d * sg, d * (a + bv[:, :h]) * sg * (1.0 - sg)], axis=1)
        return [dz, jnp.sum(dz, axis=0, keepdims=True)]

    return _rowwise(fn, [(z, 0, h), (z, 1, h), _whole(dy)], [b], [2 * h], [(1, 2 * h)], name=name,
                    out_dtypes=[MXU_DTYPE])


def _ln_core(x):
    mu = jnp.mean(x, axis=-1, keepdims=True)
    xc = x - mu
    r = lax.rsqrt(jnp.mean(xc * xc, axis=-1, keepdims=True) + EPS)
    return xc * r, r


def _lnsilu_fwd(x, g, b, *, name):
    def fn(row0, xv, gv, bv):
        xh, _ = _ln_core(xv)
        v = xh * gv + bv
        return [v * _sigmoid(v)]

    return _rowwise(fn, [_whole(x)], [g, b], [x.shape[1]], [], name=name, out_dtypes=[MXU_DTYPE])[0]


def _lnsilu_bwd(x, g, b, dy, *, name):
    def fn(row0, xv, d, gv, bv):
        xh, r = _ln_core(xv)
        v = xh * gv + bv
        sg = _sigmoid(v)
        dv = d * sg * (1.0 + v * (1.0 - sg))
        dxh = dv * gv
        dx = r * (dxh - jnp.mean(dxh, axis=-1, keepdims=True) - xh * jnp.mean(dxh * xh, axis=-1, keepdims=True))
        return [dx, jnp.sum(dv * xh, axis=0, keepdims=True), jnp.sum(dv, axis=0, keepdims=True)]

    c = x.shape[1]
    return _rowwise(fn, [_whole(x), _whole(dy)], [g, b], [c], [(1, c), (1, c)], name=name)


def _silu_fwd(x, *, name):
    return _rowwise(lambda row0, v: [v * _sigmoid(v)], [_whole(x)], [], [x.shape[1]], [], name=name)[0]


def _silu_bwd(x, dy, *, name):
    def fn(row0, v, d):
        sg = _sigmoid(v)
        return [d * sg * (1.0 + v * (1.0 - sg))]

    return _rowwise(fn, [_whole(x), _whole(dy)], [], [x.shape[1]], [], name=name)[0]


def _gate_parts(row0, pg, gb):
    lane = lax.broadcasted_iota(jnp.int32, pg.shape, 1)
    th = jnp.tanh((pg + gb) / GATE_SOFTCAP)
    s = GATE_SOFTCAP * th
    return lane, th, s, _live(row0, pg.shape[0])


def _gates_fwd(proj, gate_b, *, name):
    def fn(row0, pg, gb):
        lane, th, s, live = _gate_parts(row0, pg, gb)
        li = jnp.where(live, s, NEG)
        lf = jnp.where(live, -_softplus(-s), 0.0)
        return [jnp.where(lane < MLSTM_HEADS, li, jnp.where(lane < 2 * MLSTM_HEADS, lf, 0.0))]

    return _rowwise(fn, [(proj, GATE_COL // LANE, LANE)], [gate_b], [LANE], [], name=name)[0]


def _gates_bwd(proj, gate_b, dgl, *, name):
    def fn(row0, pg, d, gb):
        lane, th, s, live = _gate_parts(row0, pg, gb)
        ds = jnp.where(lane < MLSTM_HEADS, d, d * _sigmoid(-s))
        ds = jnp.where(live & (lane < 2 * MLSTM_HEADS), ds, 0.0)
        dp = ds * (1.0 - th * th)
        return [dp, jnp.sum(dp, axis=0, keepdims=True)]

    return _rowwise(fn, [(proj, GATE_COL // LANE, LANE), _whole(dgl)], [gate_b], [LANE], [(1, LANE)], name=name)


def _head_rms(h):
    parts = [h[:, i * MLSTM_DV:(i + 1) * MLSTM_DV] for i in range(MLSTM_HEADS)]
    rs = [lax.rsqrt(jnp.mean(p * p, axis=-1, keepdims=True) + EPS) for p in parts]
    return parts, rs


def _hnorm_fwd(hm, proj, g, *, name):
    def fn(row0, h, o, gv):
        parts, rs = _head_rms(h)
        xh = jnp.concatenate([p * r for p, r in zip(parts, rs)], axis=1)
        return [xh * gv * _sigmoid(o)]

    return _rowwise(fn, [_whole(hm), (proj, 2, MV)], [g], [MV], [], name=name)[0]


def _hnorm_bwd(hm, proj, g, dmixed, *, name):
    def fn(row0, h, o, d, gv):
        parts, rs = _head_rms(h)
        so = _sigmoid(o)
        dn = d * so
        dxs, xhs = [], []
        for i, (p, r) in enumerate(zip(parts, rs)):
            sl = slice(i * MLSTM_DV, (i + 1) * MLSTM_DV)
            xh = p * r
            dyg = dn[:, sl] * gv[:, sl]
            dxs.append(r * (dyg - xh * jnp.mean(dyg * xh, axis=-1, keepdims=True)))
            xhs.append(xh)
        xh = jnp.concatenate(xhs, axis=1)
        return [jnp.concatenate(dxs, axis=1), d * xh * gv * so * (1.0 - so), jnp.sum(dn * xh, axis=0, keepdims=True)]

    return _rowwise(fn, [_whole(hm), (proj, 2, MV), (dmixed, 0, MV)], [g], [MV, MV], [(1, MV)], name=name)


def _loss_fwd_bwd(h, target, *, name):
    first = PAD_FRONT + N_META

    def fn(row0, hv, tv):
        rows = row0 + lax.broadcasted_iota(jnp.int32, (hv.shape[0], 1), 0)
        e = jnp.where(rows >= first, hv - tv, 0.0)
        return [e * (1.0 / D_MODEL), jnp.sum(e * e, axis=0, keepdims=True)]

    return _rowwise(fn, [_whole(h), _whole(target)], [], [D_MODEL], [(1, D_MODEL)], name=name)


CONV_SUB = 64


def _conv_tiles(tp, c):
    return _row_tile(tp), _divisor(c, (256, 128))


def _conv_shift(win, shifted, tm):
    n = tm + CONV_HALO - SUBLANE
    for b in range(1, SUBLANE):
        shifted[b - 1, :, :] = win[b:b + n, :]


def _conv_window(win, shifted, offset, r0, rows):
    a, b = divmod(offset, SUBLANE)
    lo = a * SUBLANE + r0
    return win[lo:lo + rows, :] if b == 0 else shifted[b - 1, lo:lo + rows, :]


def _conv_fwd(x, w, b, *, name):
    tp, (k, c) = x.shape[0], w.shape
    tm, tc = _conv_tiles(tp, c)
    base = CONV_HALO - (k - 1)

    def body(x_ref, xp_ref, w_ref, b_ref, o_ref, win, shifted):
        i = pl.program_id(1)
        win[0:CONV_HALO, :] = jnp.where(i > 0, _mxu_rounded(xp_ref[tm - CONV_HALO:tm, :]), 0.0)
        win[CONV_HALO:CONV_HALO + tm, :] = _mxu_rounded(x_ref[...])
        _conv_shift(win, shifted, tm)
        for r0 in range(0, tm, CONV_SUB):
            acc = jnp.broadcast_to(b_ref[...], (CONV_SUB, tc))
            for j in range(k):
                acc = acc + _mxu_rounded(w_ref[j:j + 1, :]) *_conv_window(win, shifted, base + j, r0, CONV_SUB)
            o_ref[r0:r0 + CONV_SUB, :] = acc

    return pl.pallas_call(
        body, name=name, grid=(c // tc, tp // tm),
        in_specs=[pl.BlockSpec((tm, tc), lambda cc, i: (i, cc)),
                  pl.BlockSpec((tm, tc), lambda cc, i: (jnp.maximum(i - 1, 0), cc)),
                  pl.BlockSpec((k, tc), lambda cc, i: (0, cc)),
                  pl.BlockSpec((1, tc), lambda cc, i: (0, cc))],
        out_specs=pl.BlockSpec((tm, tc), lambda cc, i: (i, cc)),
        out_shape=jax.ShapeDtypeStruct((tp, c), F32),
        scratch_shapes=[pltpu.VMEM((CONV_HALO + tm, tc), F32),
                        pltpu.VMEM((SUBLANE - 1, CONV_HALO + tm - SUBLANE, tc), F32)],
        compiler_params=_params(("parallel", "arbitrary")),
    )(x, x, w, b)


def _conv_bwd(x, w, dy, *, name):
    tp, (k, c) = x.shape[0], w.shape
    tm, tc = _conv_tiles(tp, c)
    nt = tp // tm
    base = CONV_HALO - (k - 1)

    def body(x_ref, xp_ref, d_ref, dn_ref, w_ref, dx_ref, dw_ref, db_ref, winx, wind, shx, shd):
        i = pl.program_id(1)
        winx[0:CONV_HALO, :] = jnp.where(i > 0, _mxu_rounded(xp_ref[tm - CONV_HALO:tm, :]), 0.0)
        winx[CONV_HALO:CONV_HALO + tm, :] = _mxu_rounded(x_ref[...])
        d = d_ref[...]
        wind[0:tm, :] = _mxu_rounded(d)
        wind[tm:tm + CONV_HALO, :] = jnp.where(i < nt - 1, _mxu_rounded(dn_ref[0:CONV_HALO, :]), 0.0)
        _conv_shift(winx, shx, tm)
        _conv_shift(wind, shd, tm)

        @pl.when(i == 0)
        def _():
            dw_ref[...] = jnp.zeros_like(dw_ref)
            db_ref[...] = jnp.zeros_like(db_ref)

        for r0 in range(0, tm, CONV_SUB):
            acc = jnp.zeros((CONV_SUB, tc), F32)
            for j in range(k):
                acc = acc + _mxu_rounded(w_ref[j:j + 1, :]) *_conv_window(wind, shd, k - 1 - j, r0, CONV_SUB)
            dx_ref[r0:r0 + CONV_SUB, :] = acc
        for j in range(k):
            part = jnp.zeros((SUBLANE, tc), F32)
            for r0 in range(0, tm, CONV_SUB):
                p = wind[r0:r0 + CONV_SUB, :] * _conv_window(winx, shx, base + j, r0, CONV_SUB)
                part = part + jnp.sum(p.reshape(CONV_SUB // SUBLANE, SUBLANE, tc), axis=0)
            dw_ref[j:j + 1, :] += jnp.sum(part, axis=0, keepdims=True)
        db_ref[...] += jnp.sum(d, axis=0, keepdims=True)

    return pl.pallas_call(
        body, name=name, grid=(c // tc, nt),
        in_specs=[pl.BlockSpec((tm, tc), lambda cc, i: (i, cc)),
                  pl.BlockSpec((tm, tc), lambda cc, i: (jnp.maximum(i - 1, 0), cc)),
                  pl.BlockSpec((tm, tc), lambda cc, i: (i, cc)),
                  pl.BlockSpec((tm, tc), lambda cc, i: (jnp.minimum(i + 1, nt - 1), cc)),
                  pl.BlockSpec((k, tc), lambda cc, i: (0, cc))],
        out_specs=[pl.BlockSpec((tm, tc), lambda cc, i: (i, cc)),
                   pl.BlockSpec((k, tc), lambda cc, i: (0, cc)),
                   pl.BlockSpec((1, tc), lambda cc, i: (0, cc))],
        out_shape=[jax.ShapeDtypeStruct((tp, c), F32), jax.ShapeDtypeStruct((k, c), F32),
                   jax.ShapeDtypeStruct((1, c), F32)],
        scratch_shapes=[pltpu.VMEM((CONV_HALO + tm, tc), F32), pltpu.VMEM((CONV_HALO + tm, tc), F32),
                        pltpu.VMEM((SUBLANE - 1, CONV_HALO + tm - SUBLANE, tc), F32),
                        pltpu.VMEM((SUBLANE - 1, CONV_HALO + tm - SUBLANE, tc), F32)],
        compiler_params=_params(("parallel", "arbitrary")),
    )(x, x, dy, dy, w)


def _chunk_masks():
    L = MLSTM_CHUNK
    r = lax.broadcasted_iota(jnp.int32, (L, L), 0)
    c = lax.broadcasted_iota(jnp.int32, (L, L), 1)
    return r == c, c <= r, r <= c


def _to_row(col, eye):
    return jnp.sum(jnp.where(eye, col, 0.0), axis=0, keepdims=True)


def _to_col(row, eye):
    return jnp.sum(jnp.where(eye, row, 0.0), axis=1, keepdims=True)


def _mxu_rounded(x):
    return x.astype(MXU_DTYPE).astype(F32)


def _mlstm_group(nc):
    return _divisor(nc, (5, 4, 3, 2, 1))


def _each(f, *lists):
    return [f(*args) for args in zip(*lists)]


def _rsum(x):
    return jnp.sum(x, axis=1, keepdims=True)


def _csum(x):
    return jnp.sum(x, axis=0, keepdims=True)


def _mlstm_chunk(q, k, v, li_c, lf_c, c_st, n_st, m_st, masks):
    eye, low, up = masks
    li_r = _each(lambda c: _to_row(c, eye), li_c)
    lf_r = _each(lambda c: _to_row(c, eye), lf_c)
    b_c = _each(lambda r: _rsum(jnp.where(low, r, 0.0)), lf_r)
    b_r = _each(lambda c: _csum(jnp.where(up, c, 0.0)), lf_c)
    g = _each(_csum, lf_c)
    dm = _each(lambda bc, br, lr: jnp.where(low, bc - br + lr, NEG), b_c, b_r, li_r)
    inter = _each(lambda bc, m: bc + m, b_c, m_st)
    mt = _each(lambda i, d: jnp.maximum(i, jnp.max(d, axis=1, keepdims=True)), inter, dm)
    wi = _each(lambda d, m: jnp.exp(d - m), dm, mt)
    wint = _each(lambda i, m: jnp.exp(i - m), inter, mt)
    qk_ = _each(_nt, q, k)
    qc = _each(_nn, q, c_st)
    s = _each(lambda a, w: a * w, qk_, wi)
    qn = _each(lambda a, n: _rsum(_mxu_rounded(a) * _mxu_rounded(n)), q, n_st)
    sv = _each(_nn, s, v)
    num = _each(lambda a, w, b: a + w * b, sv, wint, qc)
    den = _each(lambda a, w, b: _rsum(a) + w * b, s, wint, qn)
    floor = _each(lambda m: jnp.exp(-m), mt)
    a_c = _each(lambda gg, b, l: gg - b + l, g, b_c, li_c)
    a_r = _each(lambda gg, b, l: gg - b + l, g, b_r, li_r)
    mnew = _each(lambda gg, m, a: jnp.maximum(gg + m, jnp.max(a, axis=1, keepdims=True)), g, m_st, a_r)
    wa_c = _each(lambda a, m: jnp.exp(a - m), a_c, mnew)
    wc = _each(lambda gg, m, mn: jnp.exp(gg + m - mn), g, m_st, mnew)
    return dict(wi=wi, wint=wint, s=s, qc=qc, qn=qn, num=num, den=den, floor=floor, mnew=mnew, wa_c=wa_c, wc=wc)


def _mlstm_heads(qk_ref, v_ref, gl_ref, rows):
    H, dk, dv = MLSTM_HEADS, MLSTM_DQK, MLSTM_DV
    gates = gl_ref[rows, :]
    return ([qk_ref[rows, h * dk:(h + 1) * dk] * (dk ** -0.5) for h in range(H)],
            [qk_ref[rows, MQK + h * dk:MQK + (h + 1) * dk] for h in range(H)],
            [v_ref[rows, h * dv:(h + 1) * dv] for h in range(H)],
            [gates[:, h:h + 1] for h in range(H)], [gates[:, H + h:H + h + 1] for h in range(H)])


def _mlstm_fwd(qk, proj, gl, *, name):
    tp = qk.shape[0]
    L, H, dk, dv = MLSTM_CHUNK, MLSTM_HEADS, MLSTM_DQK, MLSTM_DV
    nc = tp // L
    G = _mlstm_group(nc)

    def body(qk_ref, v_ref, gl_ref, h_ref, call_ref, nall_ref, mall_ref, c_s, n_s, m_s):
        @pl.when(pl.program_id(0) == 0)
        def _():
            c_s[...] = jnp.zeros_like(c_s)
            n_s[...] = jnp.zeros_like(n_s)
            m_s[...] = jnp.zeros_like(m_s)

        masks = _chunk_masks()
        heads = range(H)
        c_st, n_st, m_row = [c_s[h] for h in heads], [n_s[h] for h in heads], [m_s[h] for h in heads]
        for ci in range(G):
            rows = slice(ci * L, (ci + 1) * L)
            for h in heads:
                call_ref[ci, h], nall_ref[ci, h], mall_ref[ci, h] = c_st[h], n_st[h], m_row[h]
            q, k, v, li, lf = _mlstm_heads(qk_ref, v_ref, gl_ref, rows)
            f = _mlstm_chunk(q, k, v, li, lf, c_st, n_st, [m[:, 0:1] for m in m_row], masks)
            out = _each(lambda a, b, c: a / jnp.maximum(jnp.abs(b), c), f["num"], f["den"], f["floor"])
            for h in heads:
                h_ref[rows, h * dv:(h + 1) * dv] = out[h]
            kv = _each(_tn, _each(lambda a, w: a * w, k, f["wa_c"]), v)
            c_st = _each(lambda w, c, x: w * c + x, f["wc"], c_st, kv)
            n_st = _each(lambda w, n, a, b: w * n + _csum(_mxu_rounded(a) * _mxu_rounded(b)), f["wc"], n_st, k, f["wa_c"])
            m_row = _each(lambda m: jnp.broadcast_to(m, (1, LANE)), f["mnew"])
        for h in heads:
            c_s[h], n_s[h], m_s[h] = c_st[h], n_st[h], m_row[h]

    return pl.pallas_call(
        body, name=name, grid=(nc // G,),
        in_specs=[pl.BlockSpec((G * L, 2 * MQK), lambda i: (i, 0)),
                  pl.BlockSpec((G * L, MV), lambda i: (i, 1)),
                  pl.BlockSpec((G * L, LANE), lambda i: (i, 0))],
        out_specs=[pl.BlockSpec((G * L, MV), lambda i: (i, 0)),
                   pl.BlockSpec((G, H, dk, dv), lambda i: (i, 0, 0, 0)),
                   pl.BlockSpec((G, H, 1, dk), lambda i: (i, 0, 0, 0)),
                   pl.BlockSpec((G, H, 1, LANE), lambda i: (i, 0, 0, 0))],
        out_shape=[jax.ShapeDtypeStruct((tp, MV), F32),
                   jax.ShapeDtypeStruct((nc, H, dk, dv), F32),
                   jax.ShapeDtypeStruct((nc, H, 1, dk), F32),
                   jax.ShapeDtypeStruct((nc, H, 1, LANE), F32)],
        scratch_shapes=[pltpu.VMEM((H, dk, dv), F32), pltpu.VMEM((H, 1, dk), F32), pltpu.VMEM((H, 1, LANE), F32)],
        compiler_params=_params(("arbitrary",)),
    )(qk, proj, gl)


def _mlstm_bwd(qk, proj, gl, dmix, call, nall, mall, *, name):
    tp = qk.shape[0]
    L, H, dk, dv = MLSTM_CHUNK, MLSTM_HEADS, MLSTM_DQK, MLSTM_DV
    nc = tp // L
    G = _mlstm_group(nc)

    def body(qk_ref, v_ref, gl_ref, dh_ref, call_ref, nall_ref, mall_ref, dqk_ref, dv_ref, dgl_ref, dc_s, dn_s):
        @pl.when(pl.program_id(0) == 0)
        def _():
            dc_s[...] = jnp.zeros_like(dc_s)
            dn_s[...] = jnp.zeros_like(dn_s)

        masks = _chunk_masks()
        eye, low, up = masks
        lane = lax.broadcasted_iota(jnp.int32, (L, LANE), 1)
        heads = range(H)
        dcn, dnn = [dc_s[h] for h in heads], [dn_s[h] for h in heads]
        for ci in reversed(range(G)):
            rows = slice(ci * L, (ci + 1) * L)
            c_st = [call_ref[ci, h] for h in heads]
            n_st = [nall_ref[ci, h] for h in heads]
            m_st = [mall_ref[ci, h][:, 0:1] for h in heads]
            q, k, v, li, lf = _mlstm_heads(qk_ref, v_ref, gl_ref, rows)
            dh = [dh_ref[rows, h * dv:(h + 1) * dv] for h in heads]
            f = _mlstm_chunk(q, k, v, li, lf, c_st, n_st, m_st, masks)
            wint, s, wa_c, wc, den, floor = f["wint"], f["s"], f["wa_c"], f["wc"], f["den"], f["floor"]

            r = _each(lambda a, b: 1.0 / jnp.maximum(jnp.abs(a), b), den, floor)
            dnum = _each(lambda a, b: a * b, dh, r)
            dscale = _each(lambda a, b, c: -_rsum(a * b) * c * c, dh, f["num"], r)
            dden = _each(lambda a, b, c: jnp.where(jnp.abs(a) > b, c * jnp.sign(a), 0.0), den, floor, dscale)
            dnv = _each(_nt, dnum, v)
            wd = _each(lambda a, b: a * b, wint, dnum)
            dq_c = _each(_nt, wd, c_st)
            dc_acc = _each(_tn, q, wd)
            dv_s = _each(_tn, s, dnum)
            kd = _each(_nn, k, dcn)
            vd = _each(_nt, v, dcn)
            ds = _each(lambda a, b: a + b, dnv, dden)
            dwint = _each(lambda a, b, c, d: _rsum(a * b) + c * d, dnum, f["qc"], dden, f["qn"])
            dd = _each(lambda a, b: a * b, ds, s)
            da_mat = _each(lambda a, b: a * b, ds, f["wi"])
            dq_a = _each(_nn, da_mat, k)
            dk_a = _each(_tn, da_mat, q)
            dw_n = _each(lambda a, b: a * b, dden, wint)
            dq = _each(lambda a, b, c, d: a + b * c + d, dq_c, dw_n, n_st, dq_a)
            dn_acc = _each(lambda a, b: _csum(a * b), q, dw_n)
            dk_ = _each(lambda a, w, b, c: a + w * (b + c), dk_a, wa_c, vd, dnn)
            dv_ = _each(lambda a, w, b: a + w * b, dv_s, wa_c, kd)
            dwa = _each(lambda a, b, c, d: _rsum(a * b) + _rsum(c * d), kd, v, k, dnn)
            dwc = _each(lambda a, b, c, d: _csum(_rsum(a * b)) + _rsum(c * d), dcn, c_st, dnn, n_st)
            da_c = _each(lambda a, b: a * b, dwa, wa_c)
            dg = _each(lambda a, b, c: _csum(a) + b * c, da_c, dwc, wc)
            dd_cols = _each(_csum, dd)
            db_c = _each(lambda a, b, c, d: a * b + _rsum(c) - d, dwint, wint, dd, da_c)
            db_r = _each(lambda a, b: _to_row(a, eye) - b, db_c, dd_cols)
            dlf = _each(lambda a, b: _rsum(jnp.where(up, a, 0.0)) + b, db_r, dg)
            dli = _each(lambda a, b: a + _to_col(b, eye), da_c, dd_cols)
            dcn = _each(lambda w, a, b: w * a + b, wc, dcn, dc_acc)
            dnn = _each(lambda w, a, b: w * a + b, wc, dnn, dn_acc)

            dgl = jnp.zeros((L, LANE), F32)
            for h in heads:
                dqk_ref[rows, h * dk:(h + 1) * dk] = dq[h] * (dk ** -0.5)
                dqk_ref[rows, MQK + h * dk:MQK + (h + 1) * dk] = dk_[h]
                dv_ref[rows, h * dv:(h + 1) * dv] = dv_[h]
                dgl = dgl + jnp.where(lane == h, dli[h], 0.0) + jnp.where(lane == H + h, dlf[h], 0.0)
            dgl_ref[rows, :] = dgl
        for h in heads:
            dc_s[h], dn_s[h] = dcn[h], dnn[h]

    rev = lambda i: nc // G - 1 - i
    return pl.pallas_call(
        body, name=name, grid=(nc // G,),
        in_specs=[pl.BlockSpec((G * L, 2 * MQK), lambda i: (rev(i), 0)),
                  pl.BlockSpec((G * L, MV), lambda i: (rev(i), 1)),
                  pl.BlockSpec((G * L, LANE), lambda i: (rev(i), 0)),
                  pl.BlockSpec((G * L, MV), lambda i: (rev(i), 0)),
                  pl.BlockSpec((G, H, dk, dv), lambda i: (rev(i), 0, 0, 0)),
                  pl.BlockSpec((G, H, 1, dk), lambda i: (rev(i), 0, 0, 0)),
                  pl.BlockSpec((G, H, 1, LANE), lambda i: (rev(i), 0, 0, 0))],
        out_specs=[pl.BlockSpec((G * L, 2 * MQK), lambda i: (rev(i), 0)),
                   pl.BlockSpec((G * L, MV), lambda i: (rev(i), 0)),
                   pl.BlockSpec((G * L, LANE), lambda i: (rev(i), 0))],
        out_shape=[jax.ShapeDtypeStruct((tp, 2 * MQK), F32), jax.ShapeDtypeStruct((tp, MV), F32),
                   jax.ShapeDtypeStruct((tp, LANE), F32)],
        scratch_shapes=[pltpu.VMEM((H, dk, dv), F32), pltpu.VMEM((H, 1, dk), F32)],
        compiler_params=_params(("arbitrary",)),
    )(qk, proj, gl, dmix, call, nall, mall)


SB_EXP_CAP = 80.0
SB_Q0 = 3 * MV // LANE
SB_K0 = SB_Q0 + SB_HEADS
SB_V0 = SB_K0 + SB_HEADS


def _cumsum_dot(x, tri):
    hi = x.astype(jnp.bfloat16)
    lo = (x - hi.astype(F32)).astype(jnp.bfloat16)
    dims = (((1,), (0,)), ((), ()))
    return (lax.dot_general(hi, tri, dims, preferred_element_type=F32)
            + lax.dot_general(lo, tri, dims, preferred_element_type=F32))


def _sb_query_blocks(nq):
    return _divisor(nq, (5, 4, 3, 2, 1))


def _sb_mask(tile, g, tq):
    t_idx = tile * tq + lax.broadcasted_iota(jnp.int32, (tq, tq), 0)
    s_idx = g * tq + lax.broadcasted_iota(jnp.int32, (tq, tq), 1)
    return (s_idx < t_idx) & (s_idx >= PAD_FRONT)


def _sb_blocks(x):
    return [x[:, k * SB_BLOCK:(k + 1) * SB_BLOCK] for k in range(x.shape[1] // SB_BLOCK)]


def _sb_logits(qb, kg, tri, mask):
    z = _nt(qb, kg) * (SB_DH ** -0.5)
    zc = jnp.minimum(z, SB_EXP_CAP)
    l = (zc - z) - jnp.log(1.0 + jnp.exp(zc))
    if mask is not None:
        l = jnp.where(mask, l, 0.0)
    return z, l, [_cumsum_dot(b, tri) for b in _sb_blocks(l)]


def _sb_weights(z, withins, runs, mask):
    e = jnp.exp(z + jnp.concatenate([w + r for w, r in zip(withins, runs)], axis=1))
    return e if mask is None else jnp.where(mask, e, 0.0)


def _sb_segments(tile, group, descending):
    def diagonal():
        group(tile, True)

    def interior():
        def it(gg, c):
            group(tile - 1 - gg if descending else 1 + gg, False)
            return c
        lax.fori_loop(0, jnp.maximum(tile - 1, 0), it, 0)

    def first():
        @pl.when(tile > 0)
        def _():
            group(0, True)

    for part in ((diagonal, interior, first) if descending else (first, interior, diagonal)):
        part()


def _sb_fwd(proj, *, name):
    tp = proj.shape[0]
    B, H = SB_BLOCK, SB_HEADS
    nq = tp // B
    assert nq <= LANE and B == LANE
    r = _sb_query_blocks(nq)
    tq = r * B

    def body(q_ref, k_ref, v_ref, o_ref, ac_ref, run_s):
        tile = pl.program_id(1)
        qb = q_ref[...].astype(MXU_DTYPE)
        lane = lax.broadcasted_iota(jnp.int32, (tq, LANE), 1)
        tri = (lax.broadcasted_iota(jnp.int32, (B, B), 0) >= lax.broadcasted_iota(jnp.int32, (B, B), 1)
               ).astype(jnp.bfloat16)
        o_ref[...] = jnp.zeros_like(o_ref)
        ac_ref[0, 0] = jnp.zeros((tq, LANE), F32)
        run_s[...] = jnp.zeros_like(run_s)

        def group(g, masked):
            grows = pl.ds(pl.multiple_of(g * tq, tq), tq)
            mask = _sb_mask(tile, g, tq) if masked else None
            z, l, withins = _sb_logits(qb, k_ref[grows, :], tri, mask)
            run, saved, runs = run_s[...], ac_ref[0, 0], [None] * r
            for k in reversed(range(r)):
                runs[k] = run
                saved = jnp.where(lane == g * r + k, run, saved)
                run = run + withins[k][:, 0:1]
            o_ref[...] += _nn(_sb_weights(z, withins, runs, mask), v_ref[grows, :])
            ac_ref[0, 0] = saved
            run_s[...] = run

        _sb_segments(tile, group, descending=True)

    return pl.pallas_call(
        body, name=name, grid=(H, nq // r), scratch_shapes=[pltpu.VMEM((tq, LANE), F32)],
        in_specs=[pl.BlockSpec((tq, SB_DH), lambda h, i: (i, SB_Q0 + h)),
                  pl.BlockSpec((tp, SB_DH), lambda h, i: (0, SB_K0 + h)),
                  pl.BlockSpec((tp, SB_DH), lambda h, i: (0, SB_V0 + h))],
        out_specs=[pl.BlockSpec((tq, SB_DH), lambda h, i: (i, h)),
                   pl.BlockSpec((1, 1, tq, LANE), lambda h, i: (h, i, 0, 0))],
        out_shape=[jax.ShapeDtypeStruct((tp, SBW), F32), jax.ShapeDtypeStruct((H, nq // r, tq, LANE), F32)],
        compiler_params=_params(("parallel", "arbitrary")),
    )(proj, proj, proj)


def _sb_bwd(proj, across, dmix, *, name):
    tp = proj.shape[0]
    B, H = SB_BLOCK, SB_HEADS
    nq = tp // B
    r = _sb_query_blocks(nq)
    tq = r * B
    do0 = MV // LANE

    def body(q_ref, k_ref, v_ref, ac_ref, do_ref, dq_ref, dk_ref, dv_ref, gpre_s):
        tile = pl.program_id(1)

        @pl.when(tile == 0)
        def _():
            dk_ref[...] = jnp.zeros_like(dk_ref)
            dv_ref[...] = jnp.zeros_like(dv_ref)

        dq_ref[...] = jnp.zeros_like(dq_ref)
        gpre_s[...] = jnp.zeros_like(gpre_s)
        qb, dob = q_ref[...].astype(MXU_DTYPE), do_ref[...].astype(MXU_DTYPE)
        lane = lax.broadcasted_iota(jnp.int32, (tq, LANE), 1)
        rr = lax.broadcasted_iota(jnp.int32, (B, B), 0)
        cc = lax.broadcasted_iota(jnp.int32, (B, B), 1)
        tri = (rr >= cc).astype(jnp.bfloat16)
        prefix = (rr <= cc).astype(jnp.bfloat16)
        scale = SB_DH ** -0.5

        def group(g, masked):
            grows = pl.ds(pl.multiple_of(g * tq, tq), tq)
            kg, vg = k_ref[grows, :], v_ref[grows, :]
            mask = _sb_mask(tile, g, tq) if masked else None
            z, l, withins = _sb_logits(qb, kg, tri, mask)
            saved = ac_ref[0, 0]
            runs = [jnp.sum(jnp.where(lane == g * r + k, saved, 0.0), axis=1, keepdims=True) for k in range(r)]
            w = _sb_weights(z, withins, runs, mask)
            dv_ref[grows, :] += _tn(w, dob)
            gw = _nt(dob, vg) * w
            gpre, gcum = gpre_s[...], []
            for gc in [_cumsum_dot(b, prefix) for b in _sb_blocks(gw)]:
                gcum.append(gc + gpre)
                gpre = gpre + gc[:, B - 1:B]
            beta_g = jnp.exp(z + l) * jnp.concatenate(gcum, axis=1)
            if masked:
                beta_g = jnp.where(mask, beta_g, 0.0)
            dz = ((gw - beta_g) * scale).astype(MXU_DTYPE)
            dk_ref[grows, :] += _tn(dz, qb)
            dq_ref[...] += _nn(dz, kg)
            gpre_s[...] = gpre

        _sb_segments(tile, group, descending=False)

    return pl.pallas_call(
        body, name=name, grid=(H, nq // r), scratch_shapes=[pltpu.VMEM((tq, LANE), F32)],
        in_specs=[pl.BlockSpec((tq, SB_DH), lambda h, i: (i, SB_Q0 + h)),
                  pl.BlockSpec((tp, SB_DH), lambda h, i: (0, SB_K0 + h)),
                  pl.BlockSpec((tp, SB_DH), lambda h, i: (0, SB_V0 + h)),
                  pl.BlockSpec((1, 1, tq, LANE), lambda h, i: (h, i, 0, 0)),
                  pl.BlockSpec((tq, SB_DH), lambda h, i: (i, do0 + h))],
        out_specs=[pl.BlockSpec((tq, SB_DH), lambda h, i: (i, h)),
                   pl.BlockSpec((tp, SB_DH), lambda h, i: (0, h)),
                   pl.BlockSpec((tp, SB_DH), lambda h, i: (0, h))],
        out_shape=[jax.ShapeDtypeStruct((tp, SBW), F32)] * 3,
        compiler_params=_params(("parallel", "arbitrary")),
    )(proj, proj, proj, across, dmix)


def _ffn_forward(h, p, tag):
    u = _rms_fwd(h, p["g2"], out_dtype=MXU_DTYPE, name=f"{tag}_ffn_norm")
    a, b, s = _ffn_in(u, p["w_gate"], p["w_up"], name=f"{tag}_ffn_gate_up")
    f = _matmul(s, p["w_down"], name=f"{tag}_ffn_down")
    out = _rms_fwd(f, p["g3"], res=h, name=f"{tag}_ffn_out")
    return out, dict(h=h, u=u, a=a, b=b, s=s, f=f)


def _ffn_backward(dh, p, a, tag):
    df, dg3 = _rms_bwd(a["f"], p["g3"], dh, out_dtype=MXU_DTYPE, name=f"{tag}_ffn_out_bwd")
    da, db = _ffn_down_dx(df, p["w_down"], a["a"], a["b"], name=f"{tag}_ffn_down_dx")
    dw_down = _matmul(a["s"], df, ta=True, name=f"{tag}_ffn_down_dw")
    du_gate = _matmul(da, p["w_gate"], tb=True, name=f"{tag}_ffn_gate_dx")
    du_up = _matmul(db, p["w_up"], tb=True, name=f"{tag}_ffn_up_dx")
    dw_gate = _matmul(a["u"], da, ta=True, name=f"{tag}_ffn_gate_dw")
    dw_up = _matmul(a["u"], db, ta=True, name=f"{tag}_ffn_up_dw")
    dh_in, dg2 = _rms_bwd(a["h"], p["g2"], du_gate, dy2=du_up, add=dh, name=f"{tag}_ffn_norm_bwd")
    return dh_in, dict(g2=dg2, g3=dg3, w_gate=dw_gate, w_up=dw_up, w_down=dw_down)


def _mixer_forward(h, p, tag):
    u = _rms_fwd(h, p["g0"], out_dtype=MXU_DTYPE, name=f"{tag}_mix_norm")
    proj = _matmul(u, p["w_in"], name=f"{tag}_mix_in")
    qc = _conv_fwd(proj, p["qk_w"], p["qk_b"], name=f"{tag}_mix_qkconv")
    qk = _silu_fwd(qc, name=f"{tag}_mix_qkact")
    gl = _gates_fwd(proj, p["gate_b"], name=f"{tag}_mix_gates")
    hm, call, nall, mall = _mlstm_fwd(qk, proj, gl, name=f"{tag}_mlstm")
    hn = _hnorm_fwd(hm, proj, p["hnorm_g"], name=f"{tag}_mix_hnorm")
    hs, across = _sb_fwd(proj, name=f"{tag}_sb")
    mixed = jnp.concatenate([hn, hs], axis=1).astype(MXU_DTYPE)
    y = _matmul(mixed, p["w_out"], name=f"{tag}_mix_out")
    out = _rms_fwd(y, p["g1"], res=h, name=f"{tag}_mix_res")
    return out, dict(h=h, u=u, proj=proj, qc=qc, qk=qk, gl=gl, hm=hm, call=call, nall=nall, mall=mall,
                     across=across, mixed=mixed, y=y)


def _mixer_backward(dh, p, a, tag):
    tp = dh.shape[0]
    dy, dg1 = _rms_bwd(a["y"], p["g1"], dh, out_dtype=MXU_DTYPE, name=f"{tag}_mix_res_bwd")
    dmixed = _matmul(dy, p["w_out"], tb=True, name=f"{tag}_mix_out_dx")
    dw_out = _matmul(a["mixed"], dy, ta=True, name=f"{tag}_mix_out_dw")
    dsq, dsk, dsv = _sb_bwd(a["proj"], a["across"], dmixed, name=f"{tag}_sb_bwd")
    dhm, do, dhg = _hnorm_bwd(a["hm"], a["proj"], p["hnorm_g"], dmixed, name=f"{tag}_mix_hnorm_bwd")
    dqk, dv, dgl = _mlstm_bwd(a["qk"], a["proj"], a["gl"], dhm, a["call"], a["nall"], a["mall"],
                              name=f"{tag}_mlstm_bwd")
    dpg, dgate_b = _gates_bwd(a["proj"], p["gate_b"], dgl, name=f"{tag}_mix_gates_bwd")
    dqc = _silu_bwd(a["qc"], dqk, name=f"{tag}_mix_qkact_bwd")
    dpqk, dqk_w, dqk_b = _conv_bwd(a["proj"], p["qk_w"], dqc, name=f"{tag}_mix_qkconv_bwd")
    dproj = jnp.concatenate(
        [dpqk, dv, do, dsq, dsk, dsv, dpg, jnp.zeros((tp, PROJ_WIDTH - GATE_COL - LANE), F32)], axis=1
    ).astype(MXU_DTYPE)
    du = _matmul(dproj, p["w_in"], tb=True, name=f"{tag}_mix_in_dx")
    dw_in = _matmul(a["u"], dproj, ta=True, name=f"{tag}_mix_in_dw")
    dh_in, dg0 = _rms_bwd(a["h"], p["g0"], du, add=dh, name=f"{tag}_mix_norm_bwd")
    return dh_in, dict(g0=dg0, g1=dg1, w_in=dw_in, qk_w=dqk_w, qk_b=dqk_b, gate_b=dgate_b, hnorm_g=dhg,
                       w_out=dw_out)


def _conformer_forward(h, p, tag):
    u = _rms_fwd(h, p["g0"], out_dtype=MXU_DTYPE, name=f"{tag}_conf_norm")
    z = _matmul(u, p["w_pw1"], name=f"{tag}_conf_pw1")
    y1 = _glu_fwd(z, p["b_pw1"], name=f"{tag}_conf_glu")
    y2 = _conv_fwd(y1, p["w_dw"], p["b_dw"], name=f"{tag}_conf_dw")
    y3 = _lnsilu_fwd(y2, p["ln_g"], p["ln_b"], name=f"{tag}_conf_ln")
    y4 = _matmul(y3, p["w_pw2"], name=f"{tag}_conf_pw2")
    out = _rms_fwd(y4, p["g1"], res=h, bias=p["b_pw2"], name=f"{tag}_conf_res")
    return out, dict(h=h, u=u, z=z, y1=y1, y2=y2, y3=y3, y4=y4)


def _conformer_backward(dh, p, a, tag):
    dy4, dg1, db_pw2 = _rms_bwd(a["y4"], p["g1"], dh, bias=p["b_pw2"], out_dtype=MXU_DTYPE,
                                name=f"{tag}_conf_res_bwd")
    dy3 = _matmul(dy4, p["w_pw2"], tb=True, name=f"{tag}_conf_pw2_dx")
    dw_pw2 = _matmul(a["y3"], dy4, ta=True, name=f"{tag}_conf_pw2_dw")
    dy2, dln_g, dln_b = _lnsilu_bwd(a["y2"], p["ln_g"], p["ln_b"], dy3, name=f"{tag}_conf_ln_bwd")
    dy1, dw_dw, db_dw = _conv_bwd(a["y1"], p["w_dw"], dy2, name=f"{tag}_conf_dw_bwd")
    dz, db_pw1 = _glu_bwd(a["z"], p["b_pw1"], dy1, name=f"{tag}_conf_glu_bwd")
    du = _matmul(dz, p["w_pw1"], tb=True, name=f"{tag}_conf_pw1_dx")
    dw_pw1 = _matmul(a["u"], dz, ta=True, name=f"{tag}_conf_pw1_dw")
    dh_in, dg0 = _rms_bwd(a["h"], p["g0"], du, add=dh, name=f"{tag}_conf_norm_bwd")
    return dh_in, dict(g0=dg0, g1=dg1, w_pw1=dw_pw1, b_pw1=db_pw1, w_dw=dw_dw, b_dw=db_dw, ln_g=dln_g,
                       ln_b=dln_b, w_pw2=dw_pw2, b_pw2=db_pw2)


def _trunk_step(h0, target, layers):
    acts = []
    h = h0
    for li, p in enumerate(layers):
        tag = f"l{li}"
        h, a_mix = (_mixer_forward if li % 2 == 0 else _conformer_forward)(h, p["mix"], tag)
        h, a_ffn = _ffn_forward(h, p["ffn"], tag)
        acts.append((a_mix, a_ffn))
    dh, loss_cols = _loss_fwd_bwd(h, target, name="loss")
    grads = [None] * len(layers)
    for li in reversed(range(len(layers))):
        tag = f"l{li}"
        p = layers[li]
        dh, g_ffn = _ffn_backward(dh, p["ffn"], acts[li][1], tag)
        dh, g_mix = (_mixer_backward if li % 2 == 0 else _conformer_backward)(dh, p["mix"], acts[li][0], tag)
        grads[li] = dict(mix=g_mix, ffn=g_ffn)
    return loss_cols, dh, grads


_SPLIT = 2 * MQK + 2 * MV


def _prepare_layers(w):
    layers = []
    row = lambda v: v[None, :].astype(F32)
    for li in range(DEPTH):
        i = li // 2
        g = w["norm_g"][li].astype(F32)
        if li % 2 == 0:
            win = w["mix_w_in"][i]
            w_in = jnp.concatenate(
                [win[:, :_SPLIT], win[:, _SPLIT + 2 * MLSTM_HEADS:], win[:, _SPLIT:_SPLIT + 2 * MLSTM_HEADS],
                 jnp.zeros((D_MODEL, PROJ_WIDTH - IN_WIDTH), win.dtype)], axis=1)
            gate_b = jnp.pad(row(w["mix_gate_b"][i]), ((0, 0), (0, LANE - 2 * MLSTM_HEADS)))
            mix = dict(g0=g[0:1], g1=g[1:2], w_in=w_in, qk_w=w["mix_qk_conv_w"][i].astype(F32),
                       qk_b=row(w["mix_qk_conv_b"][i]), gate_b=gate_b, hnorm_g=row(w["mix_hnorm_g"][i]),
                       w_out=w["mix_w_out"][i])
        else:
            mix = dict(g0=g[0:1], g1=g[1:2], w_pw1=w["conv_w_pw1"][i], b_pw1=row(w["conv_b_pw1"][i]),
                       w_dw=w["conv_w_dw"][i].astype(F32), b_dw=row(w["conv_b_dw"][i]),
                       ln_g=row(w["conv_ln_g"][i]), ln_b=row(w["conv_ln_b"][i]), w_pw2=w["conv_w_pw2"][i],
                       b_pw2=row(w["conv_b_pw2"][i]))
        ffn = dict(g2=g[2:3], g3=g[3:4], w_gate=w["ffn_w_gate"][li], w_up=w["ffn_w_up"][li],
                   w_down=w["ffn_w_down"][li])
        layers.append(dict(mix=mix, ffn=ffn))
    return layers


def _collect_grads(grads):
    even = [grads[li]["mix"] for li in range(0, DEPTH, 2)]
    odd = [grads[li]["mix"] for li in range(1, DEPTH, 2)]
    ffn = [grads[li]["ffn"] for li in range(DEPTH)]
    st = lambda xs: jnp.stack(xs, axis=0)
    vec = lambda xs, k: st([x[k][0] for x in xs])
    out = {}
    out["norm_g"] = st([jnp.concatenate([grads[li]["mix"]["g0"], grads[li]["mix"]["g1"], grads[li]["ffn"]["g2"],
                                         grads[li]["ffn"]["g3"]], axis=0) for li in range(DEPTH)])
    out["mix_w_in"] = st([jnp.concatenate(
        [g["w_in"][:, :_SPLIT], g["w_in"][:, GATE_COL:GATE_COL + 2 * MLSTM_HEADS], g["w_in"][:, _SPLIT:GATE_COL]],
        axis=1) for g in even])
    out["mix_qk_conv_w"] = st([g["qk_w"] for g in even])
    out["mix_qk_conv_b"] = vec(even, "qk_b")
    out["mix_gate_b"] = st([g["gate_b"][0, :2 * MLSTM_HEADS] for g in even])
    out["mix_hnorm_g"] = vec(even, "hnorm_g")
    out["mix_w_out"] = st([g["w_out"] for g in even])
    out["conv_w_pw1"] = st([g["w_pw1"] for g in odd])
    out["conv_b_pw1"] = vec(odd, "b_pw1")
    out["conv_w_dw"] = st([g["w_dw"] for g in odd])
    out["conv_b_dw"] = vec(odd, "b_dw")
    out["conv_ln_g"] = vec(odd, "ln_g")
    out["conv_ln_b"] = vec(odd, "ln_b")
    out["conv_w_pw2"] = st([g["w_pw2"] for g in odd])
    out["conv_b_pw2"] = vec(odd, "b_pw2")
    out["ffn_w_gate"] = st([g["w_gate"] for g in ffn])
    out["ffn_w_up"] = st([g["w_up"] for g in ffn])
    out["ffn_w_down"] = st([g["w_down"] for g in ffn])
    return out


def _local_step(x, target, w):
    seq = x.shape[0]
    h0 = jnp.concatenate([jnp.zeros((PAD_FRONT, D_MODEL), F32), w["meta"].astype(F32), x], axis=0)
    tgt = jnp.concatenate([jnp.zeros((PAD_FRONT + N_META, D_MODEL), F32), target], axis=0)
    loss_cols, dh0, grads = _trunk_step(h0, tgt, _prepare_layers(w))
    out = _collect_grads(grads)
    out["meta"] = dh0[PAD_FRONT:PAD_FRONT + N_META]
    loss = 0.5 * jnp.sum(loss_cols) / D_MODEL
    return loss, dh0[PAD_FRONT + N_META:PAD_FRONT + N_META + seq], out


def _elementwise(fn, arrays, out_dtypes, *, name):
    shape = arrays[0].shape
    cols = shape[-1]
    rows = 1
    for s in shape[:-1]:
        rows *= s
    flat = [a.reshape(rows, cols) for a in arrays]
    if rows * cols * 4 <= (1 << 20) or rows % SUBLANE:
        tr = rows
    else:
        tr = _divisor(rows, (512, 256, 128, 64, 32, 16, 8))
    n = len(flat)

    def body(*refs):
        outs = fn(*[r[...] for r in refs[:n]])
        for o_ref, o in zip(refs[n:], outs):
            o_ref[...] = o.astype(o_ref.dtype)

    spec = pl.BlockSpec((tr, cols), lambda i: (i, 0))
    outs = pl.pallas_call(
        body, name=name, grid=(rows // tr,), in_specs=[spec] * n, out_specs=[spec] * len(out_dtypes),
        out_shape=[jax.ShapeDtypeStruct((rows, cols), dt) for dt in out_dtypes],
        compiler_params=_params(("parallel",)),
    )(*flat)
    return [o.reshape(shape) for o in outs]


def _adamw(w, g, m, v, *, name):
    def fn(wv, gv, mv, vv):
        mn = ADAM_B1 * mv + (1.0 - ADAM_B1) * gv
        vn = ADAM_B2 * vv + (1.0 - ADAM_B2) * (gv * gv)
        m_hat = mn / (1.0 - ADAM_B1 ** ADAM_STEP)
        v_hat = vn / (1.0 - ADAM_B2 ** ADAM_STEP)
        return [-ADAM_LR * (m_hat / (jnp.sqrt(v_hat) + ADAM_EPS) + ADAM_WD * wv), mn, vn]

    return _elementwise(fn, [w, g, m, v], [F32, F32, F32], name=name)


MESH_ID = pl.DeviceIdType.MESH
ANY = pl.BlockSpec(memory_space=pl.ANY)


def _place():
    x, y, c = lax.axis_index("x"), lax.axis_index("y"), lax.axis_index("c")
    return x, y, c, [(1 - x, y), (x, 1 - y), (1 - x, 1 - y)]


def _remote(src, dst, send_sems, recv_sems, k, to):
    return pltpu.make_async_remote_copy(src_ref=src, dst_ref=dst, send_sem=send_sems.at[k], recv_sem=recv_sems.at[k],
                                        device_id=to, device_id_type=MESH_ID)


def _comm_call(body, arrays, out_shapes, n_remote, n_local, name):
    return pl.pallas_call(
        body, name=name, in_specs=[ANY] * len(arrays), out_specs=[ANY] * len(out_shapes), out_shape=out_shapes,
        scratch_shapes=[pltpu.SemaphoreType.DMA((n_remote,)), pltpu.SemaphoreType.DMA((n_remote,)),
                        pltpu.SemaphoreType.DMA((n_local,))],
        compiler_params=pltpu.CompilerParams(has_side_effects=True),
    )(*arrays)


def _gather_chips(shards, *, name):
    n = len(shards)

    def body(*refs):
        ins, outs = refs[:n], refs[n:2 * n]
        send_sems, recv_sems, local_sems = refs[2 * n:]
        x, y, c, chips = _place()
        me, sibling = 2 * x + y, (x, y, 1 - c)

        def half(a, slot, hc):
            hl = ins[a].shape[0] // 2
            return outs[a].at[slot].at[pl.ds(hc * hl, hl)]

        def mine(a):
            hl = ins[a].shape[0] // 2
            return ins[a].at[pl.ds(c * hl, hl)]

        sent = []
        for a in range(n):
            for j, (px, py) in enumerate(chips):
                sent.append(_remote(mine(a), half(a, me, c), send_sems, recv_sems, 6 * a + j, (px, py, c)))
                sent[-1].start()
        for a in range(n):
            for j, (px, py) in enumerate(chips):
                slot = 2 * px + py
                _remote(mine(a), half(a, slot, c), send_sems, recv_sems, 6 * a + j, (px, py, c)).wait_recv()
                sent.append(_remote(half(a, slot, c), half(a, slot, c), send_sems, recv_sems, 6 * a + 3 + j, sibling))
                sent[-1].start()
        for a in range(n):
            for j, (px, py) in enumerate(chips):
                slot = 2 * px + py
                _remote(mine(a), half(a, slot, 1 - c), send_sems, recv_sems, 6 * a + 3 + j, sibling).wait_recv()
        for cp in sent:
            cp.wait_send()

    out_shapes = [jax.ShapeDtypeStruct((4,) + s.shape, s.dtype) for s in shards]
    return _comm_call(body, shards, out_shapes, 6 * n, 1, name)


def _swap_siblings(arrays, *, by_core, name):
    n = len(arrays)

    def body(*refs):
        ins, outs = refs[:n], refs[n:2 * n]
        send_sems, recv_sems, _ = refs[2 * n:]
        x, y, c, _chips = _place()
        cps = [_remote(ins[a].at[1 - c] if by_core else ins[a], outs[a], send_sems, recv_sems, a, (x, y, 1 - c))
               for a in range(n)]
        for cp in cps:
            cp.start()
        for cp in cps:
            cp.wait()

    out_shapes = [jax.ShapeDtypeStruct(a.shape[1:] if by_core else a.shape, a.dtype) for a in arrays]
    return _comm_call(body, arrays, out_shapes, n, 1, name)


def _scatter_chips(parts, small, *, name):
    n = len(parts)

    def body(*refs):
        ins, small_in = refs[:n], refs[n]
        outs, small_out = refs[n + 1:2 * n + 1], refs[2 * n + 1]
        send_sems, recv_sems, local_sems = refs[2 * n + 2:]
        x, y, c, chips = _place()
        me8 = 4 * x + 2 * y + c
        own = pltpu.make_async_copy(small_in, small_out.at[me8], local_sems.at[0])
        own.start()
        cps = []
        for fx in range(2):
            for fy in range(2):
                for fc in range(2):
                    r = 4 * fx + 2 * fy + fc - 1
                    if r >= 0:
                        to = (x + fx - 2 * x * fx, y + fy - 2 * y * fy, c + fc - 2 * c * fc)
                        cps.append(_remote(small_in, small_out.at[me8], send_sems, recv_sems, r, to))
        for a in range(n):
            for j, (px, py) in enumerate(chips):
                cps.append(_remote(ins[a].at[2 * px + py], outs[a].at[j], send_sems, recv_sems, 7 + 3 * a + j,
                                   (px, py, c)))
        for cp in cps:
            cp.start()
        for cp in cps:
            cp.wait()
        own.wait()

    out_shapes = [jax.ShapeDtypeStruct((3,) + p.shape[1:], p.dtype) for p in parts]
    out_shapes.append(jax.ShapeDtypeStruct((8,) + small.shape, small.dtype))
    return _comm_call(body, list(parts) + [small], out_shapes, 7 + 3 * n, 1, name)


WEIGHTS = ("meta", "norm_g", "mix_w_in", "mix_qk_conv_w", "mix_qk_conv_b", "mix_gate_b", "mix_hnorm_g", "mix_w_out",
           "conv_w_pw1", "conv_b_pw1", "conv_w_dw", "conv_b_dw", "conv_ln_g", "conv_ln_b", "conv_w_pw2",
           "conv_b_pw2", "ffn_w_gate", "ffn_w_up", "ffn_w_down")
SHARD_AXIS = dict(meta=1, norm_g=2, mix_w_in=2, mix_qk_conv_w=2, mix_qk_conv_b=None, mix_gate_b=None,
                  mix_hnorm_g=None, mix_w_out=1, conv_w_pw1=2, conv_b_pw1=1, conv_w_dw=2, conv_b_dw=1, conv_ln_g=1,
                  conv_ln_b=1, conv_w_pw2=1, conv_b_pw2=1, ffn_w_gate=2, ffn_w_up=2, ffn_w_down=1)
MATRICES = ("mix_w_in", "mix_w_out", "conv_w_pw1", "conv_w_pw2", "ffn_w_gate", "ffn_w_up", "ffn_w_down")
VECTORS = tuple(n for n in WEIGHTS if n not in MATRICES)
GATHER_COLS = D_MODEL // 4


def _pack_rows(arrays, cols, pad_to):
    rows = [a.astype(F32).reshape(-1) for a in arrays]
    rows = [jnp.pad(r, (0, (-r.shape[0]) % cols)).reshape(-1, cols) for r in rows]
    packed = jnp.concatenate(rows, axis=0)
    return jnp.pad(packed, ((0, pad_to - packed.shape[0]), (0, 0))), [r.shape[0] for r in rows]


def _unpack_rows(packed, counts, shapes):
    out, at = [], 0
    for n, shape in zip(counts, shapes):
        size = 1
        for s in shape:
            size *= s
        out.append(packed[..., at:at + n, :].reshape(packed.shape[:-2] + (-1,))[..., :size]
                   .reshape(packed.shape[:-2] + tuple(shape)))
        at += n
    return out


def _gather_weights(local):
    sharded_vecs = [n for n in VECTORS if SHARD_AXIS[n] is not None]
    pack, counts = _pack_rows([local[n] for n in sharded_vecs], GATHER_COLS, 120)
    shards = [local[n].astype(MXU_DTYPE) for n in MATRICES] + [pack.reshape(2, 60, GATHER_COLS)]
    me = 2 * lax.axis_index("x") + lax.axis_index("y")
    got = [lax.dynamic_update_index_in_dim(g, s, me, 0) for g, s in zip(_gather_chips(shards, name="gather_weights"),
                                                                        shards)]
    full = {n: local[n] for n in VECTORS if SHARD_AXIS[n] is None}
    for n, g in zip(MATRICES, got):
        full[n] = jnp.concatenate([g[k] for k in range(4)], axis=SHARD_AXIS[n])
    vecs = _unpack_rows(got[-1].reshape(4, 120, GATHER_COLS), counts, [local[n].shape for n in sharded_vecs])
    for n, v in zip(sharded_vecs, vecs):
        full[n] = jnp.moveaxis(v, 0, -2).reshape(v.shape[1:-1] + (4 * v.shape[-1],))
    return full


def _reduce_grads(grads):
    x, y, c = lax.axis_index("x"), lax.axis_index("y"), lax.axis_index("c")
    me = 2 * x + y
    stacked = []
    for n in MATRICES:
        g = jnp.stack(jnp.split(grads[n], 4, axis=SHARD_AXIS[n]), axis=0)
        g = g.reshape((4, 2, g.shape[1] // 2) + g.shape[2:])
        stacked.append(jnp.swapaxes(g, 0, 1))
    theirs = _swap_siblings(stacked, by_core=True, name="reduce_pair_swap")
    pair = [_elementwise(lambda a, b: [a + b], [lax.dynamic_index_in_dim(s, c, 0, keepdims=False), t], [F32],
                         name=f"reduce_pair_sum_{n}")[0] for n, s, t in zip(MATRICES, stacked, theirs)]
    shapes = [grads[n].shape for n in VECTORS]
    pack, counts = _pack_rows([grads[n] for n in VECTORS], D_MODEL, 120)
    got = _scatter_chips([p.astype(jnp.bfloat16) for p in pair], pack, name="reduce_chips")
    halves = []
    for n, p, r in zip(MATRICES, pair, got[:-1]):
        own = lax.dynamic_index_in_dim(p, me, 0, keepdims=False)
        halves.append(_elementwise(lambda a, b0, b1, b2: [((a + b0.astype(F32)) + b1.astype(F32)) + b2.astype(F32)],
                                   [own, r[0], r[1], r[2]], [F32], name=f"reduce_chip_sum_{n}")[0])
    others = _swap_siblings(halves, by_core=False, name="reduce_join")
    out = {n: jnp.where(c == 0, jnp.concatenate([h, o], axis=0), jnp.concatenate([o, h], axis=0))
           for n, h, o in zip(MATRICES, halves, others)}
    small = got[-1]
    total = _elementwise(lambda *s: [functools.reduce(lambda a, b: a + b, s)], [small[k] for k in range(8)], [F32],
                         name="reduce_small_sum")[0]
    for n, v in zip(VECTORS, _unpack_rows(total, counts, shapes)):
        ax = SHARD_AXIS[n]
        if ax is not None:
            w = v.shape[ax] // 4
            v = lax.dynamic_slice_in_dim(v, me * w, w, axis=ax)
        out[n] = v
    return out


def kernel(x, meta, norm_g, mix_w_in, mix_qk_conv_w, mix_qk_conv_b, mix_gate_b, mix_hnorm_g, mix_w_out, conv_w_pw1, conv_b_pw1, conv_w_dw, conv_b_dw, conv_ln_g, conv_ln_b, conv_w_pw2, conv_b_pw2, ffn_w_gate, ffn_w_up, ffn_w_down, loss_target, m_meta, m_norm_g, m_mix_w_in, m_mix_qk_conv_w, m_mix_qk_conv_b, m_mix_gate_b, m_mix_hnorm_g, m_mix_w_out, m_conv_w_pw1, m_conv_b_pw1, m_conv_w_dw, m_conv_b_dw, m_conv_ln_g, m_conv_ln_b, m_conv_w_pw2, m_conv_b_pw2, m_ffn_w_gate, m_ffn_w_up, m_ffn_w_down, v_meta, v_norm_g, v_mix_w_in, v_mix_qk_conv_w, v_mix_qk_conv_b, v_mix_gate_b, v_mix_hnorm_g, v_mix_w_out, v_conv_w_pw1, v_conv_b_pw1, v_conv_w_dw, v_conv_b_dw, v_conv_ln_g, v_conv_ln_b, v_conv_w_pw2, v_conv_b_pw2, v_ffn_w_gate, v_ffn_w_up, v_ffn_w_down):
    given = dict(locals())
    local = {n: given[n] for n in WEIGHTS}
    full = _gather_weights(local)
    loss, grad_x, grads = _local_step(x[0], loss_target[0], full)
    loss = lax.psum(loss, ("x", "y", "c"))
    grad_w = _reduce_grads(grads)
    delta, new_m, new_v = {}, {}, {}
    for n in WEIGHTS:
        delta[n], new_m[n], new_v[n] = _adamw(local[n], grad_w[n], given["m_" + n], given["v_" + n], name=f"adamw_{n}")
    return (loss, grad_x[None], *[grad_w[n] for n in WEIGHTS], *[delta[n] for n in WEIGHTS],
            *[new_m[n] for n in WEIGHTS], *[new_v[n] for n in WEIGHTS])
```

```python
import functools

import jax
import jax.numpy as jnp
from jax import lax
from jax.experimental import pallas as pl
from jax.experimental.pallas import tpu as pltpu

F32 = jnp.float32
MXU_DTYPE = jnp.bfloat16

D_MODEL = 1024
N_META = 16
DEPTH = 4
MLSTM_HEADS = 4
MLSTM_DQK = 128
MLSTM_DV = 256
MLSTM_CHUNK = 64
QK_CONV_WIDTH = 4
GATE_SOFTCAP = 15.0
SB_HEADS = 4
SB_DH = 128
SB_BLOCK = 128
PAD_FRONT = SB_BLOCK - N_META
CONV_WIDTH = 31
FFN_HIDDEN = 2816
MQK = MLSTM_HEADS * MLSTM_DQK
MV = MLSTM_HEADS * MLSTM_DV
SBW = SB_HEADS * SB_DH
IN_WIDTH = 2 * MQK + 2 * MV + 2 * MLSTM_HEADS + 3 * SBW
MIX_WIDTH = MV + SBW
NEG = -1e30
EPS = 1e-6
PROJ_WIDTH = 5120
GATE_COL = 3 * MV + 3 * SBW
LANE = 128
SUBLANE = 8
CONV_HALO = 32
VMEM_LIMIT = 56 * 1024 * 1024

ADAM_LR = 0.001
ADAM_B1 = 0.9
ADAM_B2 = 0.999
ADAM_EPS = 1e-08
ADAM_WD = 0.01
ADAM_STEP = 10


def _divisor(n, cands):
    for c in cands:
        if n % c == 0:
            return c
    raise ValueError(f"no tile for {n} in {cands}")


ROW_TILE_BYTES = 20 * 1024 * 1024


def _row_tile(tp, width=D_MODEL):
    for c in (640, 512, 384, 320, 256, 128, 64):
        if tp % c == 0 and c * width * 8 <= ROW_TILE_BYTES:
            return c
    raise ValueError(f"no row tile for {tp} x {width}")


def _params(sem):
    return pltpu.CompilerParams(dimension_semantics=sem, vmem_limit_bytes=VMEM_LIMIT)


def _dot(a, b, dims):
    return lax.dot_general(a.astype(MXU_DTYPE), b.astype(MXU_DTYPE), (dims, ((), ())),
                           preferred_element_type=F32)


def _nn(a, b):
    return _dot(a, b, ((1,), (0,)))


def _nt(a, b):
    return _dot(a, b, ((1,), (1,)))


def _tn(a, b):
    return _dot(a, b, ((0,), (0,)))


def _sigmoid(x):
    return 1.0 / (1.0 + jnp.exp(-x))


def _softplus(x):
    return jnp.maximum(x, 0.0) + jnp.log(1.0 + jnp.exp(-jnp.abs(x)))


MATMUL_VMEM_BYTES = 40 * 1024 * 1024


def _matmul_tiles(m, n, k, a_bytes, b_bytes):
    tm = _divisor(m, (1040, 1024, 1408, 768, 640, 512, 384, 256, 128))
    tn = _divisor(n, (1408, 1280, 1024, 768, 512, 256, 128))
    for tk in (5632, 5120, 2816, 2560, 2048, 1664, 1536, 1408, 1280, 1040, 1024, 768, 640, 512, 384, 256, 128):
        if k % tk:
            continue
        need = 2 * (tm * tk * a_bytes + tk * tn * b_bytes + tm * tn * 4) + (tm * tn * 4 if tk < k else 0)
        if need <= MATMUL_VMEM_BYTES:
            return tm, tn, tk
    raise ValueError(f"no matmul tiles for {m}x{n}x{k}")


def _matmul(a, b, *, ta=False, tb=False, name):
    m, k = (a.shape[1], a.shape[0]) if ta else a.shape
    n = b.shape[0] if tb else b.shape[1]
    assert (b.shape[1] if tb else b.shape[0]) == k, (a.shape, b.shape, ta, tb)
    tm, tn, tk = _matmul_tiles(m, n, k, a.dtype.itemsize, b.dtype.itemsize)
    nk = k // tk
    dims = ((0 if ta else 1,), (1 if tb else 0,))

    def body(a_ref, b_ref, o_ref, *acc):
        if nk == 1:
            o_ref[...] = _dot(a_ref[...], b_ref[...], dims)
            return
        acc_ref, kk = acc[0], pl.program_id(2)

        @pl.when(kk == 0)
        def _():
            acc_ref[...] = jnp.zeros_like(acc_ref)

        acc_ref[...] += _dot(a_ref[...], b_ref[...], dims)

        @pl.when(kk == nk - 1)
        def _():
            o_ref[...] = acc_ref[...]

    a_spec = (pl.BlockSpec((tk, tm), lambda i, j, kk: (kk, i)) if ta
              else pl.BlockSpec((tm, tk), lambda i, j, kk: (i, kk)))
    b_spec = (pl.BlockSpec((tn, tk), lambda i, j, kk: (j, kk)) if tb
              else pl.BlockSpec((tk, tn), lambda i, j, kk: (kk, j)))
    return pl.pallas_call(
        body, name=name, grid=(m // tm, n // tn, nk),
        in_specs=[a_spec, b_spec],
        out_specs=pl.BlockSpec((tm, tn), lambda i, j, kk: (i, j)),
        out_shape=jax.ShapeDtypeStruct((m, n), F32),
        scratch_shapes=[pltpu.VMEM((tm, tn), F32)] if nk > 1 else [],
        compiler_params=_params(("parallel", "parallel", "arbitrary")),
    )(a, b)


def _rowwise(fn, rows, fulls, out_rows, out_accs, *, name, out_dtypes=None):
    tp = rows[0][0].shape[0]
    tm = _row_tile(tp, sum(w for _, _, w in rows) + sum(out_rows))
    nr, nf, no, na = len(rows), len(fulls), len(out_rows), len(out_accs)
    out_dtypes = out_dtypes or [F32] * no

    def body(*refs):
        i = pl.program_id(0)
        outs = fn(i * tm, *[r[...].astype(F32) for r in refs[:nr + nf]])
        for k in range(no):
            refs[nr + nf + k][...] = outs[k].astype(out_dtypes[k])
        for k in range(na):
            ref = refs[nr + nf + no + k]

            @pl.when(i == 0)
            def _(ref=ref):
                ref[...] = jnp.zeros_like(ref)

            ref[...] += outs[no + k]

    in_specs = [pl.BlockSpec((tm, w), functools.partial(lambda i, cb: (i, cb), cb=cb)) for _, cb, w in rows]
    in_specs += [pl.BlockSpec(f.shape, lambda i: (0, 0)) for f in fulls]
    out_specs = [pl.BlockSpec((tm, w), lambda i: (i, 0)) for w in out_rows]
    out_specs += [pl.BlockSpec(s, lambda i: (0, 0)) for s in out_accs]
    out_shape = [jax.ShapeDtypeStruct((tp, w), dt) for w, dt in zip(out_rows, out_dtypes)]
    out_shape += [jax.ShapeDtypeStruct(s, F32) for s in out_accs]
    return pl.pallas_call(
        body, name=name, grid=(tp // tm,), in_specs=in_specs, out_specs=out_specs, out_shape=out_shape,
        compiler_params=_params(("arbitrary",)),
    )(*[r[0] for r in rows], *fulls)


def _whole(a):
    return (a, 0, a.shape[1])


def _live(row0, tm):
    return (row0 + lax.broadcasted_iota(jnp.int32, (tm, 1), 0)) >= PAD_FRONT


def _rms_core(x, g):
    r = lax.rsqrt(jnp.mean(x * x, axis=-1, keepdims=True) + EPS)
    return x * r, r


def _rms_fwd(x, g, *, name, res=None, bias=None, out_dtype=F32):
    def fn(row0, *blk):
        it = iter(blk)
        xv = next(it)
        rv = next(it) if res is not None else None
        gv = next(it)
        if bias is not None:
            xv = xv + next(it)
        xh, _ = _rms_core(xv, gv)
        y = jnp.where(_live(row0, xv.shape[0]), xh * gv, 0.0)
        return [y + rv if rv is not None else y]

    rows = [_whole(x)] + ([_whole(res)] if res is not None else [])
    fulls = [g] + ([bias] if bias is not None else [])
    return _rowwise(fn, rows, fulls, [x.shape[1]], [], name=name, out_dtypes=[out_dtype])[0]


def _rms_bwd(x, g, dy, *, name, add=None, bias=None, dy2=None, out_dtype=F32):
    def fn(row0, *blk):
        it = iter(blk)
        xv, dyv = next(it), next(it)
        if dy2 is not None:
            dyv = dyv + next(it)
        av = next(it) if add is not None else None
        gv = next(it)
        if bias is not None:
            xv = xv + next(it)
        dyv = jnp.where(_live(row0, xv.shape[0]), dyv, 0.0)
        xh, r = _rms_core(xv, gv)
        dyg = dyv * gv
        dx = r * (dyg - xh * jnp.mean(dyg * xh, axis=-1, keepdims=True))
        outs = [dx + av if av is not None else dx, jnp.sum(dyv * xh, axis=0, keepdims=True)]
        if bias is not None:
            outs.append(jnp.sum(dx, axis=0, keepdims=True))
        return outs

    rows = [_whole(x), _whole(dy)] + ([_whole(dy2)] if dy2 is not None else []) \
        + ([_whole(add)] if add is not None else [])
    fulls = [g] + ([bias] if bias is not None else [])
    c = x.shape[1]
    return _rowwise(fn, rows, fulls, [c], [(1, c)] * (2 if bias is not None else 1), name=name,
                    out_dtypes=[out_dtype])


def _ffn_tiles(m, n):
    return _divisor(m, (640, 384, 256, 128)), _divisor(n, (1408, 1024, 768, 512, 256, 128))


def _ffn_in(u, wg, wu, *, name):
    (m, k), n = u.shape, wg.shape[1]
    tm, tn = _ffn_tiles(m, n)

    def body(u_ref, wg_ref, wu_ref, a_ref, b_ref, s_ref):
        x = u_ref[...]
        a, b = _nn(x, wg_ref[...]), _nn(x, wu_ref[...])
        a_ref[...] = a
        b_ref[...] = b
        s_ref[...] = (a * _sigmoid(a) * b).astype(s_ref.dtype)

    w_spec = pl.BlockSpec((k, tn), lambda i, j: (0, j))
    o_spec = pl.BlockSpec((tm, tn), lambda i, j: (i, j))
    return pl.pallas_call(
        body, name=name, grid=(m // tm, n // tn),
        in_specs=[pl.BlockSpec((tm, k), lambda i, j: (i, 0)), w_spec, w_spec], out_specs=[o_spec] * 3,
        out_shape=[jax.ShapeDtypeStruct((m, n), F32), jax.ShapeDtypeStruct((m, n), F32),
                   jax.ShapeDtypeStruct((m, n), MXU_DTYPE)],
        compiler_params=_params(("parallel", "parallel")),
    )(u, wg, wu)


def _ffn_down_dx(df, wd, a, b, *, name):
    (m, k), n = df.shape, wd.shape[0]
    tm, tn = _ffn_tiles(m, n)

    def body(d_ref, w_ref, a_ref, b_ref, da_ref, db_ref):
        ds = _nt(d_ref[...], w_ref[...])
        av, bv = a_ref[...], b_ref[...]
        sg = _sigmoid(av)
        da_ref[...] = (ds * bv * sg * (1.0 + av * (1.0 - sg))).astype(da_ref.dtype)
        db_ref[...] = (ds * av * sg).astype(db_ref.dtype)

    t_spec = pl.BlockSpec((tm, tn), lambda i, j: (i, j))
    return pl.pallas_call(
        body, name=name, grid=(m // tm, n // tn),
        in_specs=[pl.BlockSpec((tm, k), lambda i, j: (i, 0)), pl.BlockSpec((tn, k), lambda i, j: (j, 0)), t_spec,
                  t_spec],
        out_specs=[t_spec, t_spec], out_shape=[jax.ShapeDtypeStruct((m, n), MXU_DTYPE)] * 2,
        compiler_params=_params(("parallel", "parallel")),
    )(df, wd, a, b)


def _glu_fwd(z, b, *, name):
    h = z.shape[1] // 2

    def fn(row0, a, gt, bv):
        y = (a + bv[:, :h]) * _sigmoid(gt + bv[:, h:])
        return [jnp.where(_live(row0, a.shape[0]), y, 0.0)]

    return _rowwise(fn, [(z, 0, h), (z, 1, h)], [b], [h], [], name=name)[0]


def _glu_bwd(z, b, dy, *, name):
    h = z.shape[1] // 2

    def fn(row0, a, gt, d, bv):
        d = jnp.where(_live(row0, a.shape[0]), d, 0.0)
        sg = _sigmoid(gt + bv[:, h:])
        dz = jnp.concatenate([d * sg, d * (a + bv[:, :h]) * sg * (1.0 - sg)], axis=1)
        return [dz, jnp.sum(dz, axis=0, keepdims=True)]

    return _rowwise(fn, [(z, 0, h), (z, 1, h), _whole(dy)], [b], [2 * h], [(1, 2 * h)], name=name,
                    out_dtypes=[MXU_DTYPE])


def _ln_core(x):
    mu = jnp.mean(x, axis=-1, keepdims=True)
    xc = x - mu
    r = lax.rsqrt(jnp.mean(xc * xc, axis=-1, keepdims=True) + EPS)
    return xc * r, r


def _lnsilu_fwd(x, g, b, *, name):
    def fn(row0, xv, gv, bv):
        xh, _ = _ln_core(xv)
        v = xh * gv + bv
        return [v * _sigmoid(v)]

    return _rowwise(fn, [_whole(x)], [g, b], [x.shape[1]], [], name=name, out_dtypes=[MXU_DTYPE])[0]


def _lnsilu_bwd(x, g, b, dy, *, name):
    def fn(row0, xv, d, gv, bv):
        xh, r = _ln_core(xv)
        v = xh * gv + bv
        sg = _sigmoid(v)
        dv = d * sg * (1.0 + v * (1.0 - sg))
        dxh = dv * gv
        dx = r * (dxh - jnp.mean(dxh, axis=-1, keepdims=True) - xh * jnp.mean(dxh * xh, axis=-1, keepdims=True))
        return [dx, jnp.sum(dv * xh, axis=0, keepdims=True), jnp.sum(dv, axis=0, keepdims=True)]

    c = x.shape[1]
    return _rowwise(fn, [_whole(x), _whole(dy)], [g, b], [c], [(1, c), (1, c)], name=name)


def _silu_fwd(x, *, name):
    return _rowwise(lambda row0, v: [v * _sigmoid(v)], [_whole(x)], [], [x.shape[1]], [], name=name)[0]


def _silu_bwd(x, dy, *, name):
    def fn(row0, v, d):
        sg = _sigmoid(v)
        return [d * sg * (1.0 + v * (1.0 - sg))]

    return _rowwise(fn, [_whole(x), _whole(dy)], [], [x.shape[1]], [], name=name)[0]


def _gate_parts(row0, pg, gb):
    lane = lax.broadcasted_iota(jnp.int32, pg.shape, 1)
    th = jnp.tanh((pg + gb) / GATE_SOFTCAP)
    s = GATE_SOFTCAP * th
    return lane, th, s, _live(row0, pg.shape[0])


def _gates_fwd(proj, gate_b, *, name):
    def fn(row0, pg, gb):
        lane, th, s, live = _gate_parts(row0, pg, gb)
        li = jnp.where(live, s, NEG)
        lf = jnp.where(live, -_softplus(-s), 0.0)
        return [jnp.where(lane < MLSTM_HEADS, li, jnp.where(lane < 2 * MLSTM_HEADS, lf, 0.0))]

    return _rowwise(fn, [(proj, GATE_COL // LANE, LANE)], [gate_b], [LANE], [], name=name)[0]


def _gates_bwd(proj, gate_b, dgl, *, name):
    def fn(row0, pg, d, gb):
        lane, th, s, live = _gate_parts(row0, pg, gb)
        ds = jnp.where(lane < MLSTM_HEADS, d, d * _sigmoid(-s))
        ds = jnp.where(live & (lane < 2 * MLSTM_HEADS), ds, 0.0)
        dp = ds * (1.0 - th * th)
        return [dp, jnp.sum(dp, axis=0, keepdims=True)]

    return _rowwise(fn, [(proj, GATE_COL // LANE, LANE), _whole(dgl)], [gate_b], [LANE], [(1, LANE)], name=name)


def _head_rms(h):
    parts = [h[:, i * MLSTM_DV:(i + 1) * MLSTM_DV] for i in range(MLSTM_HEADS)]
    rs = [lax.rsqrt(jnp.mean(p * p, axis=-1, keepdims=True) + EPS) for p in parts]
    return parts, rs


def _hnorm_fwd(hm, proj, g, *, name):
    def fn(row0, h, o, gv):
        parts, rs = _head_rms(h)
        xh = jnp.concatenate([p * r for p, r in zip(parts, rs)], axis=1)
        return [xh * gv * _sigmoid(o)]

    return _rowwise(fn, [_whole(hm), (proj, 2, MV)], [g], [MV], [], name=name)[0]


def _hnorm_bwd(hm, proj, g, dmixed, *, name):
    def fn(row0, h, o, d, gv):
        parts, rs = _head_rms(h)
        so = _sigmoid(o)
        dn = d * so
        dxs, xhs = [], []
        for i, (p, r) in enumerate(zip(parts, rs)):
            sl = slice(i * MLSTM_DV, (i + 1) * MLSTM_DV)
            xh = p * r
            dyg = dn[:, sl] * gv[:, sl]
            dxs.append(r * (dyg - xh * jnp.mean(dyg * xh, axis=-1, keepdims=True)))
            xhs.append(xh)
        xh = jnp.concatenate(xhs, axis=1)
        return [jnp.concatenate(dxs, axis=1), d * xh * gv * so * (1.0 - so), jnp.sum(dn * xh, axis=0, keepdims=True)]

    return _rowwise(fn, [_whole(hm), (proj, 2, MV), (dmixed, 0, MV)], [g], [MV, MV], [(1, MV)], name=name)


def _loss_fwd_bwd(h, target, *, name):
    first = PAD_FRONT + N_META

    def fn(row0, hv, tv):
        rows = row0 + lax.broadcasted_iota(jnp.int32, (hv.shape[0], 1), 0)
        e = jnp.where(rows >= first, hv - tv, 0.0)
        return [e * (1.0 / D_MODEL), jnp.sum(e * e, axis=0, keepdims=True)]

    return _rowwise(fn, [_whole(h), _whole(target)], [], [D_MODEL], [(1, D_MODEL)], name=name)


CONV_SUB = 64


def _conv_tiles(tp, c):
    return _row_tile(tp), _divisor(c, (256, 128))


def _conv_shift(win, shifted, tm):
    n = tm + CONV_HALO - SUBLANE
    for b in range(1, SUBLANE):
        shifted[b - 1, :, :] = win[b:b + n, :]


def _conv_window(win, shifted, offset, r0, rows):
    a, b = divmod(offset, SUBLANE)
    lo = a * SUBLANE + r0
    return win[lo:lo + rows, :] if b == 0 else shifted[b - 1, lo:lo + rows, :]


def _conv_fwd(x, w, b, *, name):
    tp, (k, c) = x.shape[0], w.shape
    tm, tc = _conv_tiles(tp, c)
    base = CONV_HALO - (k - 1)

    def body(x_ref, xp_ref, w_ref, b_ref, o_ref, win, shifted):
        i = pl.program_id(1)
        win[0:CONV_HALO, :] = jnp.where(i > 0, _mxu_rounded(xp_ref[tm - CONV_HALO:tm, :]), 0.0)
        win[CONV_HALO:CONV_HALO + tm, :] = _mxu_rounded(x_ref[...])
        _conv_shift(win, shifted, tm)
        for r0 in range(0, tm, CONV_SUB):
            acc = jnp.broadcast_to(b_ref[...], (CONV_SUB, tc))
            for j in range(k):
                acc = acc + _mxu_rounded(w_ref[j:j + 1, :]) *_conv_window(win, shifted, base + j, r0, CONV_SUB)
            o_ref[r0:r0 + CONV_SUB, :] = acc

    return pl.pallas_call(
        body, name=name, grid=(c // tc, tp // tm),
        in_specs=[pl.BlockSpec((tm, tc), lambda cc, i: (i, cc)),
                  pl.BlockSpec((tm, tc), lambda cc, i: (jnp.maximum(i - 1, 0), cc)),
                  pl.BlockSpec((k, tc), lambda cc, i: (0, cc)),
                  pl.BlockSpec((1, tc), lambda cc, i: (0, cc))],
        out_specs=pl.BlockSpec((tm, tc), lambda cc, i: (i, cc)),
        out_shape=jax.ShapeDtypeStruct((tp, c), F32),
        scratch_shapes=[pltpu.VMEM((CONV_HALO + tm, tc), F32),
                        pltpu.VMEM((SUBLANE - 1, CONV_HALO + tm - SUBLANE, tc), F32)],
        compiler_params=_params(("parallel", "arbitrary")),
    )(x, x, w, b)


def _conv_bwd(x, w, dy, *, name):
    tp, (k, c) = x.shape[0], w.shape
    tm, tc = _conv_tiles(tp, c)
    nt = tp // tm
    base = CONV_HALO - (k - 1)

    def body(x_ref, xp_ref, d_ref, dn_ref, w_ref, dx_ref, dw_ref, db_ref, winx, wind, shx, shd):
        i = pl.program_id(1)
        winx[0:CONV_HALO, :] = jnp.where(i > 0, _mxu_rounded(xp_ref[tm - CONV_HALO:tm, :]), 0.0)
        winx[CONV_HALO:CONV_HALO + tm, :] = _mxu_rounded(x_ref[...])
        d = d_ref[...]
        wind[0:tm, :] = _mxu_rounded(d)
        wind[tm:tm + CONV_HALO, :] = jnp.where(i < nt - 1, _mxu_rounded(dn_ref[0:CONV_HALO, :]), 0.0)
        _conv_shift(winx, shx, tm)
        _conv_shift(wind, shd, tm)

        @pl.when(i == 0)
        def _():
            dw_ref[...] = jnp.zeros_like(dw_ref)
            db_ref[...] = jnp.zeros_like(db_ref)

        for r0 in range(0, tm, CONV_SUB):
            acc = jnp.zeros((CONV_SUB, tc), F32)
            for j in range(k):
                acc = acc + _mxu_rounded(w_ref[j:j + 1, :]) *_conv_window(wind, shd, k - 1 - j, r0, CONV_SUB)
            dx_ref[r0:r0 + CONV_SUB, :] = acc
        for j in range(k):
            part = jnp.zeros((SUBLANE, tc), F32)
            for r0 in range(0, tm, CONV_SUB):
                p = wind[r0:r0 + CONV_SUB, :] * _conv_window(winx, shx, base + j, r0, CONV_SUB)
                part = part + jnp.sum(p.reshape(CONV_SUB // SUBLANE, SUBLANE, tc), axis=0)
            dw_ref[j:j + 1, :] += jnp.sum(part, axis=0, keepdims=True)
        db_ref[...] += jnp.sum(d, axis=0, keepdims=True)

    return pl.pallas_call(
        body, name=name, grid=(c // tc, nt),
        in_specs=[pl.BlockSpec((tm, tc), lambda cc, i: (i, cc)),
                  pl.BlockSpec((tm, tc), lambda cc, i: (jnp.maximum(i - 1, 0), cc)),
                  pl.BlockSpec((tm, tc), lambda cc, i: (i, cc)),
                  pl.BlockSpec((tm, tc), lambda cc, i: (jnp.minimum(i + 1, nt - 1), cc)),
                  pl.BlockSpec((k, tc), lambda cc, i: (0, cc))],
        out_specs=[pl.BlockSpec((tm, tc), lambda cc, i: (i, cc)),
                   pl.BlockSpec((k, tc), lambda cc, i: (0, cc)),
                   pl.BlockSpec((1, tc), lambda cc, i: (0, cc))],
        out_shape=[jax.ShapeDtypeStruct((tp, c), F32), jax.ShapeDtypeStruct((k, c), F32),
                   jax.ShapeDtypeStruct((1, c), F32)],
        scratch_shapes=[pltpu.VMEM((CONV_HALO + tm, tc), F32), pltpu.VMEM((CONV_HALO + tm, tc), F32),
                        pltpu.VMEM((SUBLANE - 1, CONV_HALO + tm - SUBLANE, tc), F32),
                        pltpu.VMEM((SUBLANE - 1, CONV_HALO + tm - SUBLANE, tc), F32)],
        compiler_params=_params(("parallel", "arbitrary")),
    )(x, x, dy, dy, w)


def _chunk_masks():
    L = MLSTM_CHUNK
    r = lax.broadcasted_iota(jnp.int32, (L, L), 0)
    c = lax.broadcasted_iota(jnp.int32, (L, L), 1)
    return r == c, c <= r, r <= c


def _to_row(col, eye):
    return jnp.sum(jnp.where(eye, col, 0.0), axis=0, keepdims=True)


def _to_col(row, eye):
    return jnp.sum(jnp.where(eye, row, 0.0), axis=1, keepdims=True)


def _mxu_rounded(x):
    return x.astype(MXU_DTYPE).astype(F32)


def _mlstm_group(nc):
    return _divisor(nc, (5, 4, 3, 2, 1))


def _each(f, *lists):
    return [f(*args) for args in zip(*lists)]


def _rsum(x):
    return jnp.sum(x, axis=1, keepdims=True)


def _csum(x):
    return jnp.sum(x, axis=0, keepdims=True)


def _mlstm_chunk(q, k, v, li_c, lf_c, c_st, n_st, m_st, masks):
    eye, low, up = masks
    li_r = _each(lambda c: _to_row(c, eye), li_c)
    lf_r = _each(lambda c: _to_row(c, eye), lf_c)
    b_c = _each(lambda r: _rsum(jnp.where(low, r, 0.0)), lf_r)
    b_r = _each(lambda c: _csum(jnp.where(up, c, 0.0)), lf_c)
    g = _each(_csum, lf_c)
    dm = _each(lambda bc, br, lr: jnp.where(low, bc - br + lr, NEG), b_c, b_r, li_r)
    inter = _each(lambda bc, m: bc + m, b_c, m_st)
    mt = _each(lambda i, d: jnp.maximum(i, jnp.max(d, axis=1, keepdims=True)), inter, dm)
    wi = _each(lambda d, m: jnp.exp(d - m), dm, mt)
    wint = _each(lambda i, m: jnp.exp(i - m), inter, mt)
    qk_ = _each(_nt, q, k)
    qc = _each(_nn, q, c_st)
    s = _each(lambda a, w: a * w, qk_, wi)
    qn = _each(lambda a, n: _rsum(_mxu_rounded(a) * _mxu_rounded(n)), q, n_st)
    sv = _each(_nn, s, v)
    num = _each(lambda a, w, b: a + w * b, sv, wint, qc)
    den = _each(lambda a, w, b: _rsum(a) + w * b, s, wint, qn)
    floor = _each(lambda m: jnp.exp(-m), mt)
    a_c = _each(lambda gg, b, l: gg - b + l, g, b_c, li_c)
    a_r = _each(lambda gg, b, l: gg - b + l, g, b_r, li_r)
    mnew = _each(lambda gg, m, a: jnp.maximum(gg + m, jnp.max(a, axis=1, keepdims=True)), g, m_st, a_r)
    wa_c = _each(lambda a, m: jnp.exp(a - m), a_c, mnew)
    wc = _each(lambda gg, m, mn: jnp.exp(gg + m - mn), g, m_st, mnew)
    return dict(wi=wi, wint=wint, s=s, qc=qc, qn=qn, num=num, den=den, floor=floor, mnew=mnew, wa_c=wa_c, wc=wc)


def _mlstm_heads(qk_ref, v_ref, gl_ref, rows):
    H, dk, dv = MLSTM_HEADS, MLSTM_DQK, MLSTM_DV
    gates = gl_ref[rows, :]
    return ([qk_ref[rows, h * dk:(h + 1) * dk] * (dk ** -0.5) for h in range(H)],
            [qk_ref[rows, MQK + h * dk:MQK + (h + 1) * dk] for h in range(H)],
            [v_ref[rows, h * dv:(h + 1) * dv] for h in range(H)],
            [gates[:, h:h + 1] for h in range(H)], [gates[:, H + h:H + h + 1] for h in range(H)])


def _mlstm_fwd(qk, proj, gl, *, name):
    tp = qk.shape[0]
    L, H, dk, dv = MLSTM_CHUNK, MLSTM_HEADS, MLSTM_DQK, MLSTM_DV
    nc = tp // L
    G = _mlstm_group(nc)

    def body(qk_ref, v_ref, gl_ref, h_ref, call_ref, nall_ref, mall_ref, c_s, n_s, m_s):
        @pl.when(pl.program_id(0) == 0)
        def _():
            c_s[...] = jnp.zeros_like(c_s)
            n_s[...] = jnp.zeros_like(n_s)
            m_s[...] = jnp.zeros_like(m_s)

        masks = _chunk_masks()
        heads = range(H)
        c_st, n_st, m_row = [c_s[h] for h in heads], [n_s[h] for h in heads], [m_s[h] for h in heads]
        for ci in range(G):
            rows = slice(ci * L, (ci + 1) * L)
            for h in heads:
                call_ref[ci, h], nall_ref[ci, h], mall_ref[ci, h] = c_st[h], n_st[h], m_row[h]
            q, k, v, li, lf = _mlstm_heads(qk_ref, v_ref, gl_ref, rows)
            f = _mlstm_chunk(q, k, v, li, lf, c_st, n_st, [m[:, 0:1] for m in m_row], masks)
            out = _each(lambda a, b, c: a / jnp.maximum(jnp.abs(b), c), f["num"], f["den"], f["floor"])
            for h in heads:
                h_ref[rows, h * dv:(h + 1) * dv] = out[h]
            kv = _each(_tn, _each(lambda a, w: a * w, k, f["wa_c"]), v)
            c_st = _each(lambda w, c, x: w * c + x, f["wc"], c_st, kv)
            n_st = _each(lambda w, n, a, b: w * n + _csum(_mxu_rounded(a) * _mxu_rounded(b)), f["wc"], n_st, k, f["wa_c"])
            m_row = _each(lambda m: jnp.broadcast_to(m, (1, LANE)), f["mnew"])
        for h in heads:
            c_s[h], n_s[h], m_s[h] = c_st[h], n_st[h], m_row[h]

    return pl.pallas_call(
        body, name=name, grid=(nc // G,),
        in_specs=[pl.BlockSpec((G * L, 2 * MQK), lambda i: (i, 0)),
                  pl.BlockSpec((G * L, MV), lambda i: (i, 1)),
                  pl.BlockSpec((G * L, LANE), lambda i: (i, 0))],
        out_specs=[pl.BlockSpec((G * L, MV), lambda i: (i, 0)),
                   pl.BlockSpec((G, H, dk, dv), lambda i: (i, 0, 0, 0)),
                   pl.BlockSpec((G, H, 1, dk), lambda i: (i, 0, 0, 0)),
                   pl.BlockSpec((G, H, 1, LANE), lambda i: (i, 0, 0, 0))],
        out_shape=[jax.ShapeDtypeStruct((tp, MV), F32),
                   jax.ShapeDtypeStruct((nc, H, dk, dv), F32),
                   jax.ShapeDtypeStruct((nc, H, 1, dk), F32),
                   jax.ShapeDtypeStruct((nc, H, 1, LANE), F32)],
        scratch_shapes=[pltpu.VMEM((H, dk, dv), F32), pltpu.VMEM((H, 1, dk), F32), pltpu.VMEM((H, 1, LANE), F32)],
        compiler_params=_params(("arbitrary",)),
    )(qk, proj, gl)


def _mlstm_bwd(qk, proj, gl, dmix, call, nall, mall, *, name):
    tp = qk.shape[0]
    L, H, dk, dv = MLSTM_CHUNK, MLSTM_HEADS, MLSTM_DQK, MLSTM_DV
    nc = tp // L
    G = _mlstm_group(nc)

    def body(qk_ref, v_ref, gl_ref, dh_ref, call_ref, nall_ref, mall_ref, dqk_ref, dv_ref, dgl_ref, dc_s, dn_s):
        @pl.when(pl.program_id(0) == 0)
        def _():
            dc_s[...] = jnp.zeros_like(dc_s)
            dn_s[...] = jnp.zeros_like(dn_s)

        masks = _chunk_masks()
        eye, low, up = masks
        lane = lax.broadcasted_iota(jnp.int32, (L, LANE), 1)
        heads = range(H)
        dcn, dnn = [dc_s[h] for h in heads], [dn_s[h] for h in heads]
        for ci in reversed(range(G)):
            rows = slice(ci * L, (ci + 1) * L)
            c_st = [call_ref[ci, h] for h in heads]
            n_st = [nall_ref[ci, h] for h in heads]
            m_st = [mall_ref[ci, h][:, 0:1] for h in heads]
            q, k, v, li, lf = _mlstm_heads(qk_ref, v_ref, gl_ref, rows)
            dh = [dh_ref[rows, h * dv:(h + 1) * dv] for h in heads]
            f = _mlstm_chunk(q, k, v, li, lf, c_st, n_st, m_st, masks)
            wint, s, wa_c, wc, den, floor = f["wint"], f["s"], f["wa_c"], f["wc"], f["den"], f["floor"]

            r = _each(lambda a, b: 1.0 / jnp.maximum(jnp.abs(a), b), den, floor)
            dnum = _each(lambda a, b: a * b, dh, r)
            dscale = _each(lambda a, b, c: -_rsum(a * b) * c * c, dh, f["num"], r)
            dden = _each(lambda a, b, c: jnp.where(jnp.abs(a) > b, c * jnp.sign(a), 0.0), den, floor, dscale)
            dnv = _each(_nt, dnum, v)
            wd = _each(lambda a, b: a * b, wint, dnum)
            dq_c = _each(_nt, wd, c_st)
            dc_acc = _each(_tn, q, wd)
            dv_s = _each(_tn, s, dnum)
            kd = _each(_nn, k, dcn)
            vd = _each(_nt, v, dcn)
            ds = _each(lambda a, b: a + b, dnv, dden)
            dwint = _each(lambda a, b, c, d: _rsum(a * b) + c * d, dnum, f["qc"], dden, f["qn"])
            dd = _each(lambda a, b: a * b, ds, s)
            da_mat = _each(lambda a, b: a * b, ds, f["wi"])
            dq_a = _each(_nn, da_mat, k)
            dk_a = _each(_tn, da_mat, q)
            dw_n = _each(lambda a, b: a * b, dden, wint)
            dq = _each(lambda a, b, c, d: a + b * c + d, dq_c, dw_n, n_st, dq_a)
            dn_acc = _each(lambda a, b: _csum(a * b), q, dw_n)
            dk_ = _each(lambda a, w, b, c: a + w * (b + c), dk_a, wa_c, vd, dnn)
            dv_ = _each(lambda a, w, b: a + w * b, dv_s, wa_c, kd)
            dwa = _each(lambda a, b, c, d: _rsum(a * b) + _rsum(c * d), kd, v, k, dnn)
            dwc = _each(lambda a, b, c, d: _csum(_rsum(a * b)) + _rsum(c * d), dcn, c_st, dnn, n_st)
            da_c = _each(lambda a, b: a * b, dwa, wa_c)
            dg = _each(lambda a, b, c: _csum(a) + b * c, da_c, dwc, wc)
            dd_cols = _each(_csum, dd)
            db_c = _each(lambda a, b, c, d: a * b + _rsum(c) - d, dwint, wint, dd, da_c)
            db_r = _each(lambda a, b: _to_row(a, eye) - b, db_c, dd_cols)
            dlf = _each(lambda a, b: _rsum(jnp.where(up, a, 0.0)) + b, db_r, dg)
            dli = _each(lambda a, b: a + _to_col(b, eye), da_c, dd_cols)
            dcn = _each(lambda w, a, b: w * a + b, wc, dcn, dc_acc)
            dnn = _each(lambda w, a, b: w * a + b, wc, dnn, dn_acc)

            dgl = jnp.zeros((L, LANE), F32)
            for h in heads:
                dqk_ref[rows, h * dk:(h + 1) * dk] = dq[h] * (dk ** -0.5)
                dqk_ref[rows, MQK + h * dk:MQK + (h + 1) * dk] = dk_[h]
                dv_ref[rows, h * dv:(h + 1) * dv] = dv_[h]
                dgl = dgl + jnp.where(lane == h, dli[h], 0.0) + jnp.where(lane == H + h, dlf[h], 0.0)
            dgl_ref[rows, :] = dgl
        for h in heads:
            dc_s[h], dn_s[h] = dcn[h], dnn[h]

    rev = lambda i: nc // G - 1 - i
    return pl.pallas_call(
        body, name=name, grid=(nc // G,),
        in_specs=[pl.BlockSpec((G * L, 2 * MQK), lambda i: (rev(i), 0)),
                  pl.BlockSpec((G * L, MV), lambda i: (rev(i), 1)),
                  pl.BlockSpec((G * L, LANE), lambda i: (rev(i), 0)),
                  pl.BlockSpec((G * L, MV), lambda i: (rev(i), 0)),
                  pl.BlockSpec((G, H, dk, dv), lambda i: (rev(i), 0, 0, 0)),
                  pl.BlockSpec((G, H, 1, dk), lambda i: (rev(i), 0, 0, 0)),
                  pl.BlockSpec((G, H, 1, LANE), lambda i: (rev(i), 0, 0, 0))],
        out_specs=[pl.BlockSpec((G * L, 2 * MQK), lambda i: (rev(i), 0)),
                   pl.BlockSpec((G * L, MV), lambda i: (rev(i), 0)),
                   pl.BlockSpec((G * L, LANE), lambda i: (rev(i), 0))],
        out_shape=[jax.ShapeDtypeStruct((tp, 2 * MQK), F32), jax.ShapeDtypeStruct((tp, MV), F32),
                   jax.ShapeDtypeStruct((tp, LANE), F32)],
        scratch_shapes=[pltpu.VMEM((H, dk, dv), F32), pltpu.VMEM((H, 1, dk), F32)],
        compiler_params=_params(("arbitrary",)),
    )(qk, proj, gl, dmix, call, nall, mall)


SB_EXP_CAP = 80.0
SB_Q0 = 3 * MV // LANE
SB_K0 = SB_Q0 + SB_HEADS
SB_V0 = SB_K0 + SB_HEADS


def _cumsum_dot(x, tri):
    hi = x.astype(jnp.bfloat16)
    lo = (x - hi.astype(F32)).astype(jnp.bfloat16)
    dims = (((1,), (0,)), ((), ()))
    return (lax.dot_general(hi, tri, dims, preferred_element_type=F32)
            + lax.dot_general(lo, tri, dims, preferred_element_type=F32))


def _sb_query_blocks(nq):
    return _divisor(nq, (5, 4, 3, 2, 1))


def _sb_mask(tile, g, tq):
    t_idx = tile * tq + lax.broadcasted_iota(jnp.int32, (tq, tq), 0)
    s_idx = g * tq + lax.broadcasted_iota(jnp.int32, (tq, tq), 1)
    return (s_idx < t_idx) & (s_idx >= PAD_FRONT)


def _sb_blocks(x):
    return [x[:, k * SB_BLOCK:(k + 1) * SB_BLOCK] for k in range(x.shape[1] // SB_BLOCK)]


def _sb_logits(qb, kg, tri, mask):
    z = _nt(qb, kg) * (SB_DH ** -0.5)
    zc = jnp.minimum(z, SB_EXP_CAP)
    l = (zc - z) - jnp.log(1.0 + jnp.exp(zc))
    if mask is not None:
        l = jnp.where(mask, l, 0.0)
    return z, l, [_cumsum_dot(b, tri) for b in _sb_blocks(l)]


def _sb_weights(z, withins, runs, mask):
    e = jnp.exp(z + jnp.concatenate([w + r for w, r in zip(withins, runs)], axis=1))
    return e if mask is None else jnp.where(mask, e, 0.0)


def _sb_segments(tile, group, descending):
    def diagonal():
        group(tile, True)

    def interior():
        def it(gg, c):
            group(tile - 1 - gg if descending else 1 + gg, False)
            return c
        lax.fori_loop(0, jnp.maximum(tile - 1, 0), it, 0)

    def first():
        @pl.when(tile > 0)
        def _():
            group(0, True)

    for part in ((diagonal, interior, first) if descending else (first, interior, diagonal)):
        part()


def _sb_fwd(proj, *, name):
    tp = proj.shape[0]
    B, H = SB_BLOCK, SB_HEADS
    nq = tp // B
    assert nq <= LANE and B == LANE
    r = _sb_query_blocks(nq)
    tq = r * B

    def body(q_ref, k_ref, v_ref, o_ref, ac_ref, run_s):
        tile = pl.program_id(1)
        qb = q_ref[...].astype(MXU_DTYPE)
        lane = lax.broadcasted_iota(jnp.int32, (tq, LANE), 1)
        tri = (lax.broadcasted_iota(jnp.int32, (B, B), 0) >= lax.broadcasted_iota(jnp.int32, (B, B), 1)
               ).astype(jnp.bfloat16)
        o_ref[...] = jnp.zeros_like(o_ref)
        ac_ref[0, 0] = jnp.zeros((tq, LANE), F32)
        run_s[...] = jnp.zeros_like(run_s)

        def group(g, masked):
            grows = pl.ds(pl.multiple_of(g * tq, tq), tq)
            mask = _sb_mask(tile, g, tq) if masked else None
            z, l, withins = _sb_logits(qb, k_ref[grows, :], tri, mask)
            run, saved, runs = run_s[...], ac_ref[0, 0], [None] * r
            for k in reversed(range(r)):
                runs[k] = run
                saved = jnp.where(lane == g * r + k, run, saved)
                run = run + withins[k][:, 0:1]
            o_ref[...] += _nn(_sb_weights(z, withins, runs, mask), v_ref[grows, :])
            ac_ref[0, 0] = saved
            run_s[...] = run

        _sb_segments(tile, group, descending=True)

    return pl.pallas_call(
        body, name=name, grid=(H, nq // r), scratch_shapes=[pltpu.VMEM((tq, LANE), F32)],
        in_specs=[pl.BlockSpec((tq, SB_DH), lambda h, i: (i, SB_Q0 + h)),
                  pl.BlockSpec((tp, SB_DH), lambda h, i: (0, SB_K0 + h)),
                  pl.BlockSpec((tp, SB_DH), lambda h, i: (0, SB_V0 + h))],
        out_specs=[pl.BlockSpec((tq, SB_DH), lambda h, i: (i, h)),
                   pl.BlockSpec((1, 1, tq, LANE), lambda h, i: (h, i, 0, 0))],
        out_shape=[jax.ShapeDtypeStruct((tp, SBW), F32), jax.ShapeDtypeStruct((H, nq // r, tq, LANE), F32)],
        compiler_params=_params(("parallel", "arbitrary")),
    )(proj, proj, proj)


def _sb_bwd(proj, across, dmix, *, name):
    tp = proj.shape[0]
    B, H = SB_BLOCK, SB_HEADS
    nq = tp // B
    r = _sb_query_blocks(nq)
    tq = r * B
    do0 = MV // LANE

    def body(q_ref, k_ref, v_ref, ac_ref, do_ref, dq_ref, dk_ref, dv_ref, gpre_s):
        tile = pl.program_id(1)

        @pl.when(tile == 0)
        def _():
            dk_ref[...] = jnp.zeros_like(dk_ref)
            dv_ref[...] = jnp.zeros_like(dv_ref)

        dq_ref[...] = jnp.zeros_like(dq_ref)
        gpre_s[...] = jnp.zeros_like(gpre_s)
        qb, dob = q_ref[...].astype(MXU_DTYPE), do_ref[...].astype(MXU_DTYPE)
        lane = lax.broadcasted_iota(jnp.int32, (tq, LANE), 1)
        rr = lax.broadcasted_iota(jnp.int32, (B, B), 0)
        cc = lax.broadcasted_iota(jnp.int32, (B, B), 1)
        tri = (rr >= cc).astype(jnp.bfloat16)
        prefix = (rr <= cc).astype(jnp.bfloat16)
        scale = SB_DH ** -0.5

        def group(g, masked):
            grows = pl.ds(pl.multiple_of(g * tq, tq), tq)
            kg, vg = k_ref[grows, :], v_ref[grows, :]
            mask = _sb_mask(tile, g, tq) if masked else None
            z, l, withins = _sb_logits(qb, kg, tri, mask)
            saved = ac_ref[0, 0]
            runs = [jnp.sum(jnp.where(lane == g * r + k, saved, 0.0), axis=1, keepdims=True) for k in range(r)]
            w = _sb_weights(z, withins, runs, mask)
            dv_ref[grows, :] += _tn(w, dob)
            gw = _nt(dob, vg) * w
            gpre, gcum = gpre_s[...], []
            for gc in [_cumsum_dot(b, prefix) for b in _sb_blocks(gw)]:
                gcum.append(gc + gpre)
                gpre = gpre + gc[:, B - 1:B]
            beta_g = jnp.exp(z + l) * jnp.concatenate(gcum, axis=1)
            if masked:
                beta_g = jnp.where(mask, beta_g, 0.0)
            dz = ((gw - beta_g) * scale).astype(MXU_DTYPE)
            dk_ref[grows, :] += _tn(dz, qb)
            dq_ref[...] += _nn(dz, kg)
            gpre_s[...] = gpre

        _sb_segments(tile, group, descending=False)

    return pl.pallas_call(
        body, name=name, grid=(H, nq // r), scratch_shapes=[pltpu.VMEM((tq, LANE), F32)],
        in_specs=[pl.BlockSpec((tq, SB_DH), lambda h, i: (i, SB_Q0 + h)),
                  pl.BlockSpec((tp, SB_DH), lambda h, i: (0, SB_K0 + h)),
                  pl.BlockSpec((tp, SB_DH), lambda h, i: (0, SB_V0 + h)),
                  pl.BlockSpec((1, 1, tq, LANE), lambda h, i: (h, i, 0, 0)),
                  pl.BlockSpec((tq, SB_DH), lambda h, i: (i, do0 + h))],
        out_specs=[pl.BlockSpec((tq, SB_DH), lambda h, i: (i, h)),
                   pl.BlockSpec((tp, SB_DH), lambda h, i: (0, h)),
                   pl.BlockSpec((tp, SB_DH), lambda h, i: (0, h))],
        out_shape=[jax.ShapeDtypeStruct((tp, SBW), F32)] * 3,
        compiler_params=_params(("parallel", "arbitrary")),
    )(proj, proj, proj, across, dmix)


def _ffn_forward(h, p, tag):
    u = _rms_fwd(h, p["g2"], out_dtype=MXU_DTYPE, name=f"{tag}_ffn_norm")
    a, b, s = _ffn_in(u, p["w_gate"], p["w_up"], name=f"{tag}_ffn_gate_up")
    f = _matmul(s, p["w_down"], name=f"{tag}_ffn_down")
    out = _rms_fwd(f, p["g3"], res=h, name=f"{tag}_ffn_out")
    return out, dict(h=h, u=u, a=a, b=b, s=s, f=f)


def _ffn_backward(dh, p, a, tag):
    df, dg3 = _rms_bwd(a["f"], p["g3"], dh, out_dtype=MXU_DTYPE, name=f"{tag}_ffn_out_bwd")
    da, db = _ffn_down_dx(df, p["w_down"], a["a"], a["b"], name=f"{tag}_ffn_down_dx")
    dw_down = _matmul(a["s"], df, ta=True, name=f"{tag}_ffn_down_dw")
    du_gate = _matmul(da, p["w_gate"], tb=True, name=f"{tag}_ffn_gate_dx")
    du_up = _matmul(db, p["w_up"], tb=True, name=f"{tag}_ffn_up_dx")
    dw_gate = _matmul(a["u"], da, ta=True, name=f"{tag}_ffn_gate_dw")
    dw_up = _matmul(a["u"], db, ta=True, name=f"{tag}_ffn_up_dw")
    dh_in, dg2 = _rms_bwd(a["h"], p["g2"], du_gate, dy2=du_up, add=dh, name=f"{tag}_ffn_norm_bwd")
    return dh_in, dict(g2=dg2, g3=dg3, w_gate=dw_gate, w_up=dw_up, w_down=dw_down)


def _mixer_forward(h, p, tag):
    u = _rms_fwd(h, p["g0"], out_dtype=MXU_DTYPE, name=f"{tag}_mix_norm")
    proj = _matmul(u, p["w_in"], name=f"{tag}_mix_in")
    qc = _conv_fwd(proj, p["qk_w"], p["qk_b"], name=f"{tag}_mix_qkconv")
    qk = _silu_fwd(qc, name=f"{tag}_mix_qkact")
    gl = _gates_fwd(proj, p["gate_b"], name=f"{tag}_mix_gates")
    hm, call, nall, mall = _mlstm_fwd(qk, proj, gl, name=f"{tag}_mlstm")
    hn = _hnorm_fwd(hm, proj, p["hnorm_g"], name=f"{tag}_mix_hnorm")
    hs, across = _sb_fwd(proj, name=f"{tag}_sb")
    mixed = jnp.concatenate([hn, hs], axis=1).astype(MXU_DTYPE)
    y = _matmul(mixed, p["w_out"], name=f"{tag}_mix_out")
    out = _rms_fwd(y, p["g1"], res=h, name=f"{tag}_mix_res")
    return out, dict(h=h, u=u, proj=proj, qc=qc, qk=qk, gl=gl, hm=hm, call=call, nall=nall, mall=mall,
                     across=across, mixed=mixed, y=y)


def _mixer_backward(dh, p, a, tag):
    tp = dh.shape[0]
    dy, dg1 = _rms_bwd(a["y"], p["g1"], dh, out_dtype=MXU_DTYPE, name=f"{tag}_mix_res_bwd")
    dmixed = _matmul(dy, p["w_out"], tb=True, name=f"{tag}_mix_out_dx")
    dw_out = _matmul(a["mixed"], dy, ta=True, name=f"{tag}_mix_out_dw")
    dsq, dsk, dsv = _sb_bwd(a["proj"], a["across"], dmixed, name=f"{tag}_sb_bwd")
    dhm, do, dhg = _hnorm_bwd(a["hm"], a["proj"], p["hnorm_g"], dmixed, name=f"{tag}_mix_hnorm_bwd")
    dqk, dv, dgl = _mlstm_bwd(a["qk"], a["proj"], a["gl"], dhm, a["call"], a["nall"], a["mall"],
                              name=f"{tag}_mlstm_bwd")
    dpg, dgate_b = _gates_bwd(a["proj"], p["gate_b"], dgl, name=f"{tag}_mix_gates_bwd")
    dqc = _silu_bwd(a["qc"], dqk, name=f"{tag}_mix_qkact_bwd")
    dpqk, dqk_w, dqk_b = _conv_bwd(a["proj"], p["qk_w"], dqc, name=f"{tag}_mix_qkconv_bwd")
    dproj = jnp.concatenate(
        [dpqk, dv, do, dsq, dsk, dsv, dpg, jnp.zeros((tp, PROJ_WIDTH - GATE_COL - LANE), F32)], axis=1
    ).astype(MXU_DTYPE)
    du = _matmul(dproj, p["w_in"], tb=True, name=f"{tag}_mix_in_dx")
    dw_in = _matmul(a["u"], dproj, ta=True, name=f"{tag}_mix_in_dw")
    dh_in, dg0 = _rms_bwd(a["h"], p["g0"], du, add=dh, name=f"{tag}_mix_norm_bwd")
    return dh_in, dict(g0=dg0, g1=dg1, w_in=dw_in, qk_w=dqk_w, qk_b=dqk_b, gate_b=dgate_b, hnorm_g=dhg,
                       w_out=dw_out)


def _conformer_forward(h, p, tag):
    u = _rms_fwd(h, p["g0"], out_dtype=MXU_DTYPE, name=f"{tag}_conf_norm")
    z = _matmul(u, p["w_pw1"], name=f"{tag}_conf_pw1")
    y1 = _glu_fwd(z, p["b_pw1"], name=f"{tag}_conf_glu")
    y2 = _conv_fwd(y1, p["w_dw"], p["b_dw"], name=f"{tag}_conf_dw")
    y3 = _lnsilu_fwd(y2, p["ln_g"], p["ln_b"], name=f"{tag}_conf_ln")
    y4 = _matmul(y3, p["w_pw2"], name=f"{tag}_conf_pw2")
    out = _rms_fwd(y4, p["g1"], res=h, bias=p["b_pw2"], name=f"{tag}_conf_res")
    return out, dict(h=h, u=u, z=z, y1=y1, y2=y2, y3=y3, y4=y4)


def _conformer_backward(dh, p, a, tag):
    dy4, dg1, db_pw2 = _rms_bwd(a["y4"], p["g1"], dh, bias=p["b_pw2"], out_dtype=MXU_DTYPE,
                                name=f"{tag}_conf_res_bwd")
    dy3 = _matmul(dy4, p["w_pw2"], tb=True, name=f"{tag}_conf_pw2_dx")
    dw_pw2 = _matmul(a["y3"], dy4, ta=True, name=f"{tag}_conf_pw2_dw")
    dy2, dln_g, dln_b = _lnsilu_bwd(a["y2"], p["ln_g"], p["ln_b"], dy3, name=f"{tag}_conf_ln_bwd")
    dy1, dw_dw, db_dw = _conv_bwd(a["y1"], p["w_dw"], dy2, name=f"{tag}_conf_dw_bwd")
    dz, db_pw1 = _glu_bwd(a["z"], p["b_pw1"], dy1, name=f"{tag}_conf_glu_bwd")
    du = _matmul(dz, p["w_pw1"], tb=True, name=f"{tag}_conf_pw1_dx")
    dw_pw1 = _matmul(a["u"], dz, ta=True, name=f"{tag}_conf_pw1_dw")
    dh_in, dg0 = _rms_bwd(a["h"], p["g0"], du, add=dh, name=f"{tag}_conf_norm_bwd")
    return dh_in, dict(g0=dg0, g1=dg1, w_pw1=dw_pw1, b_pw1=db_pw1, w_dw=dw_dw, b_dw=db_dw, ln_g=dln_g,
                       ln_b=dln_b, w_pw2=dw_pw2, b_pw2=db_pw2)


def _trunk_step(h0, target, w, later=None):
    acts, layers = [], []
    h = h0
    for li in range(DEPTH):
        if li == 1 and later is not None:
            w = later(w, h)
        tag, p = f"l{li}", _layer_operands(w, li)
        h, a_mix = (_mixer_forward if li % 2 == 0 else _conformer_forward)(h, p["mix"], tag)
        h, a_ffn = _ffn_forward(h, p["ffn"], tag)
        acts.append((a_mix, a_ffn))
        layers.append(p)
    dh, loss_cols = _loss_fwd_bwd(h, target, name="loss")
    grads = [None] * len(layers)
    for li in reversed(range(len(layers))):
        tag = f"l{li}"
        p = layers[li]
        dh, g_ffn = _ffn_backward(dh, p["ffn"], acts[li][1], tag)
        dh, g_mix = (_mixer_backward if li % 2 == 0 else _conformer_backward)(dh, p["mix"], acts[li][0], tag)
        grads[li] = dict(mix=g_mix, ffn=g_ffn)
    return loss_cols, dh, grads


_SPLIT = 2 * MQK + 2 * MV


def _layer_operands(w, li):
    row = lambda v: v[None, :].astype(F32)
    i = li // 2
    g = w["norm_g"][li].astype(F32)
    if li % 2 == 0:
        win = w["mix_w_in"][i]
        w_in = jnp.concatenate(
            [win[:, :_SPLIT], win[:, _SPLIT + 2 * MLSTM_HEADS:], win[:, _SPLIT:_SPLIT + 2 * MLSTM_HEADS],
             jnp.zeros((D_MODEL, PROJ_WIDTH - IN_WIDTH), win.dtype)], axis=1)
        gate_b = jnp.pad(row(w["mix_gate_b"][i]), ((0, 0), (0, LANE - 2 * MLSTM_HEADS)))
        mix = dict(g0=g[0:1], g1=g[1:2], w_in=w_in, qk_w=w["mix_qk_conv_w"][i].astype(F32),
                   qk_b=row(w["mix_qk_conv_b"][i]), gate_b=gate_b, hnorm_g=row(w["mix_hnorm_g"][i]),
                   w_out=w["mix_w_out"][i])
    else:
        mix = dict(g0=g[0:1], g1=g[1:2], w_pw1=w["conv_w_pw1"][i], b_pw1=row(w["conv_b_pw1"][i]),
                   w_dw=w["conv_w_dw"][i].astype(F32), b_dw=row(w["conv_b_dw"][i]),
                   ln_g=row(w["conv_ln_g"][i]), ln_b=row(w["conv_ln_b"][i]), w_pw2=w["conv_w_pw2"][i],
                   b_pw2=row(w["conv_b_pw2"][i]))
    ffn = dict(g2=g[2:3], g3=g[3:4], w_gate=w["ffn_w_gate"][li], w_up=w["ffn_w_up"][li],
               w_down=w["ffn_w_down"][li])
    return dict(mix=mix, ffn=ffn)


def _collect_grads(grads):
    even = [grads[li]["mix"] for li in range(0, DEPTH, 2)]
    odd = [grads[li]["mix"] for li in range(1, DEPTH, 2)]
    ffn = [grads[li]["ffn"] for li in range(DEPTH)]
    st = lambda xs: jnp.stack(xs, axis=0)
    vec = lambda xs, k: st([x[k][0] for x in xs])
    out = {}
    out["norm_g"] = st([jnp.concatenate([grads[li]["mix"]["g0"], grads[li]["mix"]["g1"], grads[li]["ffn"]["g2"],
                                         grads[li]["ffn"]["g3"]], axis=0) for li in range(DEPTH)])
    out["mix_w_in"] = st([jnp.concatenate(
        [g["w_in"][:, :_SPLIT], g["w_in"][:, GATE_COL:GATE_COL + 2 * MLSTM_HEADS], g["w_in"][:, _SPLIT:GATE_COL]],
        axis=1) for g in even])
    out["mix_qk_conv_w"] = st([g["qk_w"] for g in even])
    out["mix_qk_conv_b"] = vec(even, "qk_b")
    out["mix_gate_b"] = st([g["gate_b"][0, :2 * MLSTM_HEADS] for g in even])
    out["mix_hnorm_g"] = vec(even, "hnorm_g")
    out["mix_w_out"] = st([g["w_out"] for g in even])
    out["conv_w_pw1"] = st([g["w_pw1"] for g in odd])
    out["conv_b_pw1"] = vec(odd, "b_pw1")
    out["conv_w_dw"] = st([g["w_dw"] for g in odd])
    out["conv_b_dw"] = vec(odd, "b_dw")
    out["conv_ln_g"] = vec(odd, "ln_g")
    out["conv_ln_b"] = vec(odd, "ln_b")
    out["conv_w_pw2"] = st([g["w_pw2"] for g in odd])
    out["conv_b_pw2"] = vec(odd, "b_pw2")
    out["ffn_w_gate"] = st([g["w_gate"] for g in ffn])
    out["ffn_w_up"] = st([g["w_up"] for g in ffn])
    out["ffn_w_down"] = st([g["w_down"] for g in ffn])
    return out


def _local_step(x, target, w, later=None):
    seq = x.shape[0]
    h0 = jnp.concatenate([jnp.zeros((PAD_FRONT, D_MODEL), F32), w["meta"].astype(F32), x], axis=0)
    tgt = jnp.concatenate([jnp.zeros((PAD_FRONT + N_META, D_MODEL), F32), target], axis=0)
    loss_cols, dh0, grads = _trunk_step(h0, tgt, w, later)
    out = _collect_grads(grads)
    out["meta"] = dh0[PAD_FRONT:PAD_FRONT + N_META]
    loss = 0.5 * jnp.sum(loss_cols) / D_MODEL
    return loss, dh0[PAD_FRONT + N_META:PAD_FRONT + N_META + seq], out


def _elementwise(fn, arrays, out_dtypes, *, name):
    shape = arrays[0].shape
    cols = shape[-1]
    rows = 1
    for s in shape[:-1]:
        rows *= s
    flat = [a.reshape(rows, cols) for a in arrays]
    if rows * cols * 4 <= (1 << 20) or rows % SUBLANE:
        tr = rows
    else:
        tr = _divisor(rows, (512, 256, 128, 64, 32, 16, 8))
    n = len(flat)

    def body(*refs):
        outs = fn(*[r[...] for r in refs[:n]])
        for o_ref, o in zip(refs[n:], outs):
            o_ref[...] = o.astype(o_ref.dtype)

    spec = pl.BlockSpec((tr, cols), lambda i: (i, 0))
    outs = pl.pallas_call(
        body, name=name, grid=(rows // tr,), in_specs=[spec] * n, out_specs=[spec] * len(out_dtypes),
        out_shape=[jax.ShapeDtypeStruct((rows, cols), dt) for dt in out_dtypes],
        compiler_params=_params(("parallel",)),
    )(*flat)
    return [o.reshape(shape) for o in outs]


def _adamw(w, g, m, v, *, name):
    def fn(wv, gv, mv, vv):
        mn = ADAM_B1 * mv + (1.0 - ADAM_B1) * gv
        vn = ADAM_B2 * vv + (1.0 - ADAM_B2) * (gv * gv)
        m_hat = mn / (1.0 - ADAM_B1 ** ADAM_STEP)
        v_hat = vn / (1.0 - ADAM_B2 ** ADAM_STEP)
        return [-ADAM_LR * (m_hat / (jnp.sqrt(v_hat) + ADAM_EPS) + ADAM_WD * wv), mn, vn]

    return _elementwise(fn, [w, g, m, v], [F32, F32, F32], name=name)


MESH_ID = pl.DeviceIdType.MESH
ANY = pl.BlockSpec(memory_space=pl.ANY)


def _place():
    x, y, c = lax.axis_index("x"), lax.axis_index("y"), lax.axis_index("c")
    return x, y, c, [(1 - x, y), (x, 1 - y), (1 - x, 1 - y)]


def _remote(src, dst, send_sems, recv_sems, k, to):
    return pltpu.make_async_remote_copy(src_ref=src, dst_ref=dst, send_sem=send_sems.at[k], recv_sem=recv_sems.at[k],
                                        device_id=to, device_id_type=MESH_ID)


def _comm_call(body, arrays, out_shapes, n_remote, n_local, name):
    return pl.pallas_call(
        body, name=name, in_specs=[ANY] * len(arrays), out_specs=[ANY] * len(out_shapes), out_shape=out_shapes,
        scratch_shapes=[pltpu.SemaphoreType.DMA((n_remote,)), pltpu.SemaphoreType.DMA((n_remote,)),
                        pltpu.SemaphoreType.DMA((n_local,))],
        compiler_params=pltpu.CompilerParams(has_side_effects=True),
    )(*arrays)


def _gather_chips(shards, *, name):
    n = len(shards)

    def body(*refs):
        ins, outs = refs[:n], refs[n:2 * n]
        send_sems, recv_sems, local_sems = refs[2 * n:]
        x, y, c, chips = _place()
        me, sibling = 2 * x + y, (x, y, 1 - c)

        def half(a, slot, hc):
            hl = ins[a].shape[0] // 2
            return outs[a].at[slot].at[pl.ds(hc * hl, hl)]

        def mine(a):
            hl = ins[a].shape[0] // 2
            return ins[a].at[pl.ds(c * hl, hl)]

        sent = []
        for a in range(n):
            for j, (px, py) in enumerate(chips):
                sent.append(_remote(mine(a), half(a, me, c), send_sems, recv_sems, 6 * a + j, (px, py, c)))
                sent[-1].start()
        for a in range(n):
            for j, (px, py) in enumerate(chips):
                slot = 2 * px + py
                _remote(mine(a), half(a, slot, c), send_sems, recv_sems, 6 * a + j, (px, py, c)).wait_recv()
                sent.append(_remote(half(a, slot, c), half(a, slot, c), send_sems, recv_sems, 6 * a + 3 + j, sibling))
                sent[-1].start()
        for a in range(n):
            for j, (px, py) in enumerate(chips):
                slot = 2 * px + py
                _remote(mine(a), half(a, slot, 1 - c), send_sems, recv_sems, 6 * a + 3 + j, sibling).wait_recv()
        for cp in sent:
            cp.wait_send()

    out_shapes = [jax.ShapeDtypeStruct((4,) + s.shape, s.dtype) for s in shards]
    return _comm_call(body, shards, out_shapes, 6 * n, 1, name)


def _swap_siblings(arrays, *, by_core, name):
    n = len(arrays)

    def body(*refs):
        ins, outs = refs[:n], refs[n:2 * n]
        send_sems, recv_sems, _ = refs[2 * n:]
        x, y, c, _chips = _place()
        cps = [_remote(ins[a].at[1 - c] if by_core else ins[a], outs[a], send_sems, recv_sems, a, (x, y, 1 - c))
               for a in range(n)]
        for cp in cps:
            cp.start()
        for cp in cps:
            cp.wait()

    out_shapes = [jax.ShapeDtypeStruct(a.shape[1:] if by_core else a.shape, a.dtype) for a in arrays]
    return _comm_call(body, arrays, out_shapes, n, 1, name)


HBM = pl.BlockSpec(memory_space=pltpu.HBM)
SEM = pl.BlockSpec(memory_space=pltpu.SEMAPHORE)


def _gather_start(shards, after, *, name):
    n = len(shards)
    lands = [lax.empty((4,) + s.shape[1:], s.dtype) for s in shards]

    def body(*refs):
        ins, land = refs[:n], refs[n:2 * n]
        send_sems, recv_sems, token = refs[2 * n + 1], refs[2 * n + 2], refs[-1]
        x, y, c, chips = _place()
        for a in range(n):
            for j, (px, py) in enumerate(chips):
                _remote(ins[a].at[c], land[a].at[2 * x + y], send_sems, recv_sems, 3 * a + j, (px, py, c)).start()
        token[...] = jnp.zeros_like(token)

    hbm = lambda a: pltpu.with_memory_space_constraint(a, pltpu.HBM)
    out = pl.pallas_call(
        body, name=name,
        out_shape=(pltpu.SemaphoreType.DMA((3 * n,)), pltpu.SemaphoreType.DMA((3 * n,)),
                   *[pltpu.HBM(a.shape, a.dtype) for a in shards + lands], jax.ShapeDtypeStruct((SUBLANE, LANE), F32)),
        in_specs=[HBM] * (2 * n) + [ANY],
        out_specs=(SEM, SEM, *[HBM] * (2 * n), pl.BlockSpec(memory_space=pltpu.VMEM)),
        input_output_aliases={a: 2 + a for a in range(2 * n)},
        compiler_params=pltpu.CompilerParams(has_side_effects=pltpu.SideEffectType.DATAFLOW_SIDE_EFFECTING),
    )(*[hbm(a) for a in shards + lands], after)
    return out[0], out[1], list(out[2:2 + n]), list(out[2 + n:2 + 2 * n]), out[-1]


def _gather_wait(send_sems, recv_sems, shards, lands, after, *, name):
    n = len(shards)

    def body(*refs):
        ins, land, ssem, rsem = refs[:n], refs[n:2 * n], refs[2 * n], refs[2 * n + 1]
        x, y, c, chips = _place()
        for a in range(n):
            for j, (px, py) in enumerate(chips):
                cp = _remote(ins[a].at[c], land[a].at[2 * px + py], ssem, rsem, 3 * a + j, (px, py, c))
                cp.wait_send()
                cp.wait_recv()

    out = pl.pallas_call(
        body, name=name, out_shape=[pltpu.HBM(a.shape, a.dtype) for a in shards + lands],
        in_specs=[HBM] * (2 * n) + [SEM, SEM, ANY], out_specs=[HBM] * (2 * n),
        input_output_aliases={a: a for a in range(2 * n)},
        compiler_params=pltpu.CompilerParams(has_side_effects=pltpu.SideEffectType.DATAFLOW_SIDE_EFFECTING),
    )(*shards, *lands, send_sems, recv_sems, after)
    return list(out[n:])


def _scatter_chips(parts, small, *, name):
    n = len(parts)

    def body(*refs):
        ins, small_in = refs[:n], refs[n]
        outs, small_out = refs[n + 1:2 * n + 1], refs[2 * n + 1]
        send_sems, recv_sems, local_sems = refs[2 * n + 2:]
        x, y, c, chips = _place()
        me8 = 4 * x + 2 * y + c
        own = pltpu.make_async_copy(small_in, small_out.at[me8], local_sems.at[0])
        own.start()
        cps = []
        for fx in range(2):
            for fy in range(2):
                for fc in range(2):
                    r = 4 * fx + 2 * fy + fc - 1
                    if r >= 0:
                        to = (x + fx - 2 * x * fx, y + fy - 2 * y * fy, c + fc - 2 * c * fc)
                        cps.append(_remote(small_in, small_out.at[me8], send_sems, recv_sems, r, to))
        for a in range(n):
            for j, (px, py) in enumerate(chips):
                cps.append(_remote(ins[a].at[2 * px + py], outs[a].at[j], send_sems, recv_sems, 7 + 3 * a + j,
                                   (px, py, c)))
        for cp in cps:
            cp.start()
        for cp in cps:
            cp.wait()
        own.wait()

    out_shapes = [jax.ShapeDtypeStruct((3,) + p.shape[1:], p.dtype) for p in parts]
    out_shapes.append(jax.ShapeDtypeStruct((8,) + small.shape, small.dtype))
    return _comm_call(body, list(parts) + [small], out_shapes, 7 + 3 * n, 1, name)


WEIGHTS = ("meta", "norm_g", "mix_w_in", "mix_qk_conv_w", "mix_qk_conv_b", "mix_gate_b", "mix_hnorm_g", "mix_w_out",
           "conv_w_pw1", "conv_b_pw1", "conv_w_dw", "conv_b_dw", "conv_ln_g", "conv_ln_b", "conv_w_pw2",
           "conv_b_pw2", "ffn_w_gate", "ffn_w_up", "ffn_w_down")
SHARD_AXIS = dict(meta=1, norm_g=2, mix_w_in=2, mix_qk_conv_w=2, mix_qk_conv_b=None, mix_gate_b=None,
                  mix_hnorm_g=None, mix_w_out=1, conv_w_pw1=2, conv_b_pw1=1, conv_w_dw=2, conv_b_dw=1, conv_ln_g=1,
                  conv_ln_b=1, conv_w_pw2=1, conv_b_pw2=1, ffn_w_gate=2, ffn_w_up=2, ffn_w_down=1)
MATRICES = ("mix_w_in", "mix_w_out", "conv_w_pw1", "conv_w_pw2", "ffn_w_gate", "ffn_w_up", "ffn_w_down")
VECTORS = tuple(n for n in WEIGHTS if n not in MATRICES)
GATHER_COLS = D_MODEL // 4


def _pack_rows(arrays, cols, pad_to):
    rows = [a.astype(F32).reshape(-1) for a in arrays]
    rows = [jnp.pad(r, (0, (-r.shape[0]) % cols)).reshape(-1, cols) for r in rows]
    packed = jnp.concatenate(rows, axis=0)
    return jnp.pad(packed, ((0, pad_to - packed.shape[0]), (0, 0))), [r.shape[0] for r in rows]


def _unpack_rows(packed, counts, shapes):
    out, at = [], 0
    for n, shape in zip(counts, shapes):
        size = 1
        for s in shape:
            size *= s
        out.append(packed[..., at:at + n, :].reshape(packed.shape[:-2] + (-1,))[..., :size]
                   .reshape(packed.shape[:-2] + tuple(shape)))
        at += n
    return out


FIRST_LAYER = ("mix_w_in", "mix_w_out", "ffn_w_gate", "ffn_w_up", "ffn_w_down")
PACK_ROWS = 128


def _row_halves(a):
    return a.reshape((2, a.shape[0] * a.shape[1] // 2, a.shape[2]))


def _assemble(slots, own, name):
    me = 2 * lax.axis_index("x") + lax.axis_index("y")
    slots = lax.dynamic_update_index_in_dim(slots, own, me, 0)
    return jnp.concatenate([slots[k] for k in range(4)], axis=SHARD_AXIS[name])


def _gather_weights(local):
    c = lax.axis_index("c")
    sharded_vecs = [n for n in VECTORS if SHARD_AXIS[n] is not None]
    pack, counts = _pack_rows([local[n] for n in sharded_vecs], GATHER_COLS, PACK_ROWS)
    first = [local[n][:1].astype(MXU_DTYPE) for n in FIRST_LAYER]
    got = _gather_chips([_row_halves(s) for s in first] + [_row_halves(pack[None])], name="gather_first")
    w = {n: local[n] for n in VECTORS if SHARD_AXIS[n] is None}
    for n, s, g in zip(FIRST_LAYER, first, got):
        w[n] = [_assemble(g.reshape((4,) + s.shape), s, n)[0]]
    packs = lax.dynamic_update_index_in_dim(got[-1].reshape(4, PACK_ROWS, GATHER_COLS), pack,
                                            2 * lax.axis_index("x") + lax.axis_index("y"), 0)
    vecs = _unpack_rows(packs, counts, [local[n].shape for n in sharded_vecs])
    for n, v in zip(sharded_vecs, vecs):
        w[n] = jnp.moveaxis(v, 0, -2).reshape(v.shape[1:-1] + (4 * v.shape[-1],))

    rest = [(local[n][1:] if n in FIRST_LAYER else local[n]).astype(MXU_DTYPE) for n in MATRICES]
    send_sems, recv_sems, sent, lands, token = _gather_start([_row_halves(r) for r in rest], got[-1],
                                                             name="gather_rest_start")

    def later(w, after):
        mine = _gather_wait(send_sems, recv_sems, sent, lands, after, name="gather_rest_wait")
        theirs = _swap_siblings(mine, by_core=False, name="gather_rest_swap")
        w = dict(w)
        for n, r, m, t in zip(MATRICES, rest, mine, theirs):
            both = jnp.where(c == 0, jnp.concatenate([m, t], axis=1), jnp.concatenate([t, m], axis=1))
            layers = _assemble(both.reshape((4,) + r.shape), r, n)
            w[n] = list(w.get(n, [])) + [layers[k] for k in range(r.shape[0])]
        return w

    return w, later, token


def _reduce_grads(grads):
    x, y, c = lax.axis_index("x"), lax.axis_index("y"), lax.axis_index("c")
    me = 2 * x + y
    stacked = []
    for n in MATRICES:
        g = jnp.stack(jnp.split(grads[n], 4, axis=SHARD_AXIS[n]), axis=0)
        g = g.reshape((4, 2, g.shape[1] // 2) + g.shape[2:])
        stacked.append(jnp.swapaxes(g, 0, 1))
    theirs = _swap_siblings(stacked, by_core=True, name="reduce_pair_swap")
    pair = [_elementwise(lambda a, b: [a + b], [lax.dynamic_index_in_dim(s, c, 0, keepdims=False), t], [F32],
                         name=f"reduce_pair_sum_{n}")[0] for n, s, t in zip(MATRICES, stacked, theirs)]
    shapes = [grads[n].shape for n in VECTORS]
    pack, counts = _pack_rows([grads[n] for n in VECTORS], D_MODEL, 120)
    got = _scatter_chips([p.astype(jnp.bfloat16) for p in pair], pack, name="reduce_chips")
    halves = []
    for n, p, r in zip(MATRICES, pair, got[:-1]):
        own = lax.dynamic_index_in_dim(p, me, 0, keepdims=False)
        halves.append(_elementwise(lambda a, b0, b1, b2: [((a + b0.astype(F32)) + b1.astype(F32)) + b2.astype(F32)],
                                   [own, r[0], r[1], r[2]], [F32], name=f"reduce_chip_sum_{n}")[0])
    others = _swap_siblings(halves, by_core=False, name="reduce_join")
    out = {n: jnp.where(c == 0, jnp.concatenate([h, o], axis=0), jnp.concatenate([o, h], axis=0))
           for n, h, o in zip(MATRICES, halves, others)}
    small = got[-1]
    total = _elementwise(lambda *s: [functools.reduce(lambda a, b: a + b, s)], [small[k] for k in range(8)], [F32],
                         name="reduce_small_sum")[0]
    for n, v in zip(VECTORS, _unpack_rows(total, counts, shapes)):
        ax = SHARD_AXIS[n]
        if ax is not None:
            w = v.shape[ax] // 4
            v = lax.dynamic_slice_in_dim(v, me * w, w, axis=ax)
        out[n] = v
    return out


def kernel(x, meta, norm_g, mix_w_in, mix_qk_conv_w, mix_qk_conv_b, mix_gate_b, mix_hnorm_g, mix_w_out, conv_w_pw1, conv_b_pw1, conv_w_dw, conv_b_dw, conv_ln_g, conv_ln_b, conv_w_pw2, conv_b_pw2, ffn_w_gate, ffn_w_up, ffn_w_down, loss_target, m_meta, m_norm_g, m_mix_w_in, m_mix_qk_conv_w, m_mix_qk_conv_b, m_mix_gate_b, m_mix_hnorm_g, m_mix_w_out, m_conv_w_pw1, m_conv_b_pw1, m_conv_w_dw, m_conv_b_dw, m_conv_ln_g, m_conv_ln_b, m_conv_w_pw2, m_conv_b_pw2, m_ffn_w_gate, m_ffn_w_up, m_ffn_w_down, v_meta, v_norm_g, v_mix_w_in, v_mix_qk_conv_w, v_mix_qk_conv_b, v_mix_gate_b, v_mix_hnorm_g, v_mix_w_out, v_conv_w_pw1, v_conv_b_pw1, v_conv_w_dw, v_conv_b_dw, v_conv_ln_g, v_conv_ln_b, v_conv_w_pw2, v_conv_b_pw2, v_ffn_w_gate, v_ffn_w_up, v_ffn_w_down):
    given = dict(locals())
    local = {n: given[n] for n in WEIGHTS}
    first, later, token = _gather_weights(local)
    loss, grad_x, grads = _local_step(x[0] + token[0, 0], loss_target[0], first, later)
    loss = lax.psum(loss, ("x", "y", "c"))
    grad_w = _reduce_grads(grads)
    delta, new_m, new_v = {}, {}, {}
    for n in WEIGHTS:
        delta[n], new_m[n], new_v[n] = _adamw(local[n], grad_w[n], given["m_" + n], given["v_" + n], name=f"adamw_{n}")
    return (loss, grad_x[None], *[grad_w[n] for n in WEIGHTS], *[delta[n] for n in WEIGHTS],
            *[new_m[n] for n in WEIGHTS], *[new_v[n] for n in WEIGHTS])
```

```python
import functools

import jax
import jax.numpy as jnp
from jax import lax
from jax.experimental import pallas as pl
from jax.experimental.pallas import tpu as pltpu

F32 = jnp.float32
MXU_DTYPE = jnp.bfloat16

D_MODEL = 1024
N_META = 16
DEPTH = 4
MLSTM_HEADS = 4
MLSTM_DQK = 128
MLSTM_DV = 256
MLSTM_CHUNK = 64
QK_CONV_WIDTH = 4
GATE_SOFTCAP = 15.0
SB_HEADS = 4
SB_DH = 128
SB_BLOCK = 128
PAD_FRONT = SB_BLOCK - N_META
CONV_WIDTH = 31
FFN_HIDDEN = 2816
MQK = MLSTM_HEADS * MLSTM_DQK
MV = MLSTM_HEADS * MLSTM_DV
SBW = SB_HEADS * SB_DH
IN_WIDTH = 2 * MQK + 2 * MV + 2 * MLSTM_HEADS + 3 * SBW
MIX_WIDTH = MV + SBW
NEG = -1e30
EPS = 1e-6
PROJ_WIDTH = 5120
GATE_COL = 3 * MV + 3 * SBW
LANE = 128
SUBLANE = 8
CONV_HALO = 32
VMEM_LIMIT = 56 * 1024 * 1024

ADAM_LR = 0.001
ADAM_B1 = 0.9
ADAM_B2 = 0.999
ADAM_EPS = 1e-08
ADAM_WD = 0.01
ADAM_STEP = 10


def _divisor(n, cands):
    for c in cands:
        if n % c == 0:
            return c
    raise ValueError(f"no tile for {n} in {cands}")


ROW_TILE_BYTES = 20 * 1024 * 1024


def _row_tile(tp, width=D_MODEL):
    for c in (640, 512, 384, 320, 256, 128, 64):
        if tp % c == 0 and c * width * 8 <= ROW_TILE_BYTES:
            return c
    raise ValueError(f"no row tile for {tp} x {width}")


def _params(sem):
    return pltpu.CompilerParams(dimension_semantics=sem, vmem_limit_bytes=VMEM_LIMIT)


def _dot(a, b, dims):
    return lax.dot_general(a.astype(MXU_DTYPE), b.astype(MXU_DTYPE), (dims, ((), ())),
                           preferred_element_type=F32)


def _nn(a, b):
    return _dot(a, b, ((1,), (0,)))


def _nt(a, b):
    return _dot(a, b, ((1,), (1,)))


def _tn(a, b):
    return _dot(a, b, ((0,), (0,)))


def _sigmoid(x):
    return 1.0 / (1.0 + jnp.exp(-x))


def _softplus(x):
    return jnp.maximum(x, 0.0) + jnp.log(1.0 + jnp.exp(-jnp.abs(x)))


MATMUL_VMEM_BYTES = 40 * 1024 * 1024


def _matmul_tiles(m, n, k, a_bytes, b_bytes):
    tm = _divisor(m, (1040, 1024, 1408, 768, 640, 512, 384, 256, 128))
    tn = _divisor(n, (1408, 1280, 1024, 768, 512, 256, 128))
    for tk in (5632, 5120, 2816, 2560, 2048, 1664, 1536, 1408, 1280, 1040, 1024, 768, 640, 512, 384, 256, 128):
        if k % tk:
            continue
        need = 2 * (tm * tk * a_bytes + tk * tn * b_bytes + tm * tn * 4) + (tm * tn * 4 if tk < k else 0)
        if need <= MATMUL_VMEM_BYTES:
            return tm, tn, tk
    raise ValueError(f"no matmul tiles for {m}x{n}x{k}")


def _matmul(a, b, *, ta=False, tb=False, name):
    m, k = (a.shape[1], a.shape[0]) if ta else a.shape
    n = b.shape[0] if tb else b.shape[1]
    assert (b.shape[1] if tb else b.shape[0]) == k, (a.shape, b.shape, ta, tb)
    tm, tn, tk = _matmul_tiles(m, n, k, a.dtype.itemsize, b.dtype.itemsize)
    nk = k // tk
    dims = ((0 if ta else 1,), (1 if tb else 0,))

    def body(a_ref, b_ref, o_ref, *acc):
        if nk == 1:
            o_ref[...] = _dot(a_ref[...], b_ref[...], dims)
            return
        acc_ref, kk = acc[0], pl.program_id(2)

        @pl.when(kk == 0)
        def _():
            acc_ref[...] = jnp.zeros_like(acc_ref)

        acc_ref[...] += _dot(a_ref[...], b_ref[...], dims)

        @pl.when(kk == nk - 1)
        def _():
            o_ref[...] = acc_ref[...]

    a_spec = (pl.BlockSpec((tk, tm), lambda i, j, kk: (kk, i)) if ta
              else pl.BlockSpec((tm, tk), lambda i, j, kk: (i, kk)))
    b_spec = (pl.BlockSpec((tn, tk), lambda i, j, kk: (j, kk)) if tb
              else pl.BlockSpec((tk, tn), lambda i, j, kk: (kk, j)))
    return pl.pallas_call(
        body, name=name, grid=(m // tm, n // tn, nk),
        in_specs=[a_spec, b_spec],
        out_specs=pl.BlockSpec((tm, tn), lambda i, j, kk: (i, j)),
        out_shape=jax.ShapeDtypeStruct((m, n), F32),
        scratch_shapes=[pltpu.VMEM((tm, tn), F32)] if nk > 1 else [],
        compiler_params=_params(("parallel", "parallel", "arbitrary")),
    )(a, b)


def _rowwise(fn, rows, fulls, out_rows, out_accs, *, name, out_dtypes=None):
    tp = rows[0][0].shape[0]
    tm = _row_tile(tp, sum(w for _, _, w in rows) + sum(out_rows))
    nr, nf, no, na = len(rows), len(fulls), len(out_rows), len(out_accs)
    out_dtypes = out_dtypes or [F32] * no

    def body(*refs):
        i = pl.program_id(0)
        outs = fn(i * tm, *[r[...].astype(F32) for r in refs[:nr + nf]])
        for k in range(no):
            refs[nr + nf + k][...] = outs[k].astype(out_dtypes[k])
        for k in range(na):
            ref = refs[nr + nf + no + k]

            @pl.when(i == 0)
            def _(ref=ref):
                ref[...] = jnp.zeros_like(ref)

            ref[...] += outs[no + k]

    in_specs = [pl.BlockSpec((tm, w), functools.partial(lambda i, cb: (i, cb), cb=cb)) for _, cb, w in rows]
    in_specs += [pl.BlockSpec(f.shape, lambda i: (0, 0)) for f in fulls]
    out_specs = [pl.BlockSpec((tm, w), lambda i: (i, 0)) for w in out_rows]
    out_specs += [pl.BlockSpec(s, lambda i: (0, 0)) for s in out_accs]
    out_shape = [jax.ShapeDtypeStruct((tp, w), dt) for w, dt in zip(out_rows, out_dtypes)]
    out_shape += [jax.ShapeDtypeStruct(s, F32) for s in out_accs]
    return pl.pallas_call(
        body, name=name, grid=(tp // tm,), in_specs=in_specs, out_specs=out_specs, out_shape=out_shape,
        compiler_params=_params(("arbitrary",)),
    )(*[r[0] for r in rows], *fulls)


def _whole(a):
    return (a, 0, a.shape[1])


def _live(row0, tm):
    return (row0 + lax.broadcasted_iota(jnp.int32, (tm, 1), 0)) >= PAD_FRONT


def _rms_core(x, g):
    r = lax.rsqrt(jnp.mean(x * x, axis=-1, keepdims=True) + EPS)
    return x * r, r


def _rms_fwd(x, g, *, name, res=None, bias=None, out_dtype=F32):
    def fn(row0, *blk):
        it = iter(blk)
        xv = next(it)
        rv = next(it) if res is not None else None
        gv = next(it)
        if bias is not None:
            xv = xv + next(it)
        xh, _ = _rms_core(xv, gv)
        y = jnp.where(_live(row0, xv.shape[0]), xh * gv, 0.0)
        return [y + rv if rv is not None else y]

    rows = [_whole(x)] + ([_whole(res)] if res is not None else [])
    fulls = [g] + ([bias] if bias is not None else [])
    return _rowwise(fn, rows, fulls, [x.shape[1]], [], name=name, out_dtypes=[out_dtype])[0]


def _rms_bwd(x, g, dy, *, name, add=None, bias=None, dy2=None, out_dtype=F32):
    def fn(row0, *blk):
        it = iter(blk)
        xv, dyv = next(it), next(it)
        if dy2 is not None:
            dyv = dyv + next(it)
        av = next(it) if add is not None else None
        gv = next(it)
        if bias is not None:
            xv = xv + next(it)
        dyv = jnp.where(_live(row0, xv.shape[0]), dyv, 0.0)
        xh, r = _rms_core(xv, gv)
        dyg = dyv * gv
        dx = r * (dyg - xh * jnp.mean(dyg * xh, axis=-1, keepdims=True))
        outs = [dx + av if av is not None else dx, jnp.sum(dyv * xh, axis=0, keepdims=True)]
        if bias is not None:
            outs.append(jnp.sum(dx, axis=0, keepdims=True))
        return outs

    rows = [_whole(x), _whole(dy)] + ([_whole(dy2)] if dy2 is not None else []) \
        + ([_whole(add)] if add is not None else [])
    fulls = [g] + ([bias] if bias is not None else [])
    c = x.shape[1]
    return _rowwise(fn, rows, fulls, [c], [(1, c)] * (2 if bias is not None else 1), name=name,
                    out_dtypes=[out_dtype])


def _ffn_tiles(m, n):
    return _divisor(m, (640, 384, 256, 128)), _divisor(n, (1408, 1024, 768, 512, 256, 128))


def _ffn_in(u, wg, wu, *, name):
    (m, k), n = u.shape, wg.shape[1]
    tm, tn = _ffn_tiles(m, n)

    def body(u_ref, wg_ref, wu_ref, a_ref, b_ref, s_ref):
        x = u_ref[...]
        a, b = _nn(x, wg_ref[...]), _nn(x, wu_ref[...])
        a_ref[...] = a
        b_ref[...] = b
        s_ref[...] = (a * _sigmoid(a) * b).astype(s_ref.dtype)

    w_spec = pl.BlockSpec((k, tn), lambda i, j: (0, j))
    o_spec = pl.BlockSpec((tm, tn), lambda i, j: (i, j))
    return pl.pallas_call(
        body, name=name, grid=(m // tm, n // tn),
        in_specs=[pl.BlockSpec((tm, k), lambda i, j: (i, 0)), w_spec, w_spec], out_specs=[o_spec] * 3,
        out_shape=[jax.ShapeDtypeStruct((m, n), F32), jax.ShapeDtypeStruct((m, n), F32),
                   jax.ShapeDtypeStruct((m, n), MXU_DTYPE)],
        compiler_params=_params(("parallel", "parallel")),
    )(u, wg, wu)


def _ffn_down_dx(df, wd, a, b, *, name):
    (m, k), n = df.shape, wd.shape[0]
    tm, tn = _ffn_tiles(m, n)

    def body(d_ref, w_ref, a_ref, b_ref, da_ref, db_ref):
        ds = _nt(d_ref[...], w_ref[...])
        av, bv = a_ref[...], b_ref[...]
        sg = _sigmoid(av)
        da_ref[...] = (ds * bv * sg * (1.0 + av * (1.0 - sg))).astype(da_ref.dtype)
        db_ref[...] = (ds * av * sg).astype(db_ref.dtype)

    t_spec = pl.BlockSpec((tm, tn), lambda i, j: (i, j))
    return pl.pallas_call(
        body, name=name, grid=(m // tm, n // tn),
        in_specs=[pl.BlockSpec((tm, k), lambda i, j: (i, 0)), pl.BlockSpec((tn, k), lambda i, j: (j, 0)), t_spec,
                  t_spec],
        out_specs=[t_spec, t_spec], out_shape=[jax.ShapeDtypeStruct((m, n), MXU_DTYPE)] * 2,
        compiler_params=_params(("parallel", "parallel")),
    )(df, wd, a, b)


def _glu_fwd(z, b, *, name):
    h = z.shape[1] // 2

    def fn(row0, a, gt, bv):
        y = (a + bv[:, :h]) * _sigmoid(gt + bv[:, h:])
        return [jnp.where(_live(row0, a.shape[0]), y, 0.0)]

    return _rowwise(fn, [(z, 0, h), (z, 1, h)], [b], [h], [], name=name)[0]


def _glu_bwd(z, b, dy, *, name):
    h = z.shape[1] // 2

    def fn(row0, a, gt, d, bv):
        d = jnp.where(_live(row0, a.shape[0]), d, 0.0)
        sg = _sigmoid(gt + bv[:, h:])
        dz = jnp.concatenate([d * sg, d * (a + bv[:, :h]) * sg * (1.0 - sg)], axis=1)
        return [dz, jnp.sum(dz, axis=0, keepdims=True)]

    return _rowwise(fn, [(z, 0, h), (z, 1, h), _whole(dy)], [b], [2 * h], [(1, 2 * h)], name=name,
                    out_dtypes=[MXU_DTYPE])


def _ln_core(x):
    mu = jnp.mean(x, axis=-1, keepdims=True)
    xc = x - mu
    r = lax.rsqrt(jnp.mean(xc * xc, axis=-1, keepdims=True) + EPS)
    return xc * r, r


def _lnsilu_fwd(x, g, b, *, name):
    def fn(row0, xv, gv, bv):
        xh, _ = _ln_core(xv)
        v = xh * gv + bv
        return [v * _sigmoid(v)]

    return _rowwise(fn, [_whole(x)], [g, b], [x.shape[1]], [], name=name, out_dtypes=[MXU_DTYPE])[0]


def _lnsilu_bwd(x, g, b, dy, *, name):
    def fn(row0, xv, d, gv, bv):
        xh, r = _ln_core(xv)
        v = xh * gv + bv
        sg = _sigmoid(v)
        dv = d * sg * (1.0 + v * (1.0 - sg))
        dxh = dv * gv
        dx = r * (dxh - jnp.mean(dxh, axis=-1, keepdims=True) - xh * jnp.mean(dxh * xh, axis=-1, keepdims=True))
        return [dx, jnp.sum(dv * xh, axis=0, keepdims=True), jnp.sum(dv, axis=0, keepdims=True)]

    c = x.shape[1]
    return _rowwise(fn, [_whole(x), _whole(dy)], [g, b], [c], [(1, c), (1, c)], name=name)


def _silu_fwd(x, *, name):
    return _rowwise(lambda row0, v: [v * _sigmoid(v)], [_whole(x)], [], [x.shape[1]], [], name=name)[0]


def _silu_bwd(x, dy, *, name):
    def fn(row0, v, d):
        sg = _sigmoid(v)
        return [d * sg * (1.0 + v * (1.0 - sg))]

    return _rowwise(fn, [_whole(x), _whole(dy)], [], [x.shape[1]], [], name=name)[0]


def _gate_parts(row0, pg, gb):
    lane = lax.broadcasted_iota(jnp.int32, pg.shape, 1)
    th = jnp.tanh((pg + gb) / GATE_SOFTCAP)
    s = GATE_SOFTCAP * th
    return lane, th, s, _live(row0, pg.shape[0])


def _gates_fwd(proj, gate_b, *, name):
    def fn(row0, pg, gb):
        lane, th, s, live = _gate_parts(row0, pg, gb)
        li = jnp.where(live, s, NEG)
        lf = jnp.where(live, -_softplus(-s), 0.0)
        return [jnp.where(lane < MLSTM_HEADS, li, jnp.where(lane < 2 * MLSTM_HEADS, lf, 0.0))]

    return _rowwise(fn, [(proj, GATE_COL // LANE, LANE)], [gate_b], [LANE], [], name=name)[0]


def _gates_bwd(proj, gate_b, dgl, *, name):
    def fn(row0, pg, d, gb):
        lane, th, s, live = _gate_parts(row0, pg, gb)
        ds = jnp.where(lane < MLSTM_HEADS, d, d * _sigmoid(-s))
        ds = jnp.where(live & (lane < 2 * MLSTM_HEADS), ds, 0.0)
        dp = ds * (1.0 - th * th)
        return [dp, jnp.sum(dp, axis=0, keepdims=True)]

    return _rowwise(fn, [(proj, GATE_COL // LANE, LANE), _whole(dgl)], [gate_b], [LANE], [(1, LANE)], name=name)


def _head_rms(h):
    parts = [h[:, i * MLSTM_DV:(i + 1) * MLSTM_DV] for i in range(MLSTM_HEADS)]
    rs = [lax.rsqrt(jnp.mean(p * p, axis=-1, keepdims=True) + EPS) for p in parts]
    return parts, rs


def _hnorm_fwd(hm, proj, g, *, name):
    def fn(row0, h, o, gv):
        parts, rs = _head_rms(h)
        xh = jnp.concatenate([p * r for p, r in zip(parts, rs)], axis=1)
        return [xh * gv * _sigmoid(o)]

    return _rowwise(fn, [_whole(hm), (proj, 2, MV)], [g], [MV], [], name=name)[0]


def _hnorm_bwd(hm, proj, g, dmixed, *, name):
    def fn(row0, h, o, d, gv):
        parts, rs = _head_rms(h)
        so = _sigmoid(o)
        dn = d * so
        dxs, xhs = [], []
        for i, (p, r) in enumerate(zip(parts, rs)):
            sl = slice(i * MLSTM_DV, (i + 1) * MLSTM_DV)
            xh = p * r
            dyg = dn[:, sl] * gv[:, sl]
            dxs.append(r * (dyg - xh * jnp.mean(dyg * xh, axis=-1, keepdims=True)))
            xhs.append(xh)
        xh = jnp.concatenate(xhs, axis=1)
        return [jnp.concatenate(dxs, axis=1), d * xh * gv * so * (1.0 - so), jnp.sum(dn * xh, axis=0, keepdims=True)]

    return _rowwise(fn, [_whole(hm), (proj, 2, MV), (dmixed, 0, MV)], [g], [MV, MV], [(1, MV)], name=name)


def _loss_fwd_bwd(h, target, *, name):
    first = PAD_FRONT + N_META

    def fn(row0, hv, tv):
        rows = row0 + lax.broadcasted_iota(jnp.int32, (hv.shape[0], 1), 0)
        e = jnp.where(rows >= first, hv - tv, 0.0)
        return [e * (1.0 / D_MODEL), jnp.sum(e * e, axis=0, keepdims=True)]

    return _rowwise(fn, [_whole(h), _whole(target)], [], [D_MODEL], [(1, D_MODEL)], name=name)


CONV_SUB = 64


def _conv_tiles(tp, c):
    return _row_tile(tp), _divisor(c, (256, 128))


def _conv_shift(win, shifted, tm):
    n = tm + CONV_HALO - SUBLANE
    for b in range(1, SUBLANE):
        shifted[b - 1, :, :] = win[b:b + n, :]


def _conv_window(win, shifted, offset, r0, rows):
    a, b = divmod(offset, SUBLANE)
    lo = a * SUBLANE + r0
    return win[lo:lo + rows, :] if b == 0 else shifted[b - 1, lo:lo + rows, :]


def _conv_fwd(x, w, b, *, name):
    tp, (k, c) = x.shape[0], w.shape
    tm, tc = _conv_tiles(tp, c)
    base = CONV_HALO - (k - 1)

    def body(x_ref, xp_ref, w_ref, b_ref, o_ref, win, shifted):
        i = pl.program_id(1)
        win[0:CONV_HALO, :] = jnp.where(i > 0, _mxu_rounded(xp_ref[tm - CONV_HALO:tm, :]), 0.0)
        win[CONV_HALO:CONV_HALO + tm, :] = _mxu_rounded(x_ref[...])
        _conv_shift(win, shifted, tm)
        for r0 in range(0, tm, CONV_SUB):
            acc = jnp.broadcast_to(b_ref[...], (CONV_SUB, tc))
            for j in range(k):
                acc = acc + _mxu_rounded(w_ref[j:j + 1, :]) *_conv_window(win, shifted, base + j, r0, CONV_SUB)
            o_ref[r0:r0 + CONV_SUB, :] = acc

    return pl.pallas_call(
        body, name=name, grid=(c // tc, tp // tm),
        in_specs=[pl.BlockSpec((tm, tc), lambda cc, i: (i, cc)),
                  pl.BlockSpec((tm, tc), lambda cc, i: (jnp.maximum(i - 1, 0), cc)),
                  pl.BlockSpec((k, tc), lambda cc, i: (0, cc)),
                  pl.BlockSpec((1, tc), lambda cc, i: (0, cc))],
        out_specs=pl.BlockSpec((tm, tc), lambda cc, i: (i, cc)),
        out_shape=jax.ShapeDtypeStruct((tp, c), F32),
        scratch_shapes=[pltpu.VMEM((CONV_HALO + tm, tc), F32),
                        pltpu.VMEM((SUBLANE - 1, CONV_HALO + tm - SUBLANE, tc), F32)],
        compiler_params=_params(("parallel", "arbitrary")),
    )(x, x, w, b)


def _conv_bwd(x, w, dy, *, name):
    tp, (k, c) = x.shape[0], w.shape
    tm, tc = _conv_tiles(tp, c)
    nt = tp // tm
    base = CONV_HALO - (k - 1)

    def body(x_ref, xp_ref, d_ref, dn_ref, w_ref, dx_ref, dw_ref, db_ref, winx, wind, shx, shd):
        i = pl.program_id(1)
        winx[0:CONV_HALO, :] = jnp.where(i > 0, _mxu_rounded(xp_ref[tm - CONV_HALO:tm, :]), 0.0)
        winx[CONV_HALO:CONV_HALO + tm, :] = _mxu_rounded(x_ref[...])
        d = d_ref[...]
        wind[0:tm, :] = _mxu_rounded(d)
        wind[tm:tm + CONV_HALO, :] = jnp.where(i < nt - 1, _mxu_rounded(dn_ref[0:CONV_HALO, :]), 0.0)
        _conv_shift(winx, shx, tm)
        _conv_shift(wind, shd, tm)

        @pl.when(i == 0)
        def _():
            dw_ref[...] = jnp.zeros_like(dw_ref)
            db_ref[...] = jnp.zeros_like(db_ref)

        for r0 in range(0, tm, CONV_SUB):
            acc = jnp.zeros((CONV_SUB, tc), F32)
            for j in range(k):
                acc = acc + _mxu_rounded(w_ref[j:j + 1, :]) *_conv_window(wind, shd, k - 1 - j, r0, CONV_SUB)
            dx_ref[r0:r0 + CONV_SUB, :] = acc
        for j in range(k):
            part = jnp.zeros((SUBLANE, tc), F32)
            for r0 in range(0, tm, CONV_SUB):
                p = wind[r0:r0 + CONV_SUB, :] * _conv_window(winx, shx, base + j, r0, CONV_SUB)
                part = part + jnp.sum(p.reshape(CONV_SUB // SUBLANE, SUBLANE, tc), axis=0)
            dw_ref[j:j + 1, :] += jnp.sum(part, axis=0, keepdims=True)
        db_ref[...] += jnp.sum(d, axis=0, keepdims=True)

    return pl.pallas_call(
        body, name=name, grid=(c // tc, nt),
        in_specs=[pl.BlockSpec((tm, tc), lambda cc, i: (i, cc)),
                  pl.BlockSpec((tm, tc), lambda cc, i: (jnp.maximum(i - 1, 0), cc)),
                  pl.BlockSpec((tm, tc), lambda cc, i: (i, cc)),
                  pl.BlockSpec((tm, tc), lambda cc, i: (jnp.minimum(i + 1, nt - 1), cc)),
                  pl.BlockSpec((k, tc), lambda cc, i: (0, cc))],
        out_specs=[pl.BlockSpec((tm, tc), lambda cc, i: (i, cc)),
                   pl.BlockSpec((k, tc), lambda cc, i: (0, cc)),
                   pl.BlockSpec((1, tc), lambda cc, i: (0, cc))],
        out_shape=[jax.ShapeDtypeStruct((tp, c), F32), jax.ShapeDtypeStruct((k, c), F32),
                   jax.ShapeDtypeStruct((1, c), F32)],
        scratch_shapes=[pltpu.VMEM((CONV_HALO + tm, tc), F32), pltpu.VMEM((CONV_HALO + tm, tc), F32),
                        pltpu.VMEM((SUBLANE - 1, CONV_HALO + tm - SUBLANE, tc), F32),
                        pltpu.VMEM((SUBLANE - 1, CONV_HALO + tm - SUBLANE, tc), F32)],
        compiler_params=_params(("parallel", "arbitrary")),
    )(x, x, dy, dy, w)


def _chunk_masks():
    L = MLSTM_CHUNK
    r = lax.broadcasted_iota(jnp.int32, (L, L), 0)
    c = lax.broadcasted_iota(jnp.int32, (L, L), 1)
    return r == c, c <= r, r <= c


def _to_row(col, eye):
    return jnp.sum(jnp.where(eye, col, 0.0), axis=0, keepdims=True)


def _to_col(row, eye):
    return jnp.sum(jnp.where(eye, row, 0.0), axis=1, keepdims=True)


def _mxu_rounded(x):
    return x.astype(MXU_DTYPE).astype(F32)


def _mlstm_group(nc):
    return _divisor(nc, (5, 4, 3, 2, 1))


def _each(f, *lists):
    return [f(*args) for args in zip(*lists)]


def _rsum(x):
    return jnp.sum(x, axis=1, keepdims=True)


def _csum(x):
    return jnp.sum(x, axis=0, keepdims=True)


def _mlstm_chunk(q, k, v, li_c, lf_c, c_st, n_st, m_st, masks):
    eye, low, up = masks
    li_r = _each(lambda c: _to_row(c, eye), li_c)
    lf_r = _each(lambda c: _to_row(c, eye), lf_c)
    b_c = _each(lambda r: _rsum(jnp.where(low, r, 0.0)), lf_r)
    b_r = _each(lambda c: _csum(jnp.where(up, c, 0.0)), lf_c)
    g = _each(_csum, lf_c)
    dm = _each(lambda bc, br, lr: jnp.where(low, bc - br + lr, NEG), b_c, b_r, li_r)
    inter = _each(lambda bc, m: bc + m, b_c, m_st)
    mt = _each(lambda i, d: jnp.maximum(i, jnp.max(d, axis=1, keepdims=True)), inter, dm)
    wi = _each(lambda d, m: jnp.exp(d - m), dm, mt)
    wint = _each(lambda i, m: jnp.exp(i - m), inter, mt)
    qk_ = _each(_nt, q, k)
    qc = _each(_nn, q, c_st)
    s = _each(lambda a, w: a * w, qk_, wi)
    qn = _each(lambda a, n: _rsum(_mxu_rounded(a) * _mxu_rounded(n)), q, n_st)
    sv = _each(_nn, s, v)
    num = _each(lambda a, w, b: a + w * b, sv, wint, qc)
    den = _each(lambda a, w, b: _rsum(a) + w * b, s, wint, qn)
    floor = _each(lambda m: jnp.exp(-m), mt)
    a_c = _each(lambda gg, b, l: gg - b + l, g, b_c, li_c)
    a_r = _each(lambda gg, b, l: gg - b + l, g, b_r, li_r)
    mnew = _each(lambda gg, m, a: jnp.maximum(gg + m, jnp.max(a, axis=1, keepdims=True)), g, m_st, a_r)
    wa_c = _each(lambda a, m: jnp.exp(a - m), a_c, mnew)
    wc = _each(lambda gg, m, mn: jnp.exp(gg + m - mn), g, m_st, mnew)
    return dict(wi=wi, wint=wint, s=s, qc=qc, qn=qn, num=num, den=den, floor=floor, mnew=mnew, wa_c=wa_c, wc=wc)


def _mlstm_heads(qk_ref, v_ref, gl_ref, rows):
    H, dk, dv = MLSTM_HEADS, MLSTM_DQK, MLSTM_DV
    gates = gl_ref[rows, :]
    return ([qk_ref[rows, h * dk:(h + 1) * dk] * (dk ** -0.5) for h in range(H)],
            [qk_ref[rows, MQK + h * dk:MQK + (h + 1) * dk] for h in range(H)],
            [v_ref[rows, h * dv:(h + 1) * dv] for h in range(H)],
            [gates[:, h:h + 1] for h in range(H)], [gates[:, H + h:H + h + 1] for h in range(H)])


def _mlstm_fwd(qk, proj, gl, *, name):
    tp = qk.shape[0]
    L, H, dk, dv = MLSTM_CHUNK, MLSTM_HEADS, MLSTM_DQK, MLSTM_DV
    nc = tp // L
    G = _mlstm_group(nc)

    def body(qk_ref, v_ref, gl_ref, h_ref, call_ref, nall_ref, mall_ref, c_s, n_s, m_s):
        @pl.when(pl.program_id(0) == 0)
        def _():
            c_s[...] = jnp.zeros_like(c_s)
            n_s[...] = jnp.zeros_like(n_s)
            m_s[...] = jnp.zeros_like(m_s)

        masks = _chunk_masks()
        heads = range(H)
        c_st, n_st, m_row = [c_s[h] for h in heads], [n_s[h] for h in heads], [m_s[h] for h in heads]
        for ci in range(G):
            rows = slice(ci * L, (ci + 1) * L)
            for h in heads:
                call_ref[ci, h], nall_ref[ci, h], mall_ref[ci, h] = c_st[h], n_st[h], m_row[h]
            q, k, v, li, lf = _mlstm_heads(qk_ref, v_ref, gl_ref, rows)
            f = _mlstm_chunk(q, k, v, li, lf, c_st, n_st, [m[:, 0:1] for m in m_row], masks)
            out = _each(lambda a, b, c: a / jnp.maximum(jnp.abs(b), c), f["num"], f["den"], f["floor"])
            for h in heads:
                h_ref[rows, h * dv:(h + 1) * dv] = out[h]
            kv = _each(_tn, _each(lambda a, w: a * w, k, f["wa_c"]), v)
            c_st = _each(lambda w, c, x: w * c + x, f["wc"], c_st, kv)
            n_st = _each(lambda w, n, a, b: w * n + _csum(_mxu_rounded(a) * _mxu_rounded(b)), f["wc"], n_st, k, f["wa_c"])
            m_row = _each(lambda m: jnp.broadcast_to(m, (1, LANE)), f["mnew"])
        for h in heads:
            c_s[h], n_s[h], m_s[h] = c_st[h], n_st[h], m_row[h]

    return pl.pallas_call(
        body, name=name, grid=(nc // G,),
        in_specs=[pl.BlockSpec((G * L, 2 * MQK), lambda i: (i, 0)),
                  pl.BlockSpec((G * L, MV), lambda i: (i, 1)),
                  pl.BlockSpec((G * L, LANE), lambda i: (i, 0))],
        out_specs=[pl.BlockSpec((G * L, MV), lambda i: (i, 0)),
                   pl.BlockSpec((G, H, dk, dv), lambda i: (i, 0, 0, 0)),
                   pl.BlockSpec((G, H, 1, dk), lambda i: (i, 0, 0, 0)),
                   pl.BlockSpec((G, H, 1, LANE), lambda i: (i, 0, 0, 0))],
        out_shape=[jax.ShapeDtypeStruct((tp, MV), F32),
                   jax.ShapeDtypeStruct((nc, H, dk, dv), F32),
                   jax.ShapeDtypeStruct((nc, H, 1, dk), F32),
                   jax.ShapeDtypeStruct((nc, H, 1, LANE), F32)],
        scratch_shapes=[pltpu.VMEM((H, dk, dv), F32), pltpu.VMEM((H, 1, dk), F32), pltpu.VMEM((H, 1, LANE), F32)],
        compiler_params=_params(("arbitrary",)),
    )(qk, proj, gl)


def _mlstm_bwd(qk, proj, gl, dmix, call, nall, mall, *, name):
    tp = qk.shape[0]
    L, H, dk, dv = MLSTM_CHUNK, MLSTM_HEADS, MLSTM_DQK, MLSTM_DV
    nc = tp // L
    G = _mlstm_group(nc)

    def body(qk_ref, v_ref, gl_ref, dh_ref, call_ref, nall_ref, mall_ref, dqk_ref, dv_ref, dgl_ref, dc_s, dn_s):
        @pl.when(pl.program_id(0) == 0)
        def _():
            dc_s[...] = jnp.zeros_like(dc_s)
            dn_s[...] = jnp.zeros_like(dn_s)

        masks = _chunk_masks()
        eye, low, up = masks
        lane = lax.broadcasted_iota(jnp.int32, (L, LANE), 1)
        heads = range(H)
        dcn, dnn = [dc_s[h] for h in heads], [dn_s[h] for h in heads]
        for ci in reversed(range(G)):
            rows = slice(ci * L, (ci + 1) * L)
            c_st = [call_ref[ci, h] for h in heads]
            n_st = [nall_ref[ci, h] for h in heads]
            m_st = [mall_ref[ci, h][:, 0:1] for h in heads]
            q, k, v, li, lf = _mlstm_heads(qk_ref, v_ref, gl_ref, rows)
            dh = [dh_ref[rows, h * dv:(h + 1) * dv] for h in heads]
            f = _mlstm_chunk(q, k, v, li, lf, c_st, n_st, m_st, masks)
            wint, s, wa_c, wc, den, floor = f["wint"], f["s"], f["wa_c"], f["wc"], f["den"], f["floor"]

            r = _each(lambda a, b: 1.0 / jnp.maximum(jnp.abs(a), b), den, floor)
            dnum = _each(lambda a, b: a * b, dh, r)
            dscale = _each(lambda a, b, c: -_rsum(a * b) * c * c, dh, f["num"], r)
            dden = _each(lambda a, b, c: jnp.where(jnp.abs(a) > b, c * jnp.sign(a), 0.0), den, floor, dscale)
            dnv = _each(_nt, dnum, v)
            wd = _each(lambda a, b: a * b, wint, dnum)
            dq_c = _each(_nt, wd, c_st)
            dc_acc = _each(_tn, q, wd)
            dv_s = _each(_tn, s, dnum)
            kd = _each(_nn, k, dcn)
            vd = _each(_nt, v, dcn)
            ds = _each(lambda a, b: a + b, dnv, dden)
            dwint = _each(lambda a, b, c, d: _rsum(a * b) + c * d, dnum, f["qc"], dden, f["qn"])
            dd = _each(lambda a, b: a * b, ds, s)
            da_mat = _each(lambda a, b: a * b, ds, f["wi"])
            dq_a = _each(_nn, da_mat, k)
            dk_a = _each(_tn, da_mat, q)
            dw_n = _each(lambda a, b: a * b, dden, wint)
            dq = _each(lambda a, b, c, d: a + b * c + d, dq_c, dw_n, n_st, dq_a)
            dn_acc = _each(lambda a, b: _csum(a * b), q, dw_n)
            dk_ = _each(lambda a, w, b, c: a + w * (b + c), dk_a, wa_c, vd, dnn)
            dv_ = _each(lambda a, w, b: a + w * b, dv_s, wa_c, kd)
            dwa = _each(lambda a, b, c, d: _rsum(a * b) + _rsum(c * d), kd, v, k, dnn)
            dwc = _each(lambda a, b, c, d: _csum(_rsum(a * b)) + _rsum(c * d), dcn, c_st, dnn, n_st)
            da_c = _each(lambda a, b: a * b, dwa, wa_c)
            dg = _each(lambda a, b, c: _csum(a) + b * c, da_c, dwc, wc)
            dd_cols = _each(_csum, dd)
            db_c = _each(lambda a, b, c, d: a * b + _rsum(c) - d, dwint, wint, dd, da_c)
            db_r = _each(lambda a, b: _to_row(a, eye) - b, db_c, dd_cols)
            dlf = _each(lambda a, b: _rsum(jnp.where(up, a, 0.0)) + b, db_r, dg)
            dli = _each(lambda a, b: a + _to_col(b, eye), da_c, dd_cols)
            dcn = _each(lambda w, a, b: w * a + b, wc, dcn, dc_acc)
            dnn = _each(lambda w, a, b: w * a + b, wc, dnn, dn_acc)

            dgl = jnp.zeros((L, LANE), F32)
            for h in heads:
                dqk_ref[rows, h * dk:(h + 1) * dk] = dq[h] * (dk ** -0.5)
                dqk_ref[rows, MQK + h * dk:MQK + (h + 1) * dk] = dk_[h]
                dv_ref[rows, h * dv:(h + 1) * dv] = dv_[h]
                dgl = dgl + jnp.where(lane == h, dli[h], 0.0) + jnp.where(lane == H + h, dlf[h], 0.0)
            dgl_ref[rows, :] = dgl
        for h in heads:
            dc_s[h], dn_s[h] = dcn[h], dnn[h]

    rev = lambda i: nc // G - 1 - i
    return pl.pallas_call(
        body, name=name, grid=(nc // G,),
        in_specs=[pl.BlockSpec((G * L, 2 * MQK), lambda i: (rev(i), 0)),
                  pl.BlockSpec((G * L, MV), lambda i: (rev(i), 1)),
                  pl.BlockSpec((G * L, LANE), lambda i: (rev(i), 0)),
                  pl.BlockSpec((G * L, MV), lambda i: (rev(i), 0)),
                  pl.BlockSpec((G, H, dk, dv), lambda i: (rev(i), 0, 0, 0)),
                  pl.BlockSpec((G, H, 1, dk), lambda i: (rev(i), 0, 0, 0)),
                  pl.BlockSpec((G, H, 1, LANE), lambda i: (rev(i), 0, 0, 0))],
        out_specs=[pl.BlockSpec((G * L, 2 * MQK), lambda i: (rev(i), 0)),
                   pl.BlockSpec((G * L, MV), lambda i: (rev(i), 0)),
                   pl.BlockSpec((G * L, LANE), lambda i: (rev(i), 0))],
        out_shape=[jax.ShapeDtypeStruct((tp, 2 * MQK), F32), jax.ShapeDtypeStruct((tp, MV), F32),
                   jax.ShapeDtypeStruct((tp, LANE), F32)],
        scratch_shapes=[pltpu.VMEM((H, dk, dv), F32), pltpu.VMEM((H, 1, dk), F32)],
        compiler_params=_params(("arbitrary",)),
    )(qk, proj, gl, dmix, call, nall, mall)


SB_EXP_CAP = 80.0
SB_Q0 = 3 * MV // LANE
SB_K0 = SB_Q0 + SB_HEADS
SB_V0 = SB_K0 + SB_HEADS


def _cumsum_dot(x, tri):
    hi = x.astype(jnp.bfloat16)
    lo = (x - hi.astype(F32)).astype(jnp.bfloat16)
    dims = (((1,), (0,)), ((), ()))
    return (lax.dot_general(hi, tri, dims, preferred_element_type=F32)
            + lax.dot_general(lo, tri, dims, preferred_element_type=F32))


def _sb_query_blocks(nq):
    return _divisor(nq, (5, 4, 3, 2, 1))


def _sb_mask(tile, g, tq):
    t_idx = tile * tq + lax.broadcasted_iota(jnp.int32, (tq, tq), 0)
    s_idx = g * tq + lax.broadcasted_iota(jnp.int32, (tq, tq), 1)
    return (s_idx < t_idx) & (s_idx >= PAD_FRONT)


def _sb_blocks(x):
    return [x[:, k * SB_BLOCK:(k + 1) * SB_BLOCK] for k in range(x.shape[1] // SB_BLOCK)]


def _sb_logits(qb, kg, tri, mask):
    z = _nt(qb, kg) * (SB_DH ** -0.5)
    zc = jnp.minimum(z, SB_EXP_CAP)
    l = (zc - z) - jnp.log(1.0 + jnp.exp(zc))
    if mask is not None:
        l = jnp.where(mask, l, 0.0)
    return z, l, [_cumsum_dot(b, tri) for b in _sb_blocks(l)]


def _sb_weights(z, withins, runs, mask):
    e = jnp.exp(z + jnp.concatenate([w + r for w, r in zip(withins, runs)], axis=1))
    return e if mask is None else jnp.where(mask, e, 0.0)


def _sb_segments(tile, group, descending):
    def diagonal():
        group(tile, True)

    def interior():
        def it(gg, c):
            group(tile - 1 - gg if descending else 1 + gg, False)
            return c
        lax.fori_loop(0, jnp.maximum(tile - 1, 0), it, 0)

    def first():
        @pl.when(tile > 0)
        def _():
            group(0, True)

    for part in ((diagonal, interior, first) if descending else (first, interior, diagonal)):
        part()


def _sb_fwd(proj, *, name):
    tp = proj.shape[0]
    B, H = SB_BLOCK, SB_HEADS
    nq = tp // B
    assert nq <= LANE and B == LANE
    r = _sb_query_blocks(nq)
    tq = r * B

    def body(q_ref, k_ref, v_ref, o_ref, ac_ref, run_s):
        tile = pl.program_id(1)
        qb = q_ref[...].astype(MXU_DTYPE)
        lane = lax.broadcasted_iota(jnp.int32, (tq, LANE), 1)
        tri = (lax.broadcasted_iota(jnp.int32, (B, B), 0) >= lax.broadcasted_iota(jnp.int32, (B, B), 1)
               ).astype(jnp.bfloat16)
        o_ref[...] = jnp.zeros_like(o_ref)
        ac_ref[0, 0] = jnp.zeros((tq, LANE), F32)
        run_s[...] = jnp.zeros_like(run_s)

        def group(g, masked):
            grows = pl.ds(pl.multiple_of(g * tq, tq), tq)
            mask = _sb_mask(tile, g, tq) if masked else None
            z, l, withins = _sb_logits(qb, k_ref[grows, :], tri, mask)
            run, saved, runs = run_s[...], ac_ref[0, 0], [None] * r
            for k in reversed(range(r)):
                runs[k] = run
                saved = jnp.where(lane == g * r + k, run, saved)
                run = run + withins[k][:, 0:1]
            o_ref[...] += _nn(_sb_weights(z, withins, runs, mask), v_ref[grows, :])
            ac_ref[0, 0] = saved
            run_s[...] = run

        _sb_segments(tile, group, descending=True)

    return pl.pallas_call(
        body, name=name, grid=(H, nq // r), scratch_shapes=[pltpu.VMEM((tq, LANE), F32)],
        in_specs=[pl.BlockSpec((tq, SB_DH), lambda h, i: (i, SB_Q0 + h)),
                  pl.BlockSpec((tp, SB_DH), lambda h, i: (0, SB_K0 + h)),
                  pl.BlockSpec((tp, SB_DH), lambda h, i: (0, SB_V0 + h))],
        out_specs=[pl.BlockSpec((tq, SB_DH), lambda h, i: (i, h)),
                   pl.BlockSpec((1, 1, tq, LANE), lambda h, i: (h, i, 0, 0))],
        out_shape=[jax.ShapeDtypeStruct((tp, SBW), F32), jax.ShapeDtypeStruct((H, nq // r, tq, LANE), F32)],
        compiler_params=_params(("parallel", "arbitrary")),
    )(proj, proj, proj)


def _sb_bwd(proj, across, dmix, *, name):
    tp = proj.shape[0]
    B, H = SB_BLOCK, SB_HEADS
    nq = tp // B
    r = _sb_query_blocks(nq)
    tq = r * B
    do0 = MV // LANE

    def body(q_ref, k_ref, v_ref, ac_ref, do_ref, dq_ref, dk_ref, dv_ref, gpre_s):
        tile = pl.program_id(1)

        @pl.when(tile == 0)
        def _():
            dk_ref[...] = jnp.zeros_like(dk_ref)
            dv_ref[...] = jnp.zeros_like(dv_ref)

        dq_ref[...] = jnp.zeros_like(dq_ref)
        gpre_s[...] = jnp.zeros_like(gpre_s)
        qb, dob = q_ref[...].astype(MXU_DTYPE), do_ref[...].astype(MXU_DTYPE)
        lane = lax.broadcasted_iota(jnp.int32, (tq, LANE), 1)
        rr = lax.broadcasted_iota(jnp.int32, (B, B), 0)
        cc = lax.broadcasted_iota(jnp.int32, (B, B), 1)
        tri = (rr >= cc).astype(jnp.bfloat16)
        prefix = (rr <= cc).astype(jnp.bfloat16)
        scale = SB_DH ** -0.5

        def group(g, masked):
            grows = pl.ds(pl.multiple_of(g * tq, tq), tq)
            kg, vg = k_ref[grows, :], v_ref[grows, :]
            mask = _sb_mask(tile, g, tq) if masked else None
            z, l, withins = _sb_logits(qb, kg, tri, mask)
            saved = ac_ref[0, 0]
            runs = [jnp.sum(jnp.where(lane == g * r + k, saved, 0.0), axis=1, keepdims=True) for k in range(r)]
            w = _sb_weights(z, withins, runs, mask)
            dv_ref[grows, :] += _tn(w, dob)
            gw = _nt(dob, vg) * w
            gpre, gcum = gpre_s[...], []
            for gc in [_cumsum_dot(b, prefix) for b in _sb_blocks(gw)]:
                gcum.append(gc + gpre)
                gpre = gpre + gc[:, B - 1:B]
            beta_g = jnp.exp(z + l) * jnp.concatenate(gcum, axis=1)
            if masked:
                beta_g = jnp.where(mask, beta_g, 0.0)
            dz = ((gw - beta_g) * scale).astype(MXU_DTYPE)
            dk_ref[grows, :] += _tn(dz, qb)
            dq_ref[...] += _nn(dz, kg)
            gpre_s[...] = gpre

        _sb_segments(tile, group, descending=False)

    return pl.pallas_call(
        body, name=name, grid=(H, nq // r), scratch_shapes=[pltpu.VMEM((tq, LANE), F32)],
        in_specs=[pl.BlockSpec((tq, SB_DH), lambda h, i: (i, SB_Q0 + h)),
                  pl.BlockSpec((tp, SB_DH), lambda h, i: (0, SB_K0 + h)),
                  pl.BlockSpec((tp, SB_DH), lambda h, i: (0, SB_V0 + h)),
                  pl.BlockSpec((1, 1, tq, LANE), lambda h, i: (h, i, 0, 0)),
                  pl.BlockSpec((tq, SB_DH), lambda h, i: (i, do0 + h))],
        out_specs=[pl.BlockSpec((tq, SB_DH), lambda h, i: (i, h)),
                   pl.BlockSpec((tp, SB_DH), lambda h, i: (0, h)),
                   pl.BlockSpec((tp, SB_DH), lambda h, i: (0, h))],
        out_shape=[jax.ShapeDtypeStruct((tp, SBW), F32)] * 3,
        compiler_params=_params(("parallel", "arbitrary")),
    )(proj, proj, proj, across, dmix)


def _ffn_forward(h, p, tag):
    u = _rms_fwd(h, p["g2"], out_dtype=MXU_DTYPE, name=f"{tag}_ffn_norm")
    a, b, s = _ffn_in(u, p["w_gate"], p["w_up"], name=f"{tag}_ffn_gate_up")
    f = _matmul(s, p["w_down"], name=f"{tag}_ffn_down")
    out = _rms_fwd(f, p["g3"], res=h, name=f"{tag}_ffn_out")
    return out, dict(h=h, u=u, a=a, b=b, s=s, f=f)


def _ffn_backward(dh, p, a, tag):
    df, dg3 = _rms_bwd(a["f"], p["g3"], dh, out_dtype=MXU_DTYPE, name=f"{tag}_ffn_out_bwd")
    da, db = _ffn_down_dx(df, p["w_down"], a["a"], a["b"], name=f"{tag}_ffn_down_dx")
    dw_down = _matmul(a["s"], df, ta=True, name=f"{tag}_ffn_down_dw")
    du_gate = _matmul(da, p["w_gate"], tb=True, name=f"{tag}_ffn_gate_dx")
    du_up = _matmul(db, p["w_up"], tb=True, name=f"{tag}_ffn_up_dx")
    dw_gate = _matmul(a["u"], da, ta=True, name=f"{tag}_ffn_gate_dw")
    dw_up = _matmul(a["u"], db, ta=True, name=f"{tag}_ffn_up_dw")
    dh_in, dg2 = _rms_bwd(a["h"], p["g2"], du_gate, dy2=du_up, add=dh, name=f"{tag}_ffn_norm_bwd")
    return dh_in, dict(g2=dg2, g3=dg3, w_gate=dw_gate, w_up=dw_up, w_down=dw_down)


def _mixer_forward(h, p, tag):
    u = _rms_fwd(h, p["g0"], out_dtype=MXU_DTYPE, name=f"{tag}_mix_norm")
    proj = _matmul(u, p["w_in"], name=f"{tag}_mix_in")
    qc = _conv_fwd(proj, p["qk_w"], p["qk_b"], name=f"{tag}_mix_qkconv")
    qk = _silu_fwd(qc, name=f"{tag}_mix_qkact")
    gl = _gates_fwd(proj, p["gate_b"], name=f"{tag}_mix_gates")
    hm, call, nall, mall = _mlstm_fwd(qk, proj, gl, name=f"{tag}_mlstm")
    hn = _hnorm_fwd(hm, proj, p["hnorm_g"], name=f"{tag}_mix_hnorm")
    hs, across = _sb_fwd(proj, name=f"{tag}_sb")
    mixed = jnp.concatenate([hn, hs], axis=1).astype(MXU_DTYPE)
    y = _matmul(mixed, p["w_out"], name=f"{tag}_mix_out")
    out = _rms_fwd(y, p["g1"], res=h, name=f"{tag}_mix_res")
    return out, dict(h=h, u=u, proj=proj, qc=qc, qk=qk, gl=gl, hm=hm, call=call, nall=nall, mall=mall,
                     across=across, mixed=mixed, y=y)


def _mixer_backward(dh, p, a, tag):
    tp = dh.shape[0]
    dy, dg1 = _rms_bwd(a["y"], p["g1"], dh, out_dtype=MXU_DTYPE, name=f"{tag}_mix_res_bwd")
    dmixed = _matmul(dy, p["w_out"], tb=True, name=f"{tag}_mix_out_dx")
    dw_out = _matmul(a["mixed"], dy, ta=True, name=f"{tag}_mix_out_dw")
    dsq, dsk, dsv = _sb_bwd(a["proj"], a["across"], dmixed, name=f"{tag}_sb_bwd")
    dhm, do, dhg = _hnorm_bwd(a["hm"], a["proj"], p["hnorm_g"], dmixed, name=f"{tag}_mix_hnorm_bwd")
    dqk, dv, dgl = _mlstm_bwd(a["qk"], a["proj"], a["gl"], dhm, a["call"], a["nall"], a["mall"],
                              name=f"{tag}_mlstm_bwd")
    dpg, dgate_b = _gates_bwd(a["proj"], p["gate_b"], dgl, name=f"{tag}_mix_gates_bwd")
    dqc = _silu_bwd(a["qc"], dqk, name=f"{tag}_mix_qkact_bwd")
    dpqk, dqk_w, dqk_b = _conv_bwd(a["proj"], p["qk_w"], dqc, name=f"{tag}_mix_qkconv_bwd")
    dproj = jnp.concatenate(
        [dpqk, dv, do, dsq, dsk, dsv, dpg, jnp.zeros((tp, PROJ_WIDTH - GATE_COL - LANE), F32)], axis=1
    ).astype(MXU_DTYPE)
    du = _matmul(dproj, p["w_in"], tb=True, name=f"{tag}_mix_in_dx")
    dw_in = _matmul(a["u"], dproj, ta=True, name=f"{tag}_mix_in_dw")
    dh_in, dg0 = _rms_bwd(a["h"], p["g0"], du, add=dh, name=f"{tag}_mix_norm_bwd")
    return dh_in, dict(g0=dg0, g1=dg1, w_in=dw_in, qk_w=dqk_w, qk_b=dqk_b, gate_b=dgate_b, hnorm_g=dhg,
                       w_out=dw_out)


def _conformer_forward(h, p, tag):
    u = _rms_fwd(h, p["g0"], out_dtype=MXU_DTYPE, name=f"{tag}_conf_norm")
    z = _matmul(u, p["w_pw1"], name=f"{tag}_conf_pw1")
    y1 = _glu_fwd(z, p["b_pw1"], name=f"{tag}_conf_glu")
    y2 = _conv_fwd(y1, p["w_dw"], p["b_dw"], name=f"{tag}_conf_dw")
    y3 = _lnsilu_fwd(y2, p["ln_g"], p["ln_b"], name=f"{tag}_conf_ln")
    y4 = _matmul(y3, p["w_pw2"], name=f"{tag}_conf_pw2")
    out = _rms_fwd(y4, p["g1"], res=h, bias=p["b_pw2"], name=f"{tag}_conf_res")
    return out, dict(h=h, u=u, z=z, y1=y1, y2=y2, y3=y3, y4=y4)


def _conformer_backward(dh, p, a, tag):
    dy4, dg1, db_pw2 = _rms_bwd(a["y4"], p["g1"], dh, bias=p["b_pw2"], out_dtype=MXU_DTYPE,
                                name=f"{tag}_conf_res_bwd")
    dy3 = _matmul(dy4, p["w_pw2"], tb=True, name=f"{tag}_conf_pw2_dx")
    dw_pw2 = _matmul(a["y3"], dy4, ta=True, name=f"{tag}_conf_pw2_dw")
    dy2, dln_g, dln_b = _lnsilu_bwd(a["y2"], p["ln_g"], p["ln_b"], dy3, name=f"{tag}_conf_ln_bwd")
    dy1, dw_dw, db_dw = _conv_bwd(a["y1"], p["w_dw"], dy2, name=f"{tag}_conf_dw_bwd")
    dz, db_pw1 = _glu_bwd(a["z"], p["b_pw1"], dy1, name=f"{tag}_conf_glu_bwd")
    du = _matmul(dz, p["w_pw1"], tb=True, name=f"{tag}_conf_pw1_dx")
    dw_pw1 = _matmul(a["u"], dz, ta=True, name=f"{tag}_conf_pw1_dw")
    dh_in, dg0 = _rms_bwd(a["h"], p["g0"], du, add=dh, name=f"{tag}_conf_norm_bwd")
    return dh_in, dict(g0=dg0, g1=dg1, w_pw1=dw_pw1, b_pw1=db_pw1, w_dw=dw_dw, b_dw=db_dw, ln_g=dln_g,
                       ln_b=dln_b, w_pw2=dw_pw2, b_pw2=db_pw2)


def _trunk_step(h0, target, w, later=None, early=None):
    acts, layers = [], []
    h = h0
    for li in range(DEPTH):
        if li == 1 and later is not None:
            w = later(w, h)
        tag, p = f"l{li}", _layer_operands(w, li)
        h, a_mix = (_mixer_forward if li % 2 == 0 else _conformer_forward)(h, p["mix"], tag)
        h, a_ffn = _ffn_forward(h, p["ffn"], tag)
        acts.append((a_mix, a_ffn))
        layers.append(p)
    dh, loss_cols = _loss_fwd_bwd(h, target, name="loss")
    grads = [None] * len(layers)
    for li in reversed(range(len(layers))):
        if li == 0 and early is not None:
            dh = dh + early(grads)[0, 0]
        tag = f"l{li}"
        p = layers[li]
        dh, g_ffn = _ffn_backward(dh, p["ffn"], acts[li][1], tag)
        dh, g_mix = (_mixer_backward if li % 2 == 0 else _conformer_backward)(dh, p["mix"], acts[li][0], tag)
        grads[li] = dict(mix=g_mix, ffn=g_ffn)
    return loss_cols, dh, grads


_SPLIT = 2 * MQK + 2 * MV


def _layer_operands(w, li):
    row = lambda v: v[None, :].astype(F32)
    i = li // 2
    g = w["norm_g"][li].astype(F32)
    if li % 2 == 0:
        win = w["mix_w_in"][i]
        w_in = jnp.concatenate(
            [win[:, :_SPLIT], win[:, _SPLIT + 2 * MLSTM_HEADS:], win[:, _SPLIT:_SPLIT + 2 * MLSTM_HEADS],
             jnp.zeros((D_MODEL, PROJ_WIDTH - IN_WIDTH), win.dtype)], axis=1)
        gate_b = jnp.pad(row(w["mix_gate_b"][i]), ((0, 0), (0, LANE - 2 * MLSTM_HEADS)))
        mix = dict(g0=g[0:1], g1=g[1:2], w_in=w_in, qk_w=w["mix_qk_conv_w"][i].astype(F32),
                   qk_b=row(w["mix_qk_conv_b"][i]), gate_b=gate_b, hnorm_g=row(w["mix_hnorm_g"][i]),
                   w_out=w["mix_w_out"][i])
    else:
        mix = dict(g0=g[0:1], g1=g[1:2], w_pw1=w["conv_w_pw1"][i], b_pw1=row(w["conv_b_pw1"][i]),
                   w_dw=w["conv_w_dw"][i].astype(F32), b_dw=row(w["conv_b_dw"][i]),
                   ln_g=row(w["conv_ln_g"][i]), ln_b=row(w["conv_ln_b"][i]), w_pw2=w["conv_w_pw2"][i],
                   b_pw2=row(w["conv_b_pw2"][i]))
    ffn = dict(g2=g[2:3], g3=g[3:4], w_gate=w["ffn_w_gate"][li], w_up=w["ffn_w_up"][li],
               w_down=w["ffn_w_down"][li])
    return dict(mix=mix, ffn=ffn)


def _collect_grads(grads, layers=tuple(range(DEPTH)), vectors=True):
    even = [grads[li]["mix"] for li in layers if li % 2 == 0]
    odd = [grads[li]["mix"] for li in layers if li % 2 == 1]
    ffn = [grads[li]["ffn"] for li in layers]
    st = lambda xs: jnp.stack(xs, axis=0)
    vec = lambda xs, k: st([x[k][0] for x in xs])
    out = {}
    if even:
        out["mix_w_in"] = st([jnp.concatenate(
            [g["w_in"][:, :_SPLIT], g["w_in"][:, GATE_COL:GATE_COL + 2 * MLSTM_HEADS], g["w_in"][:, _SPLIT:GATE_COL]],
            axis=1) for g in even])
        out["mix_w_out"] = st([g["w_out"] for g in even])
    if odd:
        out["conv_w_pw1"] = st([g["w_pw1"] for g in odd])
        out["conv_w_pw2"] = st([g["w_pw2"] for g in odd])
    out["ffn_w_gate"] = st([g["w_gate"] for g in ffn])
    out["ffn_w_up"] = st([g["w_up"] for g in ffn])
    out["ffn_w_down"] = st([g["w_down"] for g in ffn])
    if vectors:
        out["norm_g"] = st([jnp.concatenate([grads[li]["mix"]["g0"], grads[li]["mix"]["g1"], grads[li]["ffn"]["g2"],
                                             grads[li]["ffn"]["g3"]], axis=0) for li in layers])
        out["mix_qk_conv_w"] = st([g["qk_w"] for g in even])
        out["mix_qk_conv_b"] = vec(even, "qk_b")
        out["mix_gate_b"] = st([g["gate_b"][0, :2 * MLSTM_HEADS] for g in even])
        out["mix_hnorm_g"] = vec(even, "hnorm_g")
        out["conv_b_pw1"] = vec(odd, "b_pw1")
        out["conv_w_dw"] = st([g["w_dw"] for g in odd])
        out["conv_b_dw"] = vec(odd, "b_dw")
        out["conv_ln_g"] = vec(odd, "ln_g")
        out["conv_ln_b"] = vec(odd, "ln_b")
        out["conv_b_pw2"] = vec(odd, "b_pw2")
    return out


def _local_step(x, target, w, later=None, early=None):
    seq = x.shape[0]
    h0 = jnp.concatenate([jnp.zeros((PAD_FRONT, D_MODEL), F32), w["meta"].astype(F32), x], axis=0)
    tgt = jnp.concatenate([jnp.zeros((PAD_FRONT + N_META, D_MODEL), F32), target], axis=0)
    loss_cols, dh0, grads = _trunk_step(h0, tgt, w, later, early)
    out = _collect_grads(grads)
    out["meta"] = dh0[PAD_FRONT:PAD_FRONT + N_META]
    loss = 0.5 * jnp.sum(loss_cols) / D_MODEL
    return loss, dh0[PAD_FRONT + N_META:PAD_FRONT + N_META + seq], out


def _elementwise(fn, arrays, out_dtypes, *, name):
    shape = arrays[0].shape
    cols = shape[-1]
    rows = 1
    for s in shape[:-1]:
        rows *= s
    flat = [a.reshape(rows, cols) for a in arrays]
    if rows * cols * 4 <= (1 << 20) or rows % SUBLANE:
        tr = rows
    else:
        tr = _divisor(rows, (512, 256, 128, 64, 32, 16, 8))
    n = len(flat)

    def body(*refs):
        outs = fn(*[r[...] for r in refs[:n]])
        for o_ref, o in zip(refs[n:], outs):
            o_ref[...] = o.astype(o_ref.dtype)

    spec = pl.BlockSpec((tr, cols), lambda i: (i, 0))
    outs = pl.pallas_call(
        body, name=name, grid=(rows // tr,), in_specs=[spec] * n, out_specs=[spec] * len(out_dtypes),
        out_shape=[jax.ShapeDtypeStruct((rows, cols), dt) for dt in out_dtypes],
        compiler_params=_params(("parallel",)),
    )(*flat)
    return [o.reshape(shape) for o in outs]


def _adamw(w, g, m, v, *, name):
    def fn(wv, gv, mv, vv):
        mn = ADAM_B1 * mv + (1.0 - ADAM_B1) * gv
        vn = ADAM_B2 * vv + (1.0 - ADAM_B2) * (gv * gv)
        m_hat = mn / (1.0 - ADAM_B1 ** ADAM_STEP)
        v_hat = vn / (1.0 - ADAM_B2 ** ADAM_STEP)
        return [-ADAM_LR * (m_hat / (jnp.sqrt(v_hat) + ADAM_EPS) + ADAM_WD * wv), mn, vn]

    return _elementwise(fn, [w, g, m, v], [F32, F32, F32], name=name)


MESH_ID = pl.DeviceIdType.MESH
ANY = pl.BlockSpec(memory_space=pl.ANY)


def _place():
    x, y, c = lax.axis_index("x"), lax.axis_index("y"), lax.axis_index("c")
    return x, y, c, [(1 - x, y), (x, 1 - y), (1 - x, 1 - y)]


def _remote(src, dst, send_sems, recv_sems, k, to):
    return pltpu.make_async_remote_copy(src_ref=src, dst_ref=dst, send_sem=send_sems.at[k], recv_sem=recv_sems.at[k],
                                        device_id=to, device_id_type=MESH_ID)


def _comm_call(body, arrays, out_shapes, n_remote, n_local, name):
    return pl.pallas_call(
        body, name=name, in_specs=[ANY] * len(arrays), out_specs=[ANY] * len(out_shapes), out_shape=out_shapes,
        scratch_shapes=[pltpu.SemaphoreType.DMA((n_remote,)), pltpu.SemaphoreType.DMA((n_remote,)),
                        pltpu.SemaphoreType.DMA((n_local,))],
        compiler_params=pltpu.CompilerParams(has_side_effects=True),
    )(*arrays)


def _gather_chips(shards, *, name):
    n = len(shards)

    def body(*refs):
        ins, outs = refs[:n], refs[n:2 * n]
        send_sems, recv_sems, local_sems = refs[2 * n:]
        x, y, c, chips = _place()
        me, sibling = 2 * x + y, (x, y, 1 - c)

        def half(a, slot, hc):
            hl = ins[a].shape[0] // 2
            return outs[a].at[slot].at[pl.ds(hc * hl, hl)]

        def mine(a):
            hl = ins[a].shape[0] // 2
            return ins[a].at[pl.ds(c * hl, hl)]

        sent = []
        for a in range(n):
            for j, (px, py) in enumerate(chips):
                sent.append(_remote(mine(a), half(a, me, c), send_sems, recv_sems, 6 * a + j, (px, py, c)))
                sent[-1].start()
        for a in range(n):
            for j, (px, py) in enumerate(chips):
                slot = 2 * px + py
                _remote(mine(a), half(a, slot, c), send_sems, recv_sems, 6 * a + j, (px, py, c)).wait_recv()
                sent.append(_remote(half(a, slot, c), half(a, slot, c), send_sems, recv_sems, 6 * a + 3 + j, sibling))
                sent[-1].start()
        for a in range(n):
            for j, (px, py) in enumerate(chips):
                slot = 2 * px + py
                _remote(mine(a), half(a, slot, 1 - c), send_sems, recv_sems, 6 * a + 3 + j, sibling).wait_recv()
        for cp in sent:
            cp.wait_send()

    out_shapes = [jax.ShapeDtypeStruct((4,) + s.shape, s.dtype) for s in shards]
    return _comm_call(body, shards, out_shapes, 6 * n, 1, name)


def _swap_siblings(arrays, *, by_core, name):
    n = len(arrays)

    def body(*refs):
        ins, outs = refs[:n], refs[n:2 * n]
        send_sems, recv_sems, _ = refs[2 * n:]
        x, y, c, _chips = _place()
        cps = [_remote(ins[a].at[1 - c] if by_core else ins[a], outs[a], send_sems, recv_sems, a, (x, y, 1 - c))
               for a in range(n)]
        for cp in cps:
            cp.start()
        for cp in cps:
            cp.wait()

    out_shapes = [jax.ShapeDtypeStruct(a.shape[1:] if by_core else a.shape, a.dtype) for a in arrays]
    return _comm_call(body, arrays, out_shapes, n, 1, name)


HBM = pl.BlockSpec(memory_space=pltpu.HBM)
SEM = pl.BlockSpec(memory_space=pltpu.SEMAPHORE)


def _gather_start(shards, after, *, name):
    n = len(shards)
    lands = [lax.empty((4,) + s.shape[1:], s.dtype) for s in shards]

    def body(*refs):
        ins, land = refs[:n], refs[n:2 * n]
        send_sems, recv_sems, token = refs[2 * n + 1], refs[2 * n + 2], refs[-1]
        x, y, c, chips = _place()
        for a in range(n):
            for j, (px, py) in enumerate(chips):
                _remote(ins[a].at[c], land[a].at[2 * x + y], send_sems, recv_sems, 3 * a + j, (px, py, c)).start()
        token[...] = jnp.zeros_like(token)

    hbm = lambda a: pltpu.with_memory_space_constraint(a, pltpu.HBM)
    out = pl.pallas_call(
        body, name=name,
        out_shape=(pltpu.SemaphoreType.DMA((3 * n,)), pltpu.SemaphoreType.DMA((3 * n,)),
                   *[pltpu.HBM(a.shape, a.dtype) for a in shards + lands], jax.ShapeDtypeStruct((SUBLANE, LANE), F32)),
        in_specs=[HBM] * (2 * n) + [ANY],
        out_specs=(SEM, SEM, *[HBM] * (2 * n), pl.BlockSpec(memory_space=pltpu.VMEM)),
        input_output_aliases={a: 2 + a for a in range(2 * n)},
        compiler_params=pltpu.CompilerParams(has_side_effects=pltpu.SideEffectType.DATAFLOW_SIDE_EFFECTING),
    )(*[hbm(a) for a in shards + lands], after)
    return out[0], out[1], list(out[2:2 + n]), list(out[2 + n:2 + 2 * n]), out[-1]


def _gather_wait(send_sems, recv_sems, shards, lands, after, *, name):
    n = len(shards)

    def body(*refs):
        ins, land, ssem, rsem = refs[:n], refs[n:2 * n], refs[2 * n], refs[2 * n + 1]
        x, y, c, chips = _place()
        for a in range(n):
            for j, (px, py) in enumerate(chips):
                cp = _remote(ins[a].at[c], land[a].at[2 * px + py], ssem, rsem, 3 * a + j, (px, py, c))
                cp.wait_send()
                cp.wait_recv()

    out = pl.pallas_call(
        body, name=name, out_shape=[pltpu.HBM(a.shape, a.dtype) for a in shards + lands],
        in_specs=[HBM] * (2 * n) + [SEM, SEM, ANY], out_specs=[HBM] * (2 * n),
        input_output_aliases={a: a for a in range(2 * n)},
        compiler_params=pltpu.CompilerParams(has_side_effects=pltpu.SideEffectType.DATAFLOW_SIDE_EFFECTING),
    )(*shards, *lands, send_sems, recv_sems, after)
    return list(out[n:])


def _scatter_chips(parts, small, *, name):
    n = len(parts)

    def body(*refs):
        ins, small_in = refs[:n], refs[n]
        outs, small_out = refs[n + 1:2 * n + 1], refs[2 * n + 1]
        send_sems, recv_sems, local_sems = refs[2 * n + 2:]
        x, y, c, chips = _place()
        me8 = 4 * x + 2 * y + c
        own = pltpu.make_async_copy(small_in, small_out.at[me8], local_sems.at[0])
        own.start()
        cps = []
        for fx in range(2):
            for fy in range(2):
                for fc in range(2):
                    r = 4 * fx + 2 * fy + fc - 1
                    if r >= 0:
                        to = (x + fx - 2 * x * fx, y + fy - 2 * y * fy, c + fc - 2 * c * fc)
                        cps.append(_remote(small_in, small_out.at[me8], send_sems, recv_sems, r, to))
        for a in range(n):
            for j, (px, py) in enumerate(chips):
                cps.append(_remote(ins[a].at[2 * px + py], outs[a].at[j], send_sems, recv_sems, 7 + 3 * a + j,
                                   (px, py, c)))
        for cp in cps:
            cp.start()
        for cp in cps:
            cp.wait()
        own.wait()

    out_shapes = [jax.ShapeDtypeStruct((3,) + p.shape[1:], p.dtype) for p in parts]
    out_shapes.append(jax.ShapeDtypeStruct((8,) + small.shape, small.dtype))
    return _comm_call(body, list(parts) + [small], out_shapes, 7 + 3 * n, 1, name)


WEIGHTS = ("meta", "norm_g", "mix_w_in", "mix_qk_conv_w", "mix_qk_conv_b", "mix_gate_b", "mix_hnorm_g", "mix_w_out",
           "conv_w_pw1", "conv_b_pw1", "conv_w_dw", "conv_b_dw", "conv_ln_g", "conv_ln_b", "conv_w_pw2",
           "conv_b_pw2", "ffn_w_gate", "ffn_w_up", "ffn_w_down")
SHARD_AXIS = dict(meta=1, norm_g=2, mix_w_in=2, mix_qk_conv_w=2, mix_qk_conv_b=None, mix_gate_b=None,
                  mix_hnorm_g=None, mix_w_out=1, conv_w_pw1=2, conv_b_pw1=1, conv_w_dw=2, conv_b_dw=1, conv_ln_g=1,
                  conv_ln_b=1, conv_w_pw2=1, conv_b_pw2=1, ffn_w_gate=2, ffn_w_up=2, ffn_w_down=1)
MATRICES = ("mix_w_in", "mix_w_out", "conv_w_pw1", "conv_w_pw2", "ffn_w_gate", "ffn_w_up", "ffn_w_down")
VECTORS = tuple(n for n in WEIGHTS if n not in MATRICES)
GATHER_COLS = D_MODEL // 4


def _pack_rows(arrays, cols, pad_to):
    rows = [a.astype(F32).reshape(-1) for a in arrays]
    rows = [jnp.pad(r, (0, (-r.shape[0]) % cols)).reshape(-1, cols) for r in rows]
    packed = jnp.concatenate(rows, axis=0)
    return jnp.pad(packed, ((0, pad_to - packed.shape[0]), (0, 0))), [r.shape[0] for r in rows]


def _unpack_rows(packed, counts, shapes):
    out, at = [], 0
    for n, shape in zip(counts, shapes):
        size = 1
        for s in shape:
            size *= s
        out.append(packed[..., at:at + n, :].reshape(packed.shape[:-2] + (-1,))[..., :size]
                   .reshape(packed.shape[:-2] + tuple(shape)))
        at += n
    return out


FIRST_LAYER = ("mix_w_in", "mix_w_out", "ffn_w_gate", "ffn_w_up", "ffn_w_down")
PACK_ROWS = 128


def _row_halves(a):
    return a.reshape((2, a.shape[0] * a.shape[1] // 2, a.shape[2]))


def _assemble(slots, own, name):
    me = 2 * lax.axis_index("x") + lax.axis_index("y")
    slots = lax.dynamic_update_index_in_dim(slots, own, me, 0)
    return jnp.concatenate([slots[k] for k in range(4)], axis=SHARD_AXIS[name])


def _gather_weights(local):
    c = lax.axis_index("c")
    sharded_vecs = [n for n in VECTORS if SHARD_AXIS[n] is not None]
    pack, counts = _pack_rows([local[n] for n in sharded_vecs], GATHER_COLS, PACK_ROWS)
    first = [local[n][:1].astype(MXU_DTYPE) for n in FIRST_LAYER]
    got = _gather_chips([_row_halves(s) for s in first] + [_row_halves(pack[None])], name="gather_first")
    w = {n: local[n] for n in VECTORS if SHARD_AXIS[n] is None}
    for n, s, g in zip(FIRST_LAYER, first, got):
        w[n] = [_assemble(g.reshape((4,) + s.shape), s, n)[0]]
    packs = lax.dynamic_update_index_in_dim(got[-1].reshape(4, PACK_ROWS, GATHER_COLS), pack,
                                            2 * lax.axis_index("x") + lax.axis_index("y"), 0)
    vecs = _unpack_rows(packs, counts, [local[n].shape for n in sharded_vecs])
    for n, v in zip(sharded_vecs, vecs):
        w[n] = jnp.moveaxis(v, 0, -2).reshape(v.shape[1:-1] + (4 * v.shape[-1],))

    rest = [(local[n][1:] if n in FIRST_LAYER else local[n]).astype(MXU_DTYPE) for n in MATRICES]
    send_sems, recv_sems, sent, lands, token = _gather_start([_row_halves(r) for r in rest], got[-1],
                                                             name="gather_rest_start")

    def later(w, after):
        mine = _gather_wait(send_sems, recv_sems, sent, lands, after, name="gather_rest_wait")
        theirs = _swap_siblings(mine, by_core=False, name="gather_rest_swap")
        w = dict(w)
        for n, r, m, t in zip(MATRICES, rest, mine, theirs):
            both = jnp.where(c == 0, jnp.concatenate([m, t], axis=1), jnp.concatenate([t, m], axis=1))
            layers = _assemble(both.reshape((4,) + r.shape), r, n)
            w[n] = list(w.get(n, [])) + [layers[k] for k in range(r.shape[0])]
        return w

    return w, later, token


def _pair_sums(grads, tag):
    c = lax.axis_index("c")
    stacked = []
    for n, g in grads.items():
        parts = jnp.stack(jnp.split(g, 4, axis=SHARD_AXIS[n]), axis=0)
        parts = parts.reshape(4, 2, parts.shape[1] * parts.shape[2] // 2, parts.shape[3])
        stacked.append(jnp.swapaxes(parts, 0, 1))
    theirs = _swap_siblings(stacked, by_core=True, name=f"reduce_{tag}_pair_swap")
    return [_elementwise(lambda a, b: [a + b], [lax.dynamic_index_in_dim(s, c, 0, keepdims=False), t], [F32],
                         name=f"reduce_{tag}_pair_sum_{n}")[0] for n, s, t in zip(grads, stacked, theirs)]


def _chip_sums(grads, pair, got, tag):
    c = lax.axis_index("c")
    me = 2 * lax.axis_index("x") + lax.axis_index("y")
    halves = []
    for n, p, r in zip(grads, pair, got):
        own = lax.dynamic_index_in_dim(p, me, 0, keepdims=False)
        halves.append(_elementwise(lambda a, b0, b1, b2: [((a + b0.astype(F32)) + b1.astype(F32)) + b2.astype(F32)],
                                   [own, r[0], r[1], r[2]], [F32], name=f"reduce_{tag}_chip_sum_{n}")[0])
    others = _swap_siblings(halves, by_core=False, name=f"reduce_{tag}_join")
    out = {}
    for (n, g), h, o in zip(grads.items(), halves, others):
        rows = jnp.where(c == 0, jnp.concatenate([h, o], axis=0), jnp.concatenate([o, h], axis=0))
        shard = list(g.shape)
        shard[SHARD_AXIS[n]] //= 4
        out[n] = rows.reshape(shard)
    return out


def _scatter_start(parts, after, *, name):
    n = len(parts)
    lands = [lax.empty((3,) + p.shape[1:], p.dtype) for p in parts]

    def body(*refs):
        ins, land = refs[:n], refs[n:2 * n]
        send_sems, recv_sems, token = refs[2 * n + 1], refs[2 * n + 2], refs[-1]
        x, y, c, chips = _place()
        for a in range(n):
            for j, (px, py) in enumerate(chips):
                _remote(ins[a].at[2 * px + py], land[a].at[j], send_sems, recv_sems, 3 * a + j, (px, py, c)).start()
        token[...] = jnp.zeros_like(token)

    hbm = lambda a: pltpu.with_memory_space_constraint(a, pltpu.HBM)
    out = pl.pallas_call(
        body, name=name,
        out_shape=(pltpu.SemaphoreType.DMA((3 * n,)), pltpu.SemaphoreType.DMA((3 * n,)),
                   *[pltpu.HBM(a.shape, a.dtype) for a in parts + lands], jax.ShapeDtypeStruct((SUBLANE, LANE), F32)),
        in_specs=[HBM] * (2 * n) + [ANY],
        out_specs=(SEM, SEM, *[HBM] * (2 * n), pl.BlockSpec(memory_space=pltpu.VMEM)),
        input_output_aliases={a: 2 + a for a in range(2 * n)},
        compiler_params=pltpu.CompilerParams(has_side_effects=pltpu.SideEffectType.DATAFLOW_SIDE_EFFECTING),
    )(*[hbm(a) for a in parts + lands], after)
    return out[0], out[1], list(out[2:2 + n]), list(out[2 + n:2 + 2 * n]), out[-1]


def _scatter_wait(send_sems, recv_sems, parts, lands, after, *, name):
    n = len(parts)

    def body(*refs):
        ins, land, ssem, rsem = refs[:n], refs[n:2 * n], refs[2 * n], refs[2 * n + 1]
        x, y, c, chips = _place()
        for a in range(n):
            for j, (px, py) in enumerate(chips):
                cp = _remote(ins[a].at[2 * px + py], land[a].at[j], ssem, rsem, 3 * a + j, (px, py, c))
                cp.wait_send()
                cp.wait_recv()

    out = pl.pallas_call(
        body, name=name, out_shape=[pltpu.HBM(a.shape, a.dtype) for a in parts + lands],
        in_specs=[HBM] * (2 * n) + [SEM, SEM, ANY], out_specs=[HBM] * (2 * n),
        input_output_aliases={a: a for a in range(2 * n)},
        compiler_params=pltpu.CompilerParams(has_side_effects=pltpu.SideEffectType.DATAFLOW_SIDE_EFFECTING),
    )(*parts, *lands, send_sems, recv_sems, after)
    return list(out[n:])


def _reduce_rest_start(grads_by_layer):
    grads = _collect_grads(grads_by_layer, layers=tuple(range(1, DEPTH)), vectors=False)
    pair = _pair_sums(grads, "rest")
    send_sems, recv_sems, sent, lands, token = _scatter_start([p.astype(jnp.bfloat16) for p in pair], pair[0],
                                                              name="reduce_rest_start")
    return (grads, pair, send_sems, recv_sems, sent, lands), token


def _reduce_rest_finish(state, after):
    grads, pair, send_sems, recv_sems, sent, lands = state
    got = _scatter_wait(send_sems, recv_sems, sent, lands, after, name="reduce_rest_wait")
    return _chip_sums(grads, pair, got, "rest")


def _reduce_grads(grads, rest):
    me = 2 * lax.axis_index("x") + lax.axis_index("y")
    first = {n: grads[n][:1] for n in FIRST_LAYER}
    pair = _pair_sums(first, "first")
    shapes = [grads[n].shape for n in VECTORS]
    pack, counts = _pack_rows([grads[n] for n in VECTORS], D_MODEL, 120)
    got = _scatter_chips([p.astype(jnp.bfloat16) for p in pair], pack, name="reduce_chips")
    out = _chip_sums(first, pair, got[:-1], "first")
    out = {n: (jnp.concatenate([out[n], rest[n]], axis=0) if n in out else rest[n]) for n in MATRICES}
    small = got[-1]
    total = _elementwise(lambda *s: [functools.reduce(lambda a, b: a + b, s)], [small[k] for k in range(8)], [F32],
                         name="reduce_small_sum")[0]
    for n, v in zip(VECTORS, _unpack_rows(total, counts, shapes)):
        ax = SHARD_AXIS[n]
        if ax is not None:
            w = v.shape[ax] // 4
            v = lax.dynamic_slice_in_dim(v, me * w, w, axis=ax)
        out[n] = v
    return out


def kernel(x, meta, norm_g, mix_w_in, mix_qk_conv_w, mix_qk_conv_b, mix_gate_b, mix_hnorm_g, mix_w_out, conv_w_pw1, conv_b_pw1, conv_w_dw, conv_b_dw, conv_ln_g, conv_ln_b, conv_w_pw2, conv_b_pw2, ffn_w_gate, ffn_w_up, ffn_w_down, loss_target, m_meta, m_norm_g, m_mix_w_in, m_mix_qk_conv_w, m_mix_qk_conv_b, m_mix_gate_b, m_mix_hnorm_g, m_mix_w_out, m_conv_w_pw1, m_conv_b_pw1, m_conv_w_dw, m_conv_b_dw, m_conv_ln_g, m_conv_ln_b, m_conv_w_pw2, m_conv_b_pw2, m_ffn_w_gate, m_ffn_w_up, m_ffn_w_down, v_meta, v_norm_g, v_mix_w_in, v_mix_qk_conv_w, v_mix_qk_conv_b, v_mix_gate_b, v_mix_hnorm_g, v_mix_w_out, v_conv_w_pw1, v_conv_b_pw1, v_conv_w_dw, v_conv_b_dw, v_conv_ln_g, v_conv_ln_b, v_conv_w_pw2, v_conv_b_pw2, v_ffn_w_gate, v_ffn_w_up, v_ffn_w_down):
    given = dict(locals())
    local = {n: given[n] for n in WEIGHTS}
    first, later, token = _gather_weights(local)
    in_flight = []

    def early(grads_by_layer):
        state, started = _reduce_rest_start(grads_by_layer)
        in_flight.append(state)
        return started

    loss, grad_x, grads = _local_step(x[0] + token[0, 0], loss_target[0], first, later, early)
    loss = lax.psum(loss, ("x", "y", "c"))
    grad_w = _reduce_grads(grads, _reduce_rest_finish(in_flight[0], grad_x))
    delta, new_m, new_v = {}, {}, {}
    for n in WEIGHTS:
        delta[n], new_m[n], new_v[n] = _adamw(local[n], grad_w[n], given["m_" + n], given["v_" + n], name=f"adamw_{n}")
    return (loss, grad_x[None], *[grad_w[n] for n in WEIGHTS], *[delta[n] for n in WEIGHTS],
            *[new_m[n] for n in WEIGHTS], *[new_v[n] for n in WEIGHTS])
```

```python
import functools

import jax
import jax.numpy as jnp
from jax import lax
from jax.experimental import pallas as pl
from jax.experimental.pallas import tpu as pltpu

F32 = jnp.float32
MXU_DTYPE = jnp.bfloat16

D_MODEL = 1024
N_META = 16
DEPTH = 4
MLSTM_HEADS = 4
MLSTM_DQK = 128
MLSTM_DV = 256
MLSTM_CHUNK = 64
QK_CONV_WIDTH = 4
GATE_SOFTCAP = 15.0
SB_HEADS = 4
SB_DH = 128
SB_BLOCK = 128
PAD_FRONT = SB_BLOCK - N_META
CONV_WIDTH = 31
FFN_HIDDEN = 2816
MQK = MLSTM_HEADS * MLSTM_DQK
MV = MLSTM_HEADS * MLSTM_DV
SBW = SB_HEADS * SB_DH
IN_WIDTH = 2 * MQK + 2 * MV + 2 * MLSTM_HEADS + 3 * SBW
MIX_WIDTH = MV + SBW
NEG = -1e30
EPS = 1e-6
PROJ_WIDTH = 5120
GATE_COL = 3 * MV + 3 * SBW
LANE = 128
SUBLANE = 8
CONV_HALO = 32
VMEM_LIMIT = 56 * 1024 * 1024

ADAM_LR = 0.001
ADAM_B1 = 0.9
ADAM_B2 = 0.999
ADAM_EPS = 1e-08
ADAM_WD = 0.01
ADAM_STEP = 10


def _divisor(n, cands):
    for c in cands:
        if n % c == 0:
            return c
    raise ValueError(f"no tile for {n} in {cands}")


ROW_TILE_BYTES = 20 * 1024 * 1024


def _row_tile(tp, width=D_MODEL):
    for c in (640, 512, 384, 320, 256, 128, 64):
        if tp % c == 0 and c * width * 8 <= ROW_TILE_BYTES:
            return c
    raise ValueError(f"no row tile for {tp} x {width}")


def _params(sem):
    return pltpu.CompilerParams(dimension_semantics=sem, vmem_limit_bytes=VMEM_LIMIT)


def _dot(a, b, dims):
    return lax.dot_general(a.astype(MXU_DTYPE), b.astype(MXU_DTYPE), (dims, ((), ())),
                           preferred_element_type=F32)


def _nn(a, b):
    return _dot(a, b, ((1,), (0,)))


def _nt(a, b):
    return _dot(a, b, ((1,), (1,)))


def _tn(a, b):
    return _dot(a, b, ((0,), (0,)))


def _sigmoid(x):
    return 1.0 / (1.0 + jnp.exp(-x))


def _softplus(x):
    return jnp.maximum(x, 0.0) + jnp.log(1.0 + jnp.exp(-jnp.abs(x)))


MATMUL_VMEM_BYTES = 40 * 1024 * 1024


def _matmul_tiles(m, n, k, a_bytes, b_bytes):
    tm = _divisor(m, (1040, 1024, 1408, 768, 640, 512, 384, 256, 128))
    tn = _divisor(n, (1408, 1280, 1024, 768, 512, 256, 128))
    for tk in (5632, 5120, 2816, 2560, 2048, 1664, 1536, 1408, 1280, 1040, 1024, 768, 640, 512, 384, 256, 128):
        if k % tk:
            continue
        need = 2 * (tm * tk * a_bytes + tk * tn * b_bytes + tm * tn * 4) + (tm * tn * 4 if tk < k else 0)
        if need <= MATMUL_VMEM_BYTES:
            return tm, tn, tk
    raise ValueError(f"no matmul tiles for {m}x{n}x{k}")


def _matmul(a, b, *, ta=False, tb=False, name):
    m, k = (a.shape[1], a.shape[0]) if ta else a.shape
    n = b.shape[0] if tb else b.shape[1]
    assert (b.shape[1] if tb else b.shape[0]) == k, (a.shape, b.shape, ta, tb)
    tm, tn, tk = _matmul_tiles(m, n, k, a.dtype.itemsize, b.dtype.itemsize)
    nk = k // tk
    dims = ((0 if ta else 1,), (1 if tb else 0,))

    def body(a_ref, b_ref, o_ref, *acc):
        if nk == 1:
            o_ref[...] = _dot(a_ref[...], b_ref[...], dims)
            return
        acc_ref, kk = acc[0], pl.program_id(2)

        @pl.when(kk == 0)
        def _():
            acc_ref[...] = jnp.zeros_like(acc_ref)

        acc_ref[...] += _dot(a_ref[...], b_ref[...], dims)

        @pl.when(kk == nk - 1)
        def _():
            o_ref[...] = acc_ref[...]

    a_spec = (pl.BlockSpec((tk, tm), lambda i, j, kk: (kk, i)) if ta
              else pl.BlockSpec((tm, tk), lambda i, j, kk: (i, kk)))
    b_spec = (pl.BlockSpec((tn, tk), lambda i, j, kk: (j, kk)) if tb
              else pl.BlockSpec((tk, tn), lambda i, j, kk: (kk, j)))
    return pl.pallas_call(
        body, name=name, grid=(m // tm, n // tn, nk),
        in_specs=[a_spec, b_spec],
        out_specs=pl.BlockSpec((tm, tn), lambda i, j, kk: (i, j)),
        out_shape=jax.ShapeDtypeStruct((m, n), F32),
        scratch_shapes=[pltpu.VMEM((tm, tn), F32)] if nk > 1 else [],
        compiler_params=_params(("parallel", "parallel", "arbitrary")),
    )(a, b)


def _rowwise(fn, rows, fulls, out_rows, out_accs, *, name, out_dtypes=None):
    tp = rows[0][0].shape[0]
    tm = _row_tile(tp, sum(w for _, _, w in rows) + sum(out_rows))
    nr, nf, no, na = len(rows), len(fulls), len(out_rows), len(out_accs)
    out_dtypes = out_dtypes or [F32] * no

    def body(*refs):
        i = pl.program_id(0)
        outs = fn(i * tm, *[r[...].astype(F32) for r in refs[:nr + nf]])
        for k in range(no):
            refs[nr + nf + k][...] = outs[k].astype(out_dtypes[k])
        for k in range(na):
            ref = refs[nr + nf + no + k]

            @pl.when(i == 0)
            def _(ref=ref):
                ref[...] = jnp.zeros_like(ref)

            ref[...] += outs[no + k]

    in_specs = [pl.BlockSpec((tm, w), functools.partial(lambda i, cb: (i, cb), cb=cb)) for _, cb, w in rows]
    in_specs += [pl.BlockSpec(f.shape, lambda i: (0, 0)) for f in fulls]
    out_specs = [pl.BlockSpec((tm, w), lambda i: (i, 0)) for w in out_rows]
    out_specs += [pl.BlockSpec(s, lambda i: (0, 0)) for s in out_accs]
    out_shape = [jax.ShapeDtypeStruct((tp, w), dt) for w, dt in zip(out_rows, out_dtypes)]
    out_shape += [jax.ShapeDtypeStruct(s, F32) for s in out_accs]
    return pl.pallas_call(
        body, name=name, grid=(tp // tm,), in_specs=in_specs, out_specs=out_specs, out_shape=out_shape,
        compiler_params=_params(("arbitrary",)),
    )(*[r[0] for r in rows], *fulls)


def _whole(a):
    return (a, 0, a.shape[1])


def _live(row0, tm):
    return (row0 + lax.broadcasted_iota(jnp.int32, (tm, 1), 0)) >= PAD_FRONT


def _rms_core(x, g):
    r = lax.rsqrt(jnp.mean(x * x, axis=-1, keepdims=True) + EPS)
    return x * r, r


def _rms_fwd(x, g, *, name, res=None, bias=None, out_dtype=F32):
    def fn(row0, *blk):
        it = iter(blk)
        xv = next(it)
        rv = next(it) if res is not None else None
        gv = next(it)
        if bias is not None:
            xv = xv + next(it)
        xh, _ = _rms_core(xv, gv)
        y = jnp.where(_live(row0, xv.shape[0]), xh * gv, 0.0)
        return [y + rv if rv is not None else y]

    rows = [_whole(x)] + ([_whole(res)] if res is not None else [])
    fulls = [g] + ([bias] if bias is not None else [])
    return _rowwise(fn, rows, fulls, [x.shape[1]], [], name=name, out_dtypes=[out_dtype])[0]


def _rms_bwd(x, g, dy, *, name, add=None, bias=None, dy2=None, out_dtype=F32):
    def fn(row0, *blk):
        it = iter(blk)
        xv, dyv = next(it), next(it)
        if dy2 is not None:
            dyv = dyv + next(it)
        av = next(it) if add is not None else None
        gv = next(it)
        if bias is not None:
            xv = xv + next(it)
        dyv = jnp.where(_live(row0, xv.shape[0]), dyv, 0.0)
        xh, r = _rms_core(xv, gv)
        dyg = dyv * gv
        dx = r * (dyg - xh * jnp.mean(dyg * xh, axis=-1, keepdims=True))
        outs = [dx + av if av is not None else dx, jnp.sum(dyv * xh, axis=0, keepdims=True)]
        if bias is not None:
            outs.append(jnp.sum(dx, axis=0, keepdims=True))
        return outs

    rows = [_whole(x), _whole(dy)] + ([_whole(dy2)] if dy2 is not None else []) \
        + ([_whole(add)] if add is not None else [])
    fulls = [g] + ([bias] if bias is not None else [])
    c = x.shape[1]
    return _rowwise(fn, rows, fulls, [c], [(1, c)] * (2 if bias is not None else 1), name=name,
                    out_dtypes=[out_dtype])


def _ffn_tiles(m, n):
    return _divisor(m, (640, 384, 256, 128)), _divisor(n, (1408, 1024, 768, 512, 256, 128))


def _ffn_in(u, wg, wu, *, name):
    (m, k), n = u.shape, wg.shape[1]
    tm, tn = _ffn_tiles(m, n)

    def body(u_ref, wg_ref, wu_ref, a_ref, b_ref, s_ref):
        x = u_ref[...]
        a, b = _nn(x, wg_ref[...]), _nn(x, wu_ref[...])
        a_ref[...] = a
        b_ref[...] = b
        s_ref[...] = (a * _sigmoid(a) * b).astype(s_ref.dtype)

    w_spec = pl.BlockSpec((k, tn), lambda i, j: (0, j))
    o_spec = pl.BlockSpec((tm, tn), lambda i, j: (i, j))
    return pl.pallas_call(
        body, name=name, grid=(m // tm, n // tn),
        in_specs=[pl.BlockSpec((tm, k), lambda i, j: (i, 0)), w_spec, w_spec], out_specs=[o_spec] * 3,
        out_shape=[jax.ShapeDtypeStruct((m, n), F32), jax.ShapeDtypeStruct((m, n), F32),
                   jax.ShapeDtypeStruct((m, n), MXU_DTYPE)],
        compiler_params=_params(("parallel", "parallel")),
    )(u, wg, wu)


def _ffn_down_dx(df, wd, a, b, *, name):
    (m, k), n = df.shape, wd.shape[0]
    tm, tn = _ffn_tiles(m, n)

    def body(d_ref, w_ref, a_ref, b_ref, da_ref, db_ref):
        ds = _nt(d_ref[...], w_ref[...])
        av, bv = a_ref[...], b_ref[...]
        sg = _sigmoid(av)
        da_ref[...] = (ds * bv * sg * (1.0 + av * (1.0 - sg))).astype(da_ref.dtype)
        db_ref[...] = (ds * av * sg).astype(db_ref.dtype)

    t_spec = pl.BlockSpec((tm, tn), lambda i, j: (i, j))
    return pl.pallas_call(
        body, name=name, grid=(m // tm, n // tn),
        in_specs=[pl.BlockSpec((tm, k), lambda i, j: (i, 0)), pl.BlockSpec((tn, k), lambda i, j: (j, 0)), t_spec,
                  t_spec],
        out_specs=[t_spec, t_spec], out_shape=[jax.ShapeDtypeStruct((m, n), MXU_DTYPE)] * 2,
        compiler_params=_params(("parallel", "parallel")),
    )(df, wd, a, b)


def _glu_fwd(z, b, *, name):
    h = z.shape[1] // 2

    def fn(row0, a, gt, bv):
        y = (a + bv[:, :h]) * _sigmoid(gt + bv[:, h:])
        return [jnp.where(_live(row0, a.shape[0]), y, 0.0)]

    return _rowwise(fn, [(z, 0, h), (z, 1, h)], [b], [h], [], name=name)[0]


def _glu_bwd(z, b, dy, *, name):
    h = z.shape[1] // 2

    def fn(row0, a, gt, d, bv):
        d = jnp.where(_live(row0, a.shape[0]), d, 0.0)
        sg = _sigmoid(gt + bv[:, h:])
        dz = jnp.concatenate([d * sg, d * (a + bv[:, :h]) * sg * (1.0 - sg)], axis=1)
        return [dz, jnp.sum(dz, axis=0, keepdims=True)]

    return _rowwise(fn, [(z, 0, h), (z, 1, h), _whole(dy)], [b], [2 * h], [(1, 2 * h)], name=name,
                    out_dtypes=[MXU_DTYPE])


def _ln_core(x):
    mu = jnp.mean(x, axis=-1, keepdims=True)
    xc = x - mu
    r = lax.rsqrt(jnp.mean(xc * xc, axis=-1, keepdims=True) + EPS)
    return xc * r, r


def _lnsilu_fwd(x, g, b, *, name):
    def fn(row0, xv, gv, bv):
        xh, _ = _ln_core(xv)
        v = xh * gv + bv
        return [v * _sigmoid(v)]

    return _rowwise(fn, [_whole(x)], [g, b], [x.shape[1]], [], name=name, out_dtypes=[MXU_DTYPE])[0]


def _lnsilu_bwd(x, g, b, dy, *, name):
    def fn(row0, xv, d, gv, bv):
        xh, r = _ln_core(xv)
        v = xh * gv + bv
        sg = _sigmoid(v)
        dv = d * sg * (1.0 + v * (1.0 - sg))
        dxh = dv * gv
        dx = r * (dxh - jnp.mean(dxh, axis=-1, keepdims=True) - xh * jnp.mean(dxh * xh, axis=-1, keepdims=True))
        return [dx, jnp.sum(dv * xh, axis=0, keepdims=True), jnp.sum(dv, axis=0, keepdims=True)]

    c = x.shape[1]
    return _rowwise(fn, [_whole(x), _whole(dy)], [g, b], [c], [(1, c), (1, c)], name=name)


def _silu_fwd(x, *, name):
    return _rowwise(lambda row0, v: [v * _sigmoid(v)], [_whole(x)], [], [x.shape[1]], [], name=name)[0]


def _silu_bwd(x, dy, *, name):
    def fn(row0, v, d):
        sg = _sigmoid(v)
        return [d * sg * (1.0 + v * (1.0 - sg))]

    return _rowwise(fn, [_whole(x), _whole(dy)], [], [x.shape[1]], [], name=name)[0]


def _gate_parts(row0, pg, gb):
    lane = lax.broadcasted_iota(jnp.int32, pg.shape, 1)
    th = jnp.tanh((pg + gb) / GATE_SOFTCAP)
    s = GATE_SOFTCAP * th
    return lane, th, s, _live(row0, pg.shape[0])


def _gates_fwd(proj, gate_b, *, name):
    def fn(row0, pg, gb):
        lane, th, s, live = _gate_parts(row0, pg, gb)
        li = jnp.where(live, s, NEG)
        lf = jnp.where(live, -_softplus(-s), 0.0)
        return [jnp.where(lane < MLSTM_HEADS, li, jnp.where(lane < 2 * MLSTM_HEADS, lf, 0.0))]

    return _rowwise(fn, [(proj, GATE_COL // LANE, LANE)], [gate_b], [LANE], [], name=name)[0]


def _gates_bwd(proj, gate_b, dgl, *, name):
    def fn(row0, pg, d, gb):
        lane, th, s, live = _gate_parts(row0, pg, gb)
        ds = jnp.where(lane < MLSTM_HEADS, d, d * _sigmoid(-s))
        ds = jnp.where(live & (lane < 2 * MLSTM_HEADS), ds, 0.0)
        dp = ds * (1.0 - th * th)
        return [dp, jnp.sum(dp, axis=0, keepdims=True)]

    return _rowwise(fn, [(proj, GATE_COL // LANE, LANE), _whole(dgl)], [gate_b], [LANE], [(1, LANE)], name=name)


def _head_rms(h):
    parts = [h[:, i * MLSTM_DV:(i + 1) * MLSTM_DV] for i in range(MLSTM_HEADS)]
    rs = [lax.rsqrt(jnp.mean(p * p, axis=-1, keepdims=True) + EPS) for p in parts]
    return parts, rs


def _hnorm_fwd(hm, proj, g, *, name):
    def fn(row0, h, o, gv):
        parts, rs = _head_rms(h)
        xh = jnp.concatenate([p * r for p, r in zip(parts, rs)], axis=1)
        return [xh * gv * _sigmoid(o)]

    return _rowwise(fn, [_whole(hm), (proj, 2, MV)], [g], [MV], [], name=name)[0]


def _hnorm_bwd(hm, proj, g, dmixed, *, name):
    def fn(row0, h, o, d, gv):
        parts, rs = _head_rms(h)
        so = _sigmoid(o)
        dn = d * so
        dxs, xhs = [], []
        for i, (p, r) in enumerate(zip(parts, rs)):
            sl = slice(i * MLSTM_DV, (i + 1) * MLSTM_DV)
            xh = p * r
            dyg = dn[:, sl] * gv[:, sl]
            dxs.append(r * (dyg - xh * jnp.mean(dyg * xh, axis=-1, keepdims=True)))
            xhs.append(xh)
        xh = jnp.concatenate(xhs, axis=1)
        return [jnp.concatenate(dxs, axis=1), d * xh * gv * so * (1.0 - so), jnp.sum(dn * xh, axis=0, keepdims=True)]

    return _rowwise(fn, [_whole(hm), (proj, 2, MV), (dmixed, 0, MV)], [g], [MV, MV], [(1, MV)], name=name)


def _loss_fwd_bwd(h, target, *, name):
    first = PAD_FRONT + N_META

    def fn(row0, hv, tv):
        rows = row0 + lax.broadcasted_iota(jnp.int32, (hv.shape[0], 1), 0)
        e = jnp.where(rows >= first, hv - tv, 0.0)
        return [e * (1.0 / D_MODEL), jnp.sum(e * e, axis=0, keepdims=True)]

    return _rowwise(fn, [_whole(h), _whole(target)], [], [D_MODEL], [(1, D_MODEL)], name=name)


CONV_SUB = 64


def _conv_tiles(tp, c):
    return _row_tile(tp), _divisor(c, (256, 128))


def _conv_shift(win, shifted, tm):
    n = tm + CONV_HALO - SUBLANE
    for b in range(1, SUBLANE):
        shifted[b - 1, :, :] = win[b:b + n, :]


def _conv_window(win, shifted, offset, r0, rows):
    a, b = divmod(offset, SUBLANE)
    lo = a * SUBLANE + r0
    return win[lo:lo + rows, :] if b == 0 else shifted[b - 1, lo:lo + rows, :]


def _conv_fwd(x, w, b, *, name):
    tp, (k, c) = x.shape[0], w.shape
    tm, tc = _conv_tiles(tp, c)
    base = CONV_HALO - (k - 1)

    def body(x_ref, xp_ref, w_ref, b_ref, o_ref, win, shifted):
        i = pl.program_id(1)
        win[0:CONV_HALO, :] = jnp.where(i > 0, _mxu_rounded(xp_ref[tm - CONV_HALO:tm, :]), 0.0)
        win[CONV_HALO:CONV_HALO + tm, :] = _mxu_rounded(x_ref[...])
        _conv_shift(win, shifted, tm)
        for r0 in range(0, tm, CONV_SUB):
            acc = jnp.broadcast_to(b_ref[...], (CONV_SUB, tc))
            for j in range(k):
                acc = acc + _mxu_rounded(w_ref[j:j + 1, :]) *_conv_window(win, shifted, base + j, r0, CONV_SUB)
            o_ref[r0:r0 + CONV_SUB, :] = acc

    return pl.pallas_call(
        body, name=name, grid=(c // tc, tp // tm),
        in_specs=[pl.BlockSpec((tm, tc), lambda cc, i: (i, cc)),
                  pl.BlockSpec((tm, tc), lambda cc, i: (jnp.maximum(i - 1, 0), cc)),
                  pl.BlockSpec((k, tc), lambda cc, i: (0, cc)),
                  pl.BlockSpec((1, tc), lambda cc, i: (0, cc))],
        out_specs=pl.BlockSpec((tm, tc), lambda cc, i: (i, cc)),
        out_shape=jax.ShapeDtypeStruct((tp, c), F32),
        scratch_shapes=[pltpu.VMEM((CONV_HALO + tm, tc), F32),
                        pltpu.VMEM((SUBLANE - 1, CONV_HALO + tm - SUBLANE, tc), F32)],
        compiler_params=_params(("parallel", "arbitrary")),
    )(x, x, w, b)


def _conv_bwd(x, w, dy, *, name):
    tp, (k, c) = x.shape[0], w.shape
    tm, tc = _conv_tiles(tp, c)
    nt = tp // tm
    base = CONV_HALO - (k - 1)

    def body(x_ref, xp_ref, d_ref, dn_ref, w_ref, dx_ref, dw_ref, db_ref, winx, wind, shx, shd):
        i = pl.program_id(1)
        winx[0:CONV_HALO, :] = jnp.where(i > 0, _mxu_rounded(xp_ref[tm - CONV_HALO:tm, :]), 0.0)
        winx[CONV_HALO:CONV_HALO + tm, :] = _mxu_rounded(x_ref[...])
        d = d_ref[...]
        wind[0:tm, :] = _mxu_rounded(d)
        wind[tm:tm + CONV_HALO, :] = jnp.where(i < nt - 1, _mxu_rounded(dn_ref[0:CONV_HALO, :]), 0.0)
        _conv_shift(winx, shx, tm)
        _conv_shift(wind, shd, tm)

        @pl.when(i == 0)
        def _():
            dw_ref[...] = jnp.zeros_like(dw_ref)
            db_ref[...] = jnp.zeros_like(db_ref)

        for r0 in range(0, tm, CONV_SUB):
            acc = jnp.zeros((CONV_SUB, tc), F32)
            for j in range(k):
                acc = acc + _mxu_rounded(w_ref[j:j + 1, :]) *_conv_window(wind, shd, k - 1 - j, r0, CONV_SUB)
            dx_ref[r0:r0 + CONV_SUB, :] = acc
        for j in range(k):
            part = jnp.zeros((SUBLANE, tc), F32)
            for r0 in range(0, tm, CONV_SUB):
                p = wind[r0:r0 + CONV_SUB, :] * _conv_window(winx, shx, base + j, r0, CONV_SUB)
                part = part + jnp.sum(p.reshape(CONV_SUB // SUBLANE, SUBLANE, tc), axis=0)
            dw_ref[j:j + 1, :] += jnp.sum(part, axis=0, keepdims=True)
        db_ref[...] += jnp.sum(d, axis=0, keepdims=True)

    return pl.pallas_call(
        body, name=name, grid=(c // tc, nt),
        in_specs=[pl.BlockSpec((tm, tc), lambda cc, i: (i, cc)),
                  pl.BlockSpec((tm, tc), lambda cc, i: (jnp.maximum(i - 1, 0), cc)),
                  pl.BlockSpec((tm, tc), lambda cc, i: (i, cc)),
                  pl.BlockSpec((tm, tc), lambda cc, i: (jnp.minimum(i + 1, nt - 1), cc)),
                  pl.BlockSpec((k, tc), lambda cc, i: (0, cc))],
        out_specs=[pl.BlockSpec((tm, tc), lambda cc, i: (i, cc)),
                   pl.BlockSpec((k, tc), lambda cc, i: (0, cc)),
                   pl.BlockSpec((1, tc), lambda cc, i: (0, cc))],
        out_shape=[jax.ShapeDtypeStruct((tp, c), F32), jax.ShapeDtypeStruct((k, c), F32),
                   jax.ShapeDtypeStruct((1, c), F32)],
        scratch_shapes=[pltpu.VMEM((CONV_HALO + tm, tc), F32), pltpu.VMEM((CONV_HALO + tm, tc), F32),
                        pltpu.VMEM((SUBLANE - 1, CONV_HALO + tm - SUBLANE, tc), F32),
                        pltpu.VMEM((SUBLANE - 1, CONV_HALO + tm - SUBLANE, tc), F32)],
        compiler_params=_params(("parallel", "arbitrary")),
    )(x, x, dy, dy, w)


def _chunk_masks():
    L = MLSTM_CHUNK
    r = lax.broadcasted_iota(jnp.int32, (L, L), 0)
    c = lax.broadcasted_iota(jnp.int32, (L, L), 1)
    return r == c, c <= r, r <= c


def _to_row(col, eye):
    return jnp.sum(jnp.where(eye, col, 0.0), axis=0, keepdims=True)


def _to_col(row, eye):
    return jnp.sum(jnp.where(eye, row, 0.0), axis=1, keepdims=True)


def _mxu_rounded(x):
    return x.astype(MXU_DTYPE).astype(F32)


def _mlstm_group(nc):
    return _divisor(nc, (5, 4, 3, 2, 1))


def _each(f, *lists):
    return [f(*args) for args in zip(*lists)]


def _rsum(x):
    return jnp.sum(x, axis=1, keepdims=True)


def _csum(x):
    return jnp.sum(x, axis=0, keepdims=True)


def _mlstm_chunk(q, k, v, li_c, lf_c, c_st, n_st, m_st, masks):
    eye, low, up = masks
    li_r = _each(lambda c: _to_row(c, eye), li_c)
    lf_r = _each(lambda c: _to_row(c, eye), lf_c)
    b_c = _each(lambda r: _rsum(jnp.where(low, r, 0.0)), lf_r)
    b_r = _each(lambda c: _csum(jnp.where(up, c, 0.0)), lf_c)
    g = _each(_csum, lf_c)
    dm = _each(lambda bc, br, lr: jnp.where(low, bc - br + lr, NEG), b_c, b_r, li_r)
    inter = _each(lambda bc, m: bc + m, b_c, m_st)
    mt = _each(lambda i, d: jnp.maximum(i, jnp.max(d, axis=1, keepdims=True)), inter, dm)
    wi = _each(lambda d, m: jnp.exp(d - m), dm, mt)
    wint = _each(lambda i, m: jnp.exp(i - m), inter, mt)
    qk_ = _each(_nt, q, k)
    qc = _each(_nn, q, c_st)
    s = _each(lambda a, w: a * w, qk_, wi)
    qn = _each(lambda a, n: _rsum(_mxu_rounded(a) * _mxu_rounded(n)), q, n_st)
    sv = _each(_nn, s, v)
    num = _each(lambda a, w, b: a + w * b, sv, wint, qc)
    den = _each(lambda a, w, b: _rsum(a) + w * b, s, wint, qn)
    floor = _each(lambda m: jnp.exp(-m), mt)
    a_c = _each(lambda gg, b, l: gg - b + l, g, b_c, li_c)
    a_r = _each(lambda gg, b, l: gg - b + l, g, b_r, li_r)
    mnew = _each(lambda gg, m, a: jnp.maximum(gg + m, jnp.max(a, axis=1, keepdims=True)), g, m_st, a_r)
    wa_c = _each(lambda a, m: jnp.exp(a - m), a_c, mnew)
    wc = _each(lambda gg, m, mn: jnp.exp(gg + m - mn), g, m_st, mnew)
    return dict(wi=wi, wint=wint, s=s, qc=qc, qn=qn, num=num, den=den, floor=floor, mnew=mnew, wa_c=wa_c, wc=wc)


def _mlstm_heads(qk_ref, v_ref, gl_ref, rows):
    H, dk, dv = MLSTM_HEADS, MLSTM_DQK, MLSTM_DV
    gates = gl_ref[rows, :]
    return ([qk_ref[rows, h * dk:(h + 1) * dk] * (dk ** -0.5) for h in range(H)],
            [qk_ref[rows, MQK + h * dk:MQK + (h + 1) * dk] for h in range(H)],
            [v_ref[rows, h * dv:(h + 1) * dv] for h in range(H)],
            [gates[:, h:h + 1] for h in range(H)], [gates[:, H + h:H + h + 1] for h in range(H)])


def _mlstm_fwd(qk, proj, gl, *, name):
    tp = qk.shape[0]
    L, H, dk, dv = MLSTM_CHUNK, MLSTM_HEADS, MLSTM_DQK, MLSTM_DV
    nc = tp // L
    G = _mlstm_group(nc)

    def body(qk_ref, v_ref, gl_ref, h_ref, call_ref, nall_ref, mall_ref, c_s, n_s, m_s):
        @pl.when(pl.program_id(0) == 0)
        def _():
            c_s[...] = jnp.zeros_like(c_s)
            n_s[...] = jnp.zeros_like(n_s)
            m_s[...] = jnp.zeros_like(m_s)

        masks = _chunk_masks()
        heads = range(H)
        c_st, n_st, m_row = [c_s[h] for h in heads], [n_s[h] for h in heads], [m_s[h] for h in heads]
        for ci in range(G):
            rows = slice(ci * L, (ci + 1) * L)
            for h in heads:
                call_ref[ci, h], nall_ref[ci, h], mall_ref[ci, h] = c_st[h], n_st[h], m_row[h]
            q, k, v, li, lf = _mlstm_heads(qk_ref, v_ref, gl_ref, rows)
            f = _mlstm_chunk(q, k, v, li, lf, c_st, n_st, [m[:, 0:1] for m in m_row], masks)
            out = _each(lambda a, b, c: a / jnp.maximum(jnp.abs(b), c), f["num"], f["den"], f["floor"])
            for h in heads:
                h_ref[rows, h * dv:(h + 1) * dv] = out[h]
            kv = _each(_tn, _each(lambda a, w: a * w, k, f["wa_c"]), v)
            c_st = _each(lambda w, c, x: w * c + x, f["wc"], c_st, kv)
            n_st = _each(lambda w, n, a, b: w * n + _csum(_mxu_rounded(a) * _mxu_rounded(b)), f["wc"], n_st, k, f["wa_c"])
            m_row = _each(lambda m: jnp.broadcast_to(m, (1, LANE)), f["mnew"])
        for h in heads:
            c_s[h], n_s[h], m_s[h] = c_st[h], n_st[h], m_row[h]

    return pl.pallas_call(
        body, name=name, grid=(nc // G,),
        in_specs=[pl.BlockSpec((G * L, 2 * MQK), lambda i: (i, 0)),
                  pl.BlockSpec((G * L, MV), lambda i: (i, 1)),
                  pl.BlockSpec((G * L, LANE), lambda i: (i, 0))],
        out_specs=[pl.BlockSpec((G * L, MV), lambda i: (i, 0)),
                   pl.BlockSpec((G, H, dk, dv), lambda i: (i, 0, 0, 0)),
                   pl.BlockSpec((G, H, 1, dk), lambda i: (i, 0, 0, 0)),
                   pl.BlockSpec((G, H, 1, LANE), lambda i: (i, 0, 0, 0))],
        out_shape=[jax.ShapeDtypeStruct((tp, MV), F32),
                   jax.ShapeDtypeStruct((nc, H, dk, dv), F32),
                   jax.ShapeDtypeStruct((nc, H, 1, dk), F32),
                   jax.ShapeDtypeStruct((nc, H, 1, LANE), F32)],
        scratch_shapes=[pltpu.VMEM((H, dk, dv), F32), pltpu.VMEM((H, 1, dk), F32), pltpu.VMEM((H, 1, LANE), F32)],
        compiler_params=_params(("arbitrary",)),
    )(qk, proj, gl)


def _mlstm_bwd(qk, proj, gl, dmix, call, nall, mall, *, name):
    tp = qk.shape[0]
    L, H, dk, dv = MLSTM_CHUNK, MLSTM_HEADS, MLSTM_DQK, MLSTM_DV
    nc = tp // L
    G = _mlstm_group(nc)

    def body(qk_ref, v_ref, gl_ref, dh_ref, call_ref, nall_ref, mall_ref, dqk_ref, dv_ref, dgl_ref, dc_s, dn_s):
        @pl.when(pl.program_id(0) == 0)
        def _():
            dc_s[...] = jnp.zeros_like(dc_s)
            dn_s[...] = jnp.zeros_like(dn_s)

        masks = _chunk_masks()
        eye, low, up = masks
        lane = lax.broadcasted_iota(jnp.int32, (L, LANE), 1)
        heads = range(H)
        dcn, dnn = [dc_s[h] for h in heads], [dn_s[h] for h in heads]
        for ci in reversed(range(G)):
            rows = slice(ci * L, (ci + 1) * L)
            c_st = [call_ref[ci, h] for h in heads]
            n_st = [nall_ref[ci, h] for h in heads]
            m_st = [mall_ref[ci, h][:, 0:1] for h in heads]
            q, k, v, li, lf = _mlstm_heads(qk_ref, v_ref, gl_ref, rows)
            dh = [dh_ref[rows, h * dv:(h + 1) * dv] for h in heads]
            f = _mlstm_chunk(q, k, v, li, lf, c_st, n_st, m_st, masks)
            wint, s, wa_c, wc, den, floor = f["wint"], f["s"], f["wa_c"], f["wc"], f["den"], f["floor"]

            r = _each(lambda a, b: 1.0 / jnp.maximum(jnp.abs(a), b), den, floor)
            dnum = _each(lambda a, b: a * b, dh, r)
            dscale = _each(lambda a, b, c: -_rsum(a * b) * c * c, dh, f["num"], r)
            dden = _each(lambda a, b, c: jnp.where(jnp.abs(a) > b, c * jnp.sign(a), 0.0), den, floor, dscale)
            dnv = _each(_nt, dnum, v)
            wd = _each(lambda a, b: a * b, wint, dnum)
            dq_c = _each(_nt, wd, c_st)
            dc_acc = _each(_tn, q, wd)
            dv_s = _each(_tn, s, dnum)
            kd = _each(_nn, k, dcn)
            vd = _each(_nt, v, dcn)
            ds = _each(lambda a, b: a + b, dnv, dden)
            dwint = _each(lambda a, b, c, d: _rsum(a * b) + c * d, dnum, f["qc"], dden, f["qn"])
            dd = _each(lambda a, b: a * b, ds, s)
            da_mat = _each(lambda a, b: a * b, ds, f["wi"])
            dq_a = _each(_nn, da_mat, k)
            dk_a = _each(_tn, da_mat, q)
            dw_n = _each(lambda a, b: a * b, dden, wint)
            dq = _each(lambda a, b, c, d: a + b * c + d, dq_c, dw_n, n_st, dq_a)
            dn_acc = _each(lambda a, b: _csum(a * b), q, dw_n)
            dk_ = _each(lambda a, w, b, c: a + w * (b + c), dk_a, wa_c, vd, dnn)
            dv_ = _each(lambda a, w, b: a + w * b, dv_s, wa_c, kd)
            dwa = _each(lambda a, b, c, d: _rsum(a * b) + _rsum(c * d), kd, v, k, dnn)
            dwc = _each(lambda a, b, c, d: _csum(_rsum(a * b)) + _rsum(c * d), dcn, c_st, dnn, n_st)
            da_c = _each(lambda a, b: a * b, dwa, wa_c)
            dg = _each(lambda a, b, c: _csum(a) + b * c, da_c, dwc, wc)
            dd_cols = _each(_csum, dd)
            db_c = _each(lambda a, b, c, d: a * b + _rsum(c) - d, dwint, wint, dd, da_c)
            db_r = _each(lambda a, b: _to_row(a, eye) - b, db_c, dd_cols)
            dlf = _each(lambda a, b: _rsum(jnp.where(up, a, 0.0)) + b, db_r, dg)
            dli = _each(lambda a, b: a + _to_col(b, eye), da_c, dd_cols)
            dcn = _each(lambda w, a, b: w * a + b, wc, dcn, dc_acc)
            dnn = _each(lambda w, a, b: w * a + b, wc, dnn, dn_acc)

            dgl = jnp.zeros((L, LANE), F32)
            for h in heads:
                dqk_ref[rows, h * dk:(h + 1) * dk] = dq[h] * (dk ** -0.5)
                dqk_ref[rows, MQK + h * dk:MQK + (h + 1) * dk] = dk_[h]
                dv_ref[rows, h * dv:(h + 1) * dv] = dv_[h]
                dgl = dgl + jnp.where(lane == h, dli[h], 0.0) + jnp.where(lane == H + h, dlf[h], 0.0)
            dgl_ref[rows, :] = dgl
        for h in heads:
            dc_s[h], dn_s[h] = dcn[h], dnn[h]

    rev = lambda i: nc // G - 1 - i
    return pl.pallas_call(
        body, name=name, grid=(nc // G,),
        in_specs=[pl.BlockSpec((G * L, 2 * MQK), lambda i: (rev(i), 0)),
                  pl.BlockSpec((G * L, MV), lambda i: (rev(i), 1)),
                  pl.BlockSpec((G * L, LANE), lambda i: (rev(i), 0)),
                  pl.BlockSpec((G * L, MV), lambda i: (rev(i), 0)),
                  pl.BlockSpec((G, H, dk, dv), lambda i: (rev(i), 0, 0, 0)),
                  pl.BlockSpec((G, H, 1, dk), lambda i: (rev(i), 0, 0, 0)),
                  pl.BlockSpec((G, H, 1, LANE), lambda i: (rev(i), 0, 0, 0))],
        out_specs=[pl.BlockSpec((G * L, 2 * MQK), lambda i: (rev(i), 0)),
                   pl.BlockSpec((G * L, MV), lambda i: (rev(i), 0)),
                   pl.BlockSpec((G * L, LANE), lambda i: (rev(i), 0))],
        out_shape=[jax.ShapeDtypeStruct((tp, 2 * MQK), F32), jax.ShapeDtypeStruct((tp, MV), F32),
                   jax.ShapeDtypeStruct((tp, LANE), F32)],
        scratch_shapes=[pltpu.VMEM((H, dk, dv), F32), pltpu.VMEM((H, 1, dk), F32)],
        compiler_params=_params(("arbitrary",)),
    )(qk, proj, gl, dmix, call, nall, mall)


SB_EXP_CAP = 80.0
SB_Q0 = 3 * MV // LANE
SB_K0 = SB_Q0 + SB_HEADS
SB_V0 = SB_K0 + SB_HEADS


def _cumsum_dot(x, tri):
    hi = x.astype(jnp.bfloat16)
    lo = (x - hi.astype(F32)).astype(jnp.bfloat16)
    dims = (((1,), (0,)), ((), ()))
    return (lax.dot_general(hi, tri, dims, preferred_element_type=F32)
            + lax.dot_general(lo, tri, dims, preferred_element_type=F32))


def _sb_query_blocks(nq):
    return _divisor(nq, (5, 4, 3, 2, 1))


def _sb_mask(tile, g, tq):
    t_idx = tile * tq + lax.broadcasted_iota(jnp.int32, (tq, tq), 0)
    s_idx = g * tq + lax.broadcasted_iota(jnp.int32, (tq, tq), 1)
    return (s_idx < t_idx) & (s_idx >= PAD_FRONT)


def _sb_blocks(x):
    return [x[:, k * SB_BLOCK:(k + 1) * SB_BLOCK] for k in range(x.shape[1] // SB_BLOCK)]


def _sb_logits(qb, kg, tri, mask):
    z = _nt(qb, kg) * (SB_DH ** -0.5)
    zc = jnp.minimum(z, SB_EXP_CAP)
    l = (zc - z) - jnp.log(1.0 + jnp.exp(zc))
    if mask is not None:
        l = jnp.where(mask, l, 0.0)
    return z, l, [_cumsum_dot(b, tri) for b in _sb_blocks(l)]


def _sb_weights(z, withins, runs, mask):
    e = jnp.exp(z + jnp.concatenate([w + r for w, r in zip(withins, runs)], axis=1))
    return e if mask is None else jnp.where(mask, e, 0.0)


def _sb_segments(tile, group, descending):
    def diagonal():
        group(tile, True)

    def interior():
        def it(gg, c):
            group(tile - 1 - gg if descending else 1 + gg, False)
            return c
        lax.fori_loop(0, jnp.maximum(tile - 1, 0), it, 0)

    def first():
        @pl.when(tile > 0)
        def _():
            group(0, True)

    for part in ((diagonal, interior, first) if descending else (first, interior, diagonal)):
        part()


def _sb_fwd(proj, *, name):
    tp = proj.shape[0]
    B, H = SB_BLOCK, SB_HEADS
    nq = tp // B
    assert nq <= LANE and B == LANE
    r = _sb_query_blocks(nq)
    tq = r * B

    def body(q_ref, k_ref, v_ref, o_ref, ac_ref, run_s):
        tile = pl.program_id(1)
        qb = q_ref[...].astype(MXU_DTYPE)
        lane = lax.broadcasted_iota(jnp.int32, (tq, LANE), 1)
        tri = (lax.broadcasted_iota(jnp.int32, (B, B), 0) >= lax.broadcasted_iota(jnp.int32, (B, B), 1)
               ).astype(jnp.bfloat16)
        o_ref[...] = jnp.zeros_like(o_ref)
        ac_ref[0, 0] = jnp.zeros((tq, LANE), F32)
        run_s[...] = jnp.zeros_like(run_s)

        def group(g, masked):
            grows = pl.ds(pl.multiple_of(g * tq, tq), tq)
            mask = _sb_mask(tile, g, tq) if masked else None
            z, l, withins = _sb_logits(qb, k_ref[grows, :], tri, mask)
            run, saved, runs = run_s[...], ac_ref[0, 0], [None] * r
            for k in reversed(range(r)):
                runs[k] = run
                saved = jnp.where(lane == g * r + k, run, saved)
                run = run + withins[k][:, 0:1]
            o_ref[...] += _nn(_sb_weights(z, withins, runs, mask), v_ref[grows, :])
            ac_ref[0, 0] = saved
            run_s[...] = run

        _sb_segments(tile, group, descending=True)

    return pl.pallas_call(
        body, name=name, grid=(H, nq // r), scratch_shapes=[pltpu.VMEM((tq, LANE), F32)],
        in_specs=[pl.BlockSpec((tq, SB_DH), lambda h, i: (i, SB_Q0 + h)),
                  pl.BlockSpec((tp, SB_DH), lambda h, i: (0, SB_K0 + h)),
                  pl.BlockSpec((tp, SB_DH), lambda h, i: (0, SB_V0 + h))],
        out_specs=[pl.BlockSpec((tq, SB_DH), lambda h, i: (i, h)),
                   pl.BlockSpec((1, 1, tq, LANE), lambda h, i: (h, i, 0, 0))],
        out_shape=[jax.ShapeDtypeStruct((tp, SBW), F32), jax.ShapeDtypeStruct((H, nq // r, tq, LANE), F32)],
        compiler_params=_params(("parallel", "arbitrary")),
    )(proj, proj, proj)


def _sb_bwd(proj, across, dmix, *, name):
    tp = proj.shape[0]
    B, H = SB_BLOCK, SB_HEADS
    nq = tp // B
    r = _sb_query_blocks(nq)
    tq = r * B
    do0 = MV // LANE

    def body(q_ref, k_ref, v_ref, ac_ref, do_ref, dq_ref, dk_ref, dv_ref, gpre_s):
        tile = pl.program_id(1)

        @pl.when(tile == 0)
        def _():
            dk_ref[...] = jnp.zeros_like(dk_ref)
            dv_ref[...] = jnp.zeros_like(dv_ref)

        dq_ref[...] = jnp.zeros_like(dq_ref)
        gpre_s[...] = jnp.zeros_like(gpre_s)
        qb, dob = q_ref[...].astype(MXU_DTYPE), do_ref[...].astype(MXU_DTYPE)
        lane = lax.broadcasted_iota(jnp.int32, (tq, LANE), 1)
        rr = lax.broadcasted_iota(jnp.int32, (B, B), 0)
        cc = lax.broadcasted_iota(jnp.int32, (B, B), 1)
        tri = (rr >= cc).astype(jnp.bfloat16)
        prefix = (rr <= cc).astype(jnp.bfloat16)
        scale = SB_DH ** -0.5

        def group(g, masked):
            grows = pl.ds(pl.multiple_of(g * tq, tq), tq)
            kg, vg = k_ref[grows, :], v_ref[grows, :]
            mask = _sb_mask(tile, g, tq) if masked else None
            z, l, withins = _sb_logits(qb, kg, tri, mask)
            saved = ac_ref[0, 0]
            runs = [jnp.sum(jnp.where(lane == g * r + k, saved, 0.0), axis=1, keepdims=True) for k in range(r)]
            w = _sb_weights(z, withins, runs, mask)
            dv_ref[grows, :] += _tn(w, dob)
            gw = _nt(dob, vg) * w
            gpre, gcum = gpre_s[...], []
            for gc in [_cumsum_dot(b, prefix) for b in _sb_blocks(gw)]:
                gcum.append(gc + gpre)
                gpre = gpre + gc[:, B - 1:B]
            beta_g = jnp.exp(z + l) * jnp.concatenate(gcum, axis=1)
            if masked:
                beta_g = jnp.where(mask, beta_g, 0.0)
            dz = ((gw - beta_g) * scale).astype(MXU_DTYPE)
            dk_ref[grows, :] += _tn(dz, qb)
            dq_ref[...] += _nn(dz, kg)
            gpre_s[...] = gpre

        _sb_segments(tile, group, descending=False)

    return pl.pallas_call(
        body, name=name, grid=(H, nq // r), scratch_shapes=[pltpu.VMEM((tq, LANE), F32)],
        in_specs=[pl.BlockSpec((tq, SB_DH), lambda h, i: (i, SB_Q0 + h)),
                  pl.BlockSpec((tp, SB_DH), lambda h, i: (0, SB_K0 + h)),
                  pl.BlockSpec((tp, SB_DH), lambda h, i: (0, SB_V0 + h)),
                  pl.BlockSpec((1, 1, tq, LANE), lambda h, i: (h, i, 0, 0)),
                  pl.BlockSpec((tq, SB_DH), lambda h, i: (i, do0 + h))],
        out_specs=[pl.BlockSpec((tq, SB_DH), lambda h, i: (i, h)),
                   pl.BlockSpec((tp, SB_DH), lambda h, i: (0, h)),
                   pl.BlockSpec((tp, SB_DH), lambda h, i: (0, h))],
        out_shape=[jax.ShapeDtypeStruct((tp, SBW), F32)] * 3,
        compiler_params=_params(("parallel", "arbitrary")),
    )(proj, proj, proj, across, dmix)


def _ffn_forward(h, p, tag):
    u = _rms_fwd(h, p["g2"], out_dtype=MXU_DTYPE, name=f"{tag}_ffn_norm")
    a, b, s = _ffn_in(u, p["w_gate"], p["w_up"], name=f"{tag}_ffn_gate_up")
    f = _matmul(s, p["w_down"], name=f"{tag}_ffn_down")
    out = _rms_fwd(f, p["g3"], res=h, name=f"{tag}_ffn_out")
    return out, dict(h=h, u=u, a=a, b=b, s=s, f=f)


def _ffn_backward(dh, p, a, tag):
    df, dg3 = _rms_bwd(a["f"], p["g3"], dh, out_dtype=MXU_DTYPE, name=f"{tag}_ffn_out_bwd")
    da, db = _ffn_down_dx(df, p["w_down"], a["a"], a["b"], name=f"{tag}_ffn_down_dx")
    dw_down = _matmul(a["s"], df, ta=True, name=f"{tag}_ffn_down_dw")
    du_gate = _matmul(da, p["w_gate"], tb=True, name=f"{tag}_ffn_gate_dx")
    du_up = _matmul(db, p["w_up"], tb=True, name=f"{tag}_ffn_up_dx")
    dw_gate = _matmul(a["u"], da, ta=True, name=f"{tag}_ffn_gate_dw")
    dw_up = _matmul(a["u"], db, ta=True, name=f"{tag}_ffn_up_dw")
    dh_in, dg2 = _rms_bwd(a["h"], p["g2"], du_gate, dy2=du_up, add=dh, name=f"{tag}_ffn_norm_bwd")
    return dh_in, dict(g2=dg2, g3=dg3, w_gate=dw_gate, w_up=dw_up, w_down=dw_down)


def _mixer_forward(h, p, tag):
    u = _rms_fwd(h, p["g0"], out_dtype=MXU_DTYPE, name=f"{tag}_mix_norm")
    proj = _matmul(u, p["w_in"], name=f"{tag}_mix_in")
    qc = _conv_fwd(proj, p["qk_w"], p["qk_b"], name=f"{tag}_mix_qkconv")
    qk = _silu_fwd(qc, name=f"{tag}_mix_qkact")
    gl = _gates_fwd(proj, p["gate_b"], name=f"{tag}_mix_gates")
    hm, call, nall, mall = _mlstm_fwd(qk, proj, gl, name=f"{tag}_mlstm")
    hn = _hnorm_fwd(hm, proj, p["hnorm_g"], name=f"{tag}_mix_hnorm")
    hs, across = _sb_fwd(proj, name=f"{tag}_sb")
    mixed = jnp.concatenate([hn, hs], axis=1).astype(MXU_DTYPE)
    y = _matmul(mixed, p["w_out"], name=f"{tag}_mix_out")
    out = _rms_fwd(y, p["g1"], res=h, name=f"{tag}_mix_res")
    return out, dict(h=h, u=u, proj=proj, qc=qc, qk=qk, gl=gl, hm=hm, call=call, nall=nall, mall=mall,
                     across=across, mixed=mixed, y=y)


def _mixer_backward(dh, p, a, tag):
    tp = dh.shape[0]
    dy, dg1 = _rms_bwd(a["y"], p["g1"], dh, out_dtype=MXU_DTYPE, name=f"{tag}_mix_res_bwd")
    dmixed = _matmul(dy, p["w_out"], tb=True, name=f"{tag}_mix_out_dx")
    dw_out = _matmul(a["mixed"], dy, ta=True, name=f"{tag}_mix_out_dw")
    dsq, dsk, dsv = _sb_bwd(a["proj"], a["across"], dmixed, name=f"{tag}_sb_bwd")
    dhm, do, dhg = _hnorm_bwd(a["hm"], a["proj"], p["hnorm_g"], dmixed, name=f"{tag}_mix_hnorm_bwd")
    dqk, dv, dgl = _mlstm_bwd(a["qk"], a["proj"], a["gl"], dhm, a["call"], a["nall"], a["mall"],
                              name=f"{tag}_mlstm_bwd")
    dpg, dgate_b = _gates_bwd(a["proj"], p["gate_b"], dgl, name=f"{tag}_mix_gates_bwd")
    dqc = _silu_bwd(a["qc"], dqk, name=f"{tag}_mix_qkact_bwd")
    dpqk, dqk_w, dqk_b = _conv_bwd(a["proj"], p["qk_w"], dqc, name=f"{tag}_mix_qkconv_bwd")
    dproj = jnp.concatenate(
        [dpqk, dv, do, dsq, dsk, dsv, dpg, jnp.zeros((tp, PROJ_WIDTH - GATE_COL - LANE), F32)], axis=1
    ).astype(MXU_DTYPE)
    du = _matmul(dproj, p["w_in"], tb=True, name=f"{tag}_mix_in_dx")
    dw_in = _matmul(a["u"], dproj, ta=True, name=f"{tag}_mix_in_dw")
    dh_in, dg0 = _rms_bwd(a["h"], p["g0"], du, add=dh, name=f"{tag}_mix_norm_bwd")
    return dh_in, dict(g0=dg0, g1=dg1, w_in=dw_in, qk_w=dqk_w, qk_b=dqk_b, gate_b=dgate_b, hnorm_g=dhg,
                       w_out=dw_out)


def _conformer_forward(h, p, tag):
    u = _rms_fwd(h, p["g0"], out_dtype=MXU_DTYPE, name=f"{tag}_conf_norm")
    z = _matmul(u, p["w_pw1"], name=f"{tag}_conf_pw1")
    y1 = _glu_fwd(z, p["b_pw1"], name=f"{tag}_conf_glu")
    y2 = _conv_fwd(y1, p["w_dw"], p["b_dw"], name=f"{tag}_conf_dw")
    y3 = _lnsilu_fwd(y2, p["ln_g"], p["ln_b"], name=f"{tag}_conf_ln")
    y4 = _matmul(y3, p["w_pw2"], name=f"{tag}_conf_pw2")
    out = _rms_fwd(y4, p["g1"], res=h, bias=p["b_pw2"], name=f"{tag}_conf_res")
    return out, dict(h=h, u=u, z=z, y1=y1, y2=y2, y3=y3, y4=y4)


def _conformer_backward(dh, p, a, tag):
    dy4, dg1, db_pw2 = _rms_bwd(a["y4"], p["g1"], dh, bias=p["b_pw2"], out_dtype=MXU_DTYPE,
                                name=f"{tag}_conf_res_bwd")
    dy3 = _matmul(dy4, p["w_pw2"], tb=True, name=f"{tag}_conf_pw2_dx")
    dw_pw2 = _matmul(a["y3"], dy4, ta=True, name=f"{tag}_conf_pw2_dw")
    dy2, dln_g, dln_b = _lnsilu_bwd(a["y2"], p["ln_g"], p["ln_b"], dy3, name=f"{tag}_conf_ln_bwd")
    dy1, dw_dw, db_dw = _conv_bwd(a["y1"], p["w_dw"], dy2, name=f"{tag}_conf_dw_bwd")
    dz, db_pw1 = _glu_bwd(a["z"], p["b_pw1"], dy1, name=f"{tag}_conf_glu_bwd")
    du = _matmul(dz, p["w_pw1"], tb=True, name=f"{tag}_conf_pw1_dx")
    dw_pw1 = _matmul(a["u"], dz, ta=True, name=f"{tag}_conf_pw1_dw")
    dh_in, dg0 = _rms_bwd(a["h"], p["g0"], du, add=dh, name=f"{tag}_conf_norm_bwd")
    return dh_in, dict(g0=dg0, g1=dg1, w_pw1=dw_pw1, b_pw1=db_pw1, w_dw=dw_dw, b_dw=db_dw, ln_g=dln_g,
                       ln_b=dln_b, w_pw2=dw_pw2, b_pw2=db_pw2)


def _trunk_step(h0, target, w, later=None, early=None):
    acts, layers = [], []
    h = h0
    for li in range(DEPTH):
        if li == 1 and later is not None:
            w = later(w, h)
        tag, p = f"l{li}", _layer_operands(w, li)
        h, a_mix = (_mixer_forward if li % 2 == 0 else _conformer_forward)(h, p["mix"], tag)
        h, a_ffn = _ffn_forward(h, p["ffn"], tag)
        acts.append((a_mix, a_ffn))
        layers.append(p)
    dh, loss_cols = _loss_fwd_bwd(h, target, name="loss")
    grads = [None] * len(layers)
    for li in reversed(range(len(layers))):
        tag = f"l{li}"
        p = layers[li]
        if li == 0 and early is not None:
            p = dict(p, ffn=dict(p["ffn"], g3=p["ffn"]["g3"] + early(grads)[0, 0]))
        dh, g_ffn = _ffn_backward(dh, p["ffn"], acts[li][1], tag)
        dh, g_mix = (_mixer_backward if li % 2 == 0 else _conformer_backward)(dh, p["mix"], acts[li][0], tag)
        grads[li] = dict(mix=g_mix, ffn=g_ffn)
    return loss_cols, dh, grads


_SPLIT = 2 * MQK + 2 * MV


def _layer_operands(w, li):
    row = lambda v: v[None, :].astype(F32)
    i = li // 2
    g = w["norm_g"][li].astype(F32)
    if li % 2 == 0:
        win = w["mix_w_in"][i]
        w_in = jnp.concatenate(
            [win[:, :_SPLIT], win[:, _SPLIT + 2 * MLSTM_HEADS:], win[:, _SPLIT:_SPLIT + 2 * MLSTM_HEADS],
             jnp.zeros((D_MODEL, PROJ_WIDTH - IN_WIDTH), win.dtype)], axis=1)
        gate_b = jnp.pad(row(w["mix_gate_b"][i]), ((0, 0), (0, LANE - 2 * MLSTM_HEADS)))
        mix = dict(g0=g[0:1], g1=g[1:2], w_in=w_in, qk_w=w["mix_qk_conv_w"][i].astype(F32),
                   qk_b=row(w["mix_qk_conv_b"][i]), gate_b=gate_b, hnorm_g=row(w["mix_hnorm_g"][i]),
                   w_out=w["mix_w_out"][i])
    else:
        mix = dict(g0=g[0:1], g1=g[1:2], w_pw1=w["conv_w_pw1"][i], b_pw1=row(w["conv_b_pw1"][i]),
                   w_dw=w["conv_w_dw"][i].astype(F32), b_dw=row(w["conv_b_dw"][i]),
                   ln_g=row(w["conv_ln_g"][i]), ln_b=row(w["conv_ln_b"][i]), w_pw2=w["conv_w_pw2"][i],
                   b_pw2=row(w["conv_b_pw2"][i]))
    ffn = dict(g2=g[2:3], g3=g[3:4], w_gate=w["ffn_w_gate"][li], w_up=w["ffn_w_up"][li],
               w_down=w["ffn_w_down"][li])
    return dict(mix=mix, ffn=ffn)


def _collect_grads(grads, layers=tuple(range(DEPTH)), vectors=True):
    even = [grads[li]["mix"] for li in layers if li % 2 == 0]
    odd = [grads[li]["mix"] for li in layers if li % 2 == 1]
    ffn = [grads[li]["ffn"] for li in layers]
    st = lambda xs: jnp.stack(xs, axis=0)
    vec = lambda xs, k: st([x[k][0] for x in xs])
    out = {}
    if even:
        out["mix_w_in"] = st([jnp.concatenate(
            [g["w_in"][:, :_SPLIT], g["w_in"][:, GATE_COL:GATE_COL + 2 * MLSTM_HEADS], g["w_in"][:, _SPLIT:GATE_COL]],
            axis=1) for g in even])
        out["mix_w_out"] = st([g["w_out"] for g in even])
    if odd:
        out["conv_w_pw1"] = st([g["w_pw1"] for g in odd])
        out["conv_w_pw2"] = st([g["w_pw2"] for g in odd])
    out["ffn_w_gate"] = st([g["w_gate"] for g in ffn])
    out["ffn_w_up"] = st([g["w_up"] for g in ffn])
    out["ffn_w_down"] = st([g["w_down"] for g in ffn])
    if vectors:
        out["norm_g"] = st([jnp.concatenate([grads[li]["mix"]["g0"], grads[li]["mix"]["g1"], grads[li]["ffn"]["g2"],
                                             grads[li]["ffn"]["g3"]], axis=0) for li in layers])
        out["mix_qk_conv_w"] = st([g["qk_w"] for g in even])
        out["mix_qk_conv_b"] = vec(even, "qk_b")
        out["mix_gate_b"] = st([g["gate_b"][0, :2 * MLSTM_HEADS] for g in even])
        out["mix_hnorm_g"] = vec(even, "hnorm_g")
        out["conv_b_pw1"] = vec(odd, "b_pw1")
        out["conv_w_dw"] = st([g["w_dw"] for g in odd])
        out["conv_b_dw"] = vec(odd, "b_dw")
        out["conv_ln_g"] = vec(odd, "ln_g")
        out["conv_ln_b"] = vec(odd, "ln_b")
        out["conv_b_pw2"] = vec(odd, "b_pw2")
    return out


def _local_step(x, target, w, later=None, early=None):
    seq = x.shape[0]
    h0 = jnp.concatenate([jnp.zeros((PAD_FRONT, D_MODEL), F32), w["meta"].astype(F32), x], axis=0)
    tgt = jnp.concatenate([jnp.zeros((PAD_FRONT + N_META, D_MODEL), F32), target], axis=0)
    loss_cols, dh0, grads = _trunk_step(h0, tgt, w, later, early)
    out = _collect_grads(grads)
    out["meta"] = dh0[PAD_FRONT:PAD_FRONT + N_META]
    loss = 0.5 * jnp.sum(loss_cols) / D_MODEL
    return loss, dh0[PAD_FRONT + N_META:PAD_FRONT + N_META + seq], out


def _elementwise(fn, arrays, out_dtypes, *, name):
    shape = arrays[0].shape
    cols = shape[-1]
    rows = 1
    for s in shape[:-1]:
        rows *= s
    flat = [a.reshape(rows, cols) for a in arrays]
    if rows * cols * 4 <= (1 << 20) or rows % SUBLANE:
        tr = rows
    else:
        tr = _divisor(rows, (512, 256, 128, 64, 32, 16, 8))
    n = len(flat)

    def body(*refs):
        outs = fn(*[r[...] for r in refs[:n]])
        for o_ref, o in zip(refs[n:], outs):
            o_ref[...] = o.astype(o_ref.dtype)

    spec = pl.BlockSpec((tr, cols), lambda i: (i, 0))
    outs = pl.pallas_call(
        body, name=name, grid=(rows // tr,), in_specs=[spec] * n, out_specs=[spec] * len(out_dtypes),
        out_shape=[jax.ShapeDtypeStruct((rows, cols), dt) for dt in out_dtypes],
        compiler_params=_params(("parallel",)),
    )(*flat)
    return [o.reshape(shape) for o in outs]


def _adamw(w, g, m, v, *, name):
    def fn(wv, gv, mv, vv):
        mn = ADAM_B1 * mv + (1.0 - ADAM_B1) * gv
        vn = ADAM_B2 * vv + (1.0 - ADAM_B2) * (gv * gv)
        m_hat = mn / (1.0 - ADAM_B1 ** ADAM_STEP)
        v_hat = vn / (1.0 - ADAM_B2 ** ADAM_STEP)
        return [-ADAM_LR * (m_hat / (jnp.sqrt(v_hat) + ADAM_EPS) + ADAM_WD * wv), mn, vn]

    return _elementwise(fn, [w, g, m, v], [F32, F32, F32], name=name)


MESH_ID = pl.DeviceIdType.MESH
ANY = pl.BlockSpec(memory_space=pl.ANY)


def _place():
    x, y, c = lax.axis_index("x"), lax.axis_index("y"), lax.axis_index("c")
    return x, y, c, [(1 - x, y), (x, 1 - y), (1 - x, 1 - y)]


def _remote(src, dst, send_sems, recv_sems, k, to):
    return pltpu.make_async_remote_copy(src_ref=src, dst_ref=dst, send_sem=send_sems.at[k], recv_sem=recv_sems.at[k],
                                        device_id=to, device_id_type=MESH_ID)


def _comm_call(body, arrays, out_shapes, n_remote, n_local, name):
    return pl.pallas_call(
        body, name=name, in_specs=[ANY] * len(arrays), out_specs=[ANY] * len(out_shapes), out_shape=out_shapes,
        scratch_shapes=[pltpu.SemaphoreType.DMA((n_remote,)), pltpu.SemaphoreType.DMA((n_remote,)),
                        pltpu.SemaphoreType.DMA((n_local,))],
        compiler_params=pltpu.CompilerParams(has_side_effects=True),
    )(*arrays)


def _gather_chips(shards, *, name):
    n = len(shards)

    def body(*refs):
        ins, outs = refs[:n], refs[n:2 * n]
        send_sems, recv_sems, local_sems = refs[2 * n:]
        x, y, c, chips = _place()
        me, sibling = 2 * x + y, (x, y, 1 - c)

        def half(a, slot, hc):
            hl = ins[a].shape[0] // 2
            return outs[a].at[slot].at[pl.ds(hc * hl, hl)]

        def mine(a):
            hl = ins[a].shape[0] // 2
            return ins[a].at[pl.ds(c * hl, hl)]

        sent = []
        for a in range(n):
            for j, (px, py) in enumerate(chips):
                sent.append(_remote(mine(a), half(a, me, c), send_sems, recv_sems, 6 * a + j, (px, py, c)))
                sent[-1].start()
        for a in range(n):
            for j, (px, py) in enumerate(chips):
                slot = 2 * px + py
                _remote(mine(a), half(a, slot, c), send_sems, recv_sems, 6 * a + j, (px, py, c)).wait_recv()
                sent.append(_remote(half(a, slot, c), half(a, slot, c), send_sems, recv_sems, 6 * a + 3 + j, sibling))
                sent[-1].start()
        for a in range(n):
            for j, (px, py) in enumerate(chips):
                slot = 2 * px + py
                _remote(mine(a), half(a, slot, 1 - c), send_sems, recv_sems, 6 * a + 3 + j, sibling).wait_recv()
        for cp in sent:
            cp.wait_send()

    out_shapes = [jax.ShapeDtypeStruct((4,) + s.shape, s.dtype) for s in shards]
    return _comm_call(body, shards, out_shapes, 6 * n, 1, name)


def _swap_siblings(arrays, *, by_core, name):
    n = len(arrays)

    def body(*refs):
        ins, outs = refs[:n], refs[n:2 * n]
        send_sems, recv_sems, _ = refs[2 * n:]
        x, y, c, _chips = _place()
        cps = [_remote(ins[a].at[1 - c] if by_core else ins[a], outs[a], send_sems, recv_sems, a, (x, y, 1 - c))
               for a in range(n)]
        for cp in cps:
            cp.start()
        for cp in cps:
            cp.wait()

    out_shapes = [jax.ShapeDtypeStruct(a.shape[1:] if by_core else a.shape, a.dtype) for a in arrays]
    return _comm_call(body, arrays, out_shapes, n, 1, name)


HBM = pl.BlockSpec(memory_space=pltpu.HBM)
SEM = pl.BlockSpec(memory_space=pltpu.SEMAPHORE)


def _gather_start(shards, after, *, name):
    n = len(shards)
    lands = [lax.empty((4,) + s.shape[1:], s.dtype) for s in shards]

    def body(*refs):
        ins, land = refs[:n], refs[n:2 * n]
        send_sems, recv_sems, token = refs[2 * n + 1], refs[2 * n + 2], refs[-1]
        x, y, c, chips = _place()
        for a in range(n):
            for j, (px, py) in enumerate(chips):
                _remote(ins[a].at[c], land[a].at[2 * x + y], send_sems, recv_sems, 3 * a + j, (px, py, c)).start()
        token[...] = jnp.zeros_like(token)

    hbm = lambda a: pltpu.with_memory_space_constraint(a, pltpu.HBM)
    out = pl.pallas_call(
        body, name=name,
        out_shape=(pltpu.SemaphoreType.DMA((3 * n,)), pltpu.SemaphoreType.DMA((3 * n,)),
                   *[pltpu.HBM(a.shape, a.dtype) for a in shards + lands], jax.ShapeDtypeStruct((SUBLANE, LANE), F32)),
        in_specs=[HBM] * (2 * n) + [ANY],
        out_specs=(SEM, SEM, *[HBM] * (2 * n), pl.BlockSpec(memory_space=pltpu.VMEM)),
        input_output_aliases={a: 2 + a for a in range(2 * n)},
        compiler_params=pltpu.CompilerParams(has_side_effects=pltpu.SideEffectType.DATAFLOW_SIDE_EFFECTING),
    )(*[hbm(a) for a in shards + lands], after)
    return out[0], out[1], list(out[2:2 + n]), list(out[2 + n:2 + 2 * n]), out[-1]


def _gather_wait(send_sems, recv_sems, shards, lands, after, *, name):
    n = len(shards)

    def body(*refs):
        ins, land, ssem, rsem = refs[:n], refs[n:2 * n], refs[2 * n], refs[2 * n + 1]
        x, y, c, chips = _place()
        for a in range(n):
            for j, (px, py) in enumerate(chips):
                cp = _remote(ins[a].at[c], land[a].at[2 * px + py], ssem, rsem, 3 * a + j, (px, py, c))
                cp.wait_send()
                cp.wait_recv()

    out = pl.pallas_call(
        body, name=name, out_shape=[pltpu.HBM(a.shape, a.dtype) for a in shards + lands],
        in_specs=[HBM] * (2 * n) + [SEM, SEM, ANY], out_specs=[HBM] * (2 * n),
        input_output_aliases={a: a for a in range(2 * n)},
        compiler_params=pltpu.CompilerParams(has_side_effects=pltpu.SideEffectType.DATAFLOW_SIDE_EFFECTING),
    )(*shards, *lands, send_sems, recv_sems, after)
    return list(out[n:])


def _scatter_chips(parts, small, *, name):
    n = len(parts)

    def body(*refs):
        ins, small_in = refs[:n], refs[n]
        outs, small_out = refs[n + 1:2 * n + 1], refs[2 * n + 1]
        send_sems, recv_sems, local_sems = refs[2 * n + 2:]
        x, y, c, chips = _place()
        me8 = 4 * x + 2 * y + c
        own = pltpu.make_async_copy(small_in, small_out.at[me8], local_sems.at[0])
        own.start()
        cps = []
        for fx in range(2):
            for fy in range(2):
                for fc in range(2):
                    r = 4 * fx + 2 * fy + fc - 1
                    if r >= 0:
                        to = (x + fx - 2 * x * fx, y + fy - 2 * y * fy, c + fc - 2 * c * fc)
                        cps.append(_remote(small_in, small_out.at[me8], send_sems, recv_sems, r, to))
        for a in range(n):
            for j, (px, py) in enumerate(chips):
                cps.append(_remote(ins[a].at[2 * px + py], outs[a].at[j], send_sems, recv_sems, 7 + 3 * a + j,
                                   (px, py, c)))
        for cp in cps:
            cp.start()
        for cp in cps:
            cp.wait()
        own.wait()

    out_shapes = [jax.ShapeDtypeStruct((3,) + p.shape[1:], p.dtype) for p in parts]
    out_shapes.append(jax.ShapeDtypeStruct((8,) + small.shape, small.dtype))
    return _comm_call(body, list(parts) + [small], out_shapes, 7 + 3 * n, 1, name)


WEIGHTS = ("meta", "norm_g", "mix_w_in", "mix_qk_conv_w", "mix_qk_conv_b", "mix_gate_b", "mix_hnorm_g", "mix_w_out",
           "conv_w_pw1", "conv_b_pw1", "conv_w_dw", "conv_b_dw", "conv_ln_g", "conv_ln_b", "conv_w_pw2",
           "conv_b_pw2", "ffn_w_gate", "ffn_w_up", "ffn_w_down")
SHARD_AXIS = dict(meta=1, norm_g=2, mix_w_in=2, mix_qk_conv_w=2, mix_qk_conv_b=None, mix_gate_b=None,
                  mix_hnorm_g=None, mix_w_out=1, conv_w_pw1=2, conv_b_pw1=1, conv_w_dw=2, conv_b_dw=1, conv_ln_g=1,
                  conv_ln_b=1, conv_w_pw2=1, conv_b_pw2=1, ffn_w_gate=2, ffn_w_up=2, ffn_w_down=1)
MATRICES = ("mix_w_in", "mix_w_out", "conv_w_pw1", "conv_w_pw2", "ffn_w_gate", "ffn_w_up", "ffn_w_down")
VECTORS = tuple(n for n in WEIGHTS if n not in MATRICES)
GATHER_COLS = D_MODEL // 4


def _pack_rows(arrays, cols, pad_to):
    rows = [a.astype(F32).reshape(-1) for a in arrays]
    rows = [jnp.pad(r, (0, (-r.shape[0]) % cols)).reshape(-1, cols) for r in rows]
    packed = jnp.concatenate(rows, axis=0)
    return jnp.pad(packed, ((0, pad_to - packed.shape[0]), (0, 0))), [r.shape[0] for r in rows]


def _unpack_rows(packed, counts, shapes):
    out, at = [], 0
    for n, shape in zip(counts, shapes):
        size = 1
        for s in shape:
            size *= s
        out.append(packed[..., at:at + n, :].reshape(packed.shape[:-2] + (-1,))[..., :size]
                   .reshape(packed.shape[:-2] + tuple(shape)))
        at += n
    return out


FIRST_LAYER = ("mix_w_in", "mix_w_out", "ffn_w_gate", "ffn_w_up", "ffn_w_down")
PACK_ROWS = 128


def _row_halves(a):
    return a.reshape((2, a.shape[0] * a.shape[1] // 2, a.shape[2]))


def _assemble(slots, own, name):
    me = 2 * lax.axis_index("x") + lax.axis_index("y")
    slots = lax.dynamic_update_index_in_dim(slots, own, me, 0)
    return jnp.concatenate([slots[k] for k in range(4)], axis=SHARD_AXIS[name])


def _gather_weights(local):
    c = lax.axis_index("c")
    sharded_vecs = [n for n in VECTORS if SHARD_AXIS[n] is not None]
    pack, counts = _pack_rows([local[n] for n in sharded_vecs], GATHER_COLS, PACK_ROWS)
    first = [local[n][:1].astype(MXU_DTYPE) for n in FIRST_LAYER]
    got = _gather_chips([_row_halves(s) for s in first] + [_row_halves(pack[None])], name="gather_first")
    w = {n: local[n] for n in VECTORS if SHARD_AXIS[n] is None}
    for n, s, g in zip(FIRST_LAYER, first, got):
        w[n] = [_assemble(g.reshape((4,) + s.shape), s, n)[0]]
    packs = lax.dynamic_update_index_in_dim(got[-1].reshape(4, PACK_ROWS, GATHER_COLS), pack,
                                            2 * lax.axis_index("x") + lax.axis_index("y"), 0)
    vecs = _unpack_rows(packs, counts, [local[n].shape for n in sharded_vecs])
    for n, v in zip(sharded_vecs, vecs):
        w[n] = jnp.moveaxis(v, 0, -2).reshape(v.shape[1:-1] + (4 * v.shape[-1],))

    rest = [(local[n][1:] if n in FIRST_LAYER else local[n]).astype(MXU_DTYPE) for n in MATRICES]
    send_sems, recv_sems, sent, lands, token = _gather_start([_row_halves(r) for r in rest], got[-1],
                                                             name="gather_rest_start")

    def later(w, after):
        mine = _gather_wait(send_sems, recv_sems, sent, lands, after, name="gather_rest_wait")
        theirs = _swap_siblings(mine, by_core=False, name="gather_rest_swap")
        w = dict(w)
        for n, r, m, t in zip(MATRICES, rest, mine, theirs):
            both = jnp.where(c == 0, jnp.concatenate([m, t], axis=1), jnp.concatenate([t, m], axis=1))
            layers = _assemble(both.reshape((4,) + r.shape), r, n)
            w[n] = list(w.get(n, [])) + [layers[k] for k in range(r.shape[0])]
        return w

    return w, later, token


def _pair_sums(grads, tag):
    c = lax.axis_index("c")
    stacked = []
    for n, g in grads.items():
        parts = jnp.stack(jnp.split(g, 4, axis=SHARD_AXIS[n]), axis=0)
        parts = parts.reshape(4, 2, parts.shape[1] * parts.shape[2] // 2, parts.shape[3])
        stacked.append(jnp.swapaxes(parts, 0, 1))
    theirs = _swap_siblings(stacked, by_core=True, name=f"reduce_{tag}_pair_swap")
    both = [_elementwise(lambda a, b: [a + b, a + b], [lax.dynamic_index_in_dim(s, c, 0, keepdims=False), t],
                         [F32, jnp.bfloat16], name=f"reduce_{tag}_pair_sum_{n}")
            for n, s, t in zip(grads, stacked, theirs)]
    return [b[0] for b in both], [b[1] for b in both]


def _chip_sums(grads, pair, got, tag):
    c = lax.axis_index("c")
    me = 2 * lax.axis_index("x") + lax.axis_index("y")
    halves = []
    for n, p, r in zip(grads, pair, got):
        own = lax.dynamic_index_in_dim(p, me, 0, keepdims=False)
        halves.append(_elementwise(lambda a, b0, b1, b2: [((a + b0.astype(F32)) + b1.astype(F32)) + b2.astype(F32)],
                                   [own, r[0], r[1], r[2]], [F32], name=f"reduce_{tag}_chip_sum_{n}")[0])
    others = _swap_siblings(halves, by_core=False, name=f"reduce_{tag}_join")
    out = {}
    for (n, g), h, o in zip(grads.items(), halves, others):
        rows = jnp.where(c == 0, jnp.concatenate([h, o], axis=0), jnp.concatenate([o, h], axis=0))
        shard = list(g.shape)
        shard[SHARD_AXIS[n]] //= 4
        out[n] = rows.reshape(shard)
    return out


def _scatter_start(parts, after, *, name):
    n = len(parts)
    lands = [lax.empty((3,) + p.shape[1:], p.dtype) for p in parts]

    def body(*refs):
        ins, land = refs[:n], refs[n:2 * n]
        send_sems, recv_sems, token = refs[2 * n + 1], refs[2 * n + 2], refs[-1]
        x, y, c, chips = _place()
        for a in range(n):
            for j, (px, py) in enumerate(chips):
                _remote(ins[a].at[2 * px + py], land[a].at[j], send_sems, recv_sems, 3 * a + j, (px, py, c)).start()
        token[...] = jnp.zeros_like(token)

    hbm = lambda a: pltpu.with_memory_space_constraint(a, pltpu.HBM)
    out = pl.pallas_call(
        body, name=name,
        out_shape=(pltpu.SemaphoreType.DMA((3 * n,)), pltpu.SemaphoreType.DMA((3 * n,)),
                   *[pltpu.HBM(a.shape, a.dtype) for a in parts + lands], jax.ShapeDtypeStruct((SUBLANE, LANE), F32)),
        in_specs=[HBM] * (2 * n) + [ANY],
        out_specs=(SEM, SEM, *[HBM] * (2 * n), pl.BlockSpec(memory_space=pltpu.VMEM)),
        input_output_aliases={a: 2 + a for a in range(2 * n)},
        compiler_params=pltpu.CompilerParams(has_side_effects=pltpu.SideEffectType.DATAFLOW_SIDE_EFFECTING),
    )(*[hbm(a) for a in parts + lands], after)
    return out[0], out[1], list(out[2:2 + n]), list(out[2 + n:2 + 2 * n]), out[-1]


def _scatter_wait(send_sems, recv_sems, parts, lands, after, *, name):
    n = len(parts)

    def body(*refs):
        ins, land, ssem, rsem = refs[:n], refs[n:2 * n], refs[2 * n], refs[2 * n + 1]
        x, y, c, chips = _place()
        for a in range(n):
            for j, (px, py) in enumerate(chips):
                cp = _remote(ins[a].at[2 * px + py], land[a].at[j], ssem, rsem, 3 * a + j, (px, py, c))
                cp.wait_send()
                cp.wait_recv()

    out = pl.pallas_call(
        body, name=name, out_shape=[pltpu.HBM(a.shape, a.dtype) for a in parts + lands],
        in_specs=[HBM] * (2 * n) + [SEM, SEM, ANY], out_specs=[HBM] * (2 * n),
        input_output_aliases={a: a for a in range(2 * n)},
        compiler_params=pltpu.CompilerParams(has_side_effects=pltpu.SideEffectType.DATAFLOW_SIDE_EFFECTING),
    )(*parts, *lands, send_sems, recv_sems, after)
    return list(out[n:])


def _reduce_rest_start(grads_by_layer):
    grads = _collect_grads(grads_by_layer, layers=tuple(range(1, DEPTH)), vectors=False)
    pair, pair16 = _pair_sums(grads, "rest")
    send_sems, recv_sems, sent, lands, token = _scatter_start(pair16, pair[0], name="reduce_rest_start")
    return (grads, pair, send_sems, recv_sems, sent, lands), token


def _reduce_rest_finish(state, after):
    grads, pair, send_sems, recv_sems, sent, lands = state
    got = _scatter_wait(send_sems, recv_sems, sent, lands, after, name="reduce_rest_wait")
    return _chip_sums(grads, pair, got, "rest")


def _reduce_grads(grads, rest):
    me = 2 * lax.axis_index("x") + lax.axis_index("y")
    first = {n: grads[n][:1] for n in FIRST_LAYER}
    pair, pair16 = _pair_sums(first, "first")
    shapes = [grads[n].shape for n in VECTORS]
    pack, counts = _pack_rows([grads[n] for n in VECTORS], D_MODEL, 120)
    got = _scatter_chips(pair16, pack, name="reduce_chips")
    out = _chip_sums(first, pair, got[:-1], "first")
    out = {n: (jnp.concatenate([out[n], rest[n]], axis=0) if n in out else rest[n]) for n in MATRICES}
    small = got[-1]
    total = _elementwise(lambda *s: [functools.reduce(lambda a, b: a + b, s)], [small[k] for k in range(8)], [F32],
                         name="reduce_small_sum")[0]
    for n, v in zip(VECTORS, _unpack_rows(total, counts, shapes)):
        ax = SHARD_AXIS[n]
        if ax is not None:
            w = v.shape[ax] // 4
            v = lax.dynamic_slice_in_dim(v, me * w, w, axis=ax)
        out[n] = v
    return out


def kernel(x, meta, norm_g, mix_w_in, mix_qk_conv_w, mix_qk_conv_b, mix_gate_b, mix_hnorm_g, mix_w_out, conv_w_pw1, conv_b_pw1, conv_w_dw, conv_b_dw, conv_ln_g, conv_ln_b, conv_w_pw2, conv_b_pw2, ffn_w_gate, ffn_w_up, ffn_w_down, loss_target, m_meta, m_norm_g, m_mix_w_in, m_mix_qk_conv_w, m_mix_qk_conv_b, m_mix_gate_b, m_mix_hnorm_g, m_mix_w_out, m_conv_w_pw1, m_conv_b_pw1, m_conv_w_dw, m_conv_b_dw, m_conv_ln_g, m_conv_ln_b, m_conv_w_pw2, m_conv_b_pw2, m_ffn_w_gate, m_ffn_w_up, m_ffn_w_down, v_meta, v_norm_g, v_mix_w_in, v_mix_qk_conv_w, v_mix_qk_conv_b, v_mix_gate_b, v_mix_hnorm_g, v_mix_w_out, v_conv_w_pw1, v_conv_b_pw1, v_conv_w_dw, v_conv_b_dw, v_conv_ln_g, v_conv_ln_b, v_conv_w_pw2, v_conv_b_pw2, v_ffn_w_gate, v_ffn_w_up, v_ffn_w_down):
    given = dict(locals())
    local = {n: given[n] for n in WEIGHTS}
    first, later, token = _gather_weights(local)
    in_flight = []

    def early(grads_by_layer):
        state, started = _reduce_rest_start(grads_by_layer)
        in_flight.append(state)
        return started

    first = dict(first, meta=first["meta"] + token[0, 0])
    loss, grad_x, grads = _local_step(x[0], loss_target[0], first, later, early)
    loss = lax.psum(loss, ("x", "y", "c"))
    grad_w = _reduce_grads(grads, _reduce_rest_finish(in_flight[0], grad_x))
    delta, new_m, new_v = {}, {}, {}
    for n in WEIGHTS:
        delta[n], new_m[n], new_v[n] = _adamw(local[n], grad_w[n], given["m_" + n], given["v_" + n], name=f"adamw_{n}")
    return (loss, grad_x[None], *[grad_w[n] for n in WEIGHTS], *[delta[n] for n in WEIGHTS],
            *[new_m[n] for n in WEIGHTS], *[new_v[n] for n in WEIGHTS])
```
